```python
import math
import jax, jax.numpy as jnp
from jax import lax
import numpy as np

D_MODEL = 2048
BATCH = 1
SEQ = 16384
DEPTH = 4
DEC_BATCH = 8
DEC_SEQ = 64
PAST_LEN = 2048

CHUNK = 64
N_MIXERS = 3
ROPE_THETA = 10000.0
EPS = 1e-6

A_HD = 64
A_VD = 2 * A_HD
A_HEADS = D_MODEL // A_VD
A_W = A_HEADS * A_VD
Q_BLOCK = 128

R_DK = 256
R_HEADS = D_MODEL // R_DK
R_DV = 2 * R_DK
R_QK = R_HEADS * R_DK
R_V = R_HEADS * R_DV

M_CHUNK = 128
M_GROUPS = 8
M_W = 2 * D_MODEL
M_GD = M_W // M_GROUPS

N_A = (DEPTH + 2) // 3
N_B = (DEPTH + 1) // 3
N_C = DEPTH // 3

kernel_name = "hybrid_streaming_encoder_step"


def rms_norm(x, w):
    xf = x.astype(jnp.float32)
    y = xf * lax.rsqrt(jnp.mean(xf * xf, axis=-1, keepdims=True) + EPS)
    return (y * w.astype(jnp.float32)).astype(x.dtype)


def rms_norm_plain(x):
    xf = x.astype(jnp.float32)
    return xf * lax.rsqrt(jnp.mean(xf * xf, axis=-1, keepdims=True) + EPS)


def split_heads(x, heads):
    return x.reshape(x.shape[:-1] + (heads, x.shape[-1] // heads))


def rope(x, pos):
    d = x.shape[-1]
    inv = ROPE_THETA ** (-jnp.arange(0, d, 2, dtype=jnp.float32) / d)
    ang = pos.astype(jnp.float32)[:, None] * inv[None, :]
    cos = jnp.cos(ang)[None, :, None, :]
    sin = jnp.sin(ang)[None, :, None, :]
    xf = x.astype(jnp.float32)
    x1, x2 = xf[..., : d // 2], xf[..., d // 2:]
    return jnp.concatenate([x1 * cos - x2 * sin, x2 * cos + x1 * sin], axis=-1).astype(x.dtype)


def diff_lambda(lq1, lk1, lq2, lk2, lam_init):
    f = lambda a: a.astype(jnp.float32)
    return jnp.exp(jnp.sum(f(lq1) * f(lk1))) - jnp.exp(jnp.sum(f(lq2) * f(lk2))) + lam_init


def diff_project(h, pos, w_in, q_gain, k_gain):
    z = h @ w_in
    q, k, v, g = jnp.split(z, 4, axis=-1)
    q = rope(rms_norm(split_heads(q, 2 * A_HEADS), q_gain), pos)
    k = rope(rms_norm(split_heads(k, 2 * A_HEADS), k_gain), pos)
    v = split_heads(v, A_HEADS)
    return q, k, v, g


def diff_attend(q, k, v, q_pos, k_pos, lam):
    b, t = q.shape[:2]
    s_len = k.shape[1]
    sc = jnp.einsum('bthd,bshd->bhts', q.astype(jnp.float32), k.astype(jnp.float32)) * (A_HD ** -0.5)
    visible = k_pos[None, :] < ((q_pos // CHUNK + 1) * CHUNK)[:, None]
    sc = jnp.where(visible[None, None], sc, -jnp.inf)
    p = jax.nn.softmax(sc, axis=-1).reshape(b, A_HEADS, 2, t, s_len)
    a = p[:, :, 0] - lam * p[:, :, 1]
    return jnp.einsum('bhts,bshe->bthe', a, v.astype(jnp.float32))


def diff_output(o, g, sub_gain, lam_init, w_out):
    o = rms_norm(o, sub_gain) * (1.0 - lam_init)
    o = o.reshape(o.shape[:2] + (A_W,)).astype(g.dtype) * jax.nn.silu(g)
    return o @ w_out


def diff_attn_prompt(h, pos, w_in, w_out, q_gain, k_gain, lam, lam_init, sub_gain):
    q, k, v, g = diff_project(h, pos, w_in, q_gain, k_gain)
    b, s = h.shape[:2]
    nb = s // Q_BLOCK
    qb = jnp.moveaxis(q.reshape(b, nb, Q_BLOCK, 2 * A_HEADS, A_HD), 1, 0)
    pb = pos.reshape(nb, Q_BLOCK)
    o = lax.map(lambda blk: diff_attend(blk[0], k, v, blk[1], pos, lam), (qb, pb))
    o = jnp.moveaxis(o, 0, 1).reshape(b, s, A_HEADS, A_VD)
    return diff_output(o, g, sub_gain, lam_init, w_out), k, v


def diff_attn_sample(h, pos, k_past, v_past, w_in, w_out, q_gain, k_gain, lam, lam_init, sub_gain):
    q, k, v, g = diff_project(h, pos, w_in, q_gain, k_gain)
    k_all = jnp.concatenate([k_past.astype(k.dtype), k], axis=1)
    v_all = jnp.concatenate([v_past.astype(v.dtype), v], axis=1)
    k_pos = jnp.arange(k_all.shape[1], dtype=jnp.int32)
    o = diff_attend(q, k_all, v_all, pos, k_pos, lam)
    return diff_output(o, g, sub_gain, lam_init, w_out), k, v


def ret_log_gamma():
    return jnp.log1p(-(2.0 ** (-5.0 - jnp.arange(R_HEADS, dtype=jnp.float32))))


def ret_project(h, pos, w_in):
    z = h @ w_in
    q, k, v, g = jnp.split(z, [R_QK, 2 * R_QK, 2 * R_QK + R_V], axis=-1)
    q = rope(split_heads(q, R_HEADS), pos)
    k = rope(split_heads(k, R_HEADS), pos) * (R_DK ** -0.5)
    v = split_heads(v, R_HEADS)
    return q, k, v, g


def retention_chunk(state, q, k, v):
    L = q.shape[1]
    lg = ret_log_gamma()
    idx = jnp.arange(L, dtype=jnp.float32)
    diff = idx[:, None] - idx[None, :]
    decay = jnp.where(diff >= 0, jnp.exp(lg[:, None, None] * jnp.maximum(diff, 0.0)), 0.0)
    qf, kf, vf = q.astype(jnp.float32), k.astype(jnp.float32), v.astype(jnp.float32)
    inner = jnp.einsum('bihd,bjhd->bhij', qf, kf) * decay[None]
    o = jnp.einsum('bhij,bjhe->bihe', inner, vf)
    cross_decay = jnp.exp(lg[None, :] * (idx[:, None] + 1.0))
    o = o + jnp.einsum('bihd,bhde->bihe', qf, state) * cross_decay[None, :, :, None]
    k_decay = jnp.exp(lg[:, None] * (L - 1.0 - idx[None, :]))
    new_state = state * jnp.exp(lg * L)[None, :, None, None] + jnp.einsum('bjhd,bjhe,hj->bhde', kf, vf, k_decay)
    return new_state, o


def ret_output(o, g, w_out):
    o = rms_norm_plain(o)
    o = o.reshape(o.shape[:2] + (R_V,)).astype(g.dtype) * jax.nn.silu(g)
    return o @ w_out


def retention_prompt(h, pos, w_in, w_out):
    q, k, v, g = ret_project(h, pos, w_in)
    b, s = h.shape[:2]
    nc = s // CHUNK
    to_chunks = lambda a: jnp.moveaxis(a.reshape((b, nc, CHUNK) + a.shape[2:]), 1, 0)
    s0 = jnp.zeros((b, R_HEADS, R_DK, R_DV), jnp.float32)
    s_fin, o = lax.scan(lambda st, xs: retention_chunk(st, xs[0], xs[1], xs[2]), s0,
                        (to_chunks(q), to_chunks(k), to_chunks(v)))
    o = jnp.moveaxis(o, 0, 1).reshape(b, s, R_HEADS, R_DV)
    return ret_output(o, g, w_out), s_fin


def retention_sample(h, pos, state, w_in, w_out):
    q, k, v, g = ret_project(h, pos, w_in)
    s_new, o = retention_chunk(state.astype(jnp.float32), q, k, v)
    return ret_output(o, g, w_out), s_new


def cmlp_branch(h, w_in, w_out, v_gain, w_s, b_s, chunk_len):
    z = h @ w_in
    u, v, g = jnp.split(z, 3, axis=-1)
    v = rms_norm(jax.nn.gelu(v), v_gain)
    b, s = h.shape[:2]
    nc = s // chunk_len
    vc = v.reshape(b, nc, chunk_len, M_GROUPS, M_GD)
    w = jnp.tril(w_s[:, :chunk_len, :chunk_len])
    mix = jnp.einsum('gij,bcjgd->bcigd', w, vc) + b_s[:, :chunk_len].T[None, None, :, :, None]
    y = jax.nn.gelu(u) * mix.reshape(b, s, M_W).astype(u.dtype) * jax.nn.silu(g)
    return y @ w_out, v


def setup_inputs(seed: int = 0) -> dict:
    key = jax.random.key(seed)
    ks = jax.random.split(key, 24)
    nrm = lambda k, shape, scale: jax.random.normal(k, shape, jnp.float32) * scale
    gain = lambda k, shape: 1.0 + 0.01 * jax.random.normal(k, shape, jnp.float32)
    return {
        "x_prompt": nrm(ks[0], (BATCH, SEQ, D_MODEL), 1.0),
        "x_sample": nrm(ks[1], (DEC_BATCH, DEC_SEQ, D_MODEL), 1.0),
        "cache_k_attn": nrm(ks[2], (N_A, DEC_BATCH, PAST_LEN, 2 * A_HEADS, A_HD), 1.0),
        "cache_v_attn": nrm(ks[3], (N_A, DEC_BATCH, PAST_LEN, A_HEADS, A_VD), 1.0),
        "state_ret": nrm(ks[4], (N_B, DEC_BATCH, R_HEADS, R_DK, R_DV), 0.1),
        "norm_w": gain(ks[5], (DEPTH, D_MODEL)),
        "a_w_in": nrm(ks[6], (N_A, D_MODEL, 4 * A_W), D_MODEL ** -0.5),
        "a_w_out": nrm(ks[7], (N_A, A_W, D_MODEL), A_W ** -0.5),
        "a_q_gain": gain(ks[8], (N_A, A_HD)),
        "a_k_gain": gain(ks[9], (N_A, A_HD)),
        "a_lam_q1": nrm(ks[10], (N_A, A_HD), 0.1),
        "a_lam_k1": nrm(ks[11], (N_A, A_HD), 0.1),
        "a_lam_q2": nrm(ks[12], (N_A, A_HD), 0.1),
        "a_lam_k2": nrm(ks[13], (N_A, A_HD), 0.1),
        "a_sub_gain": gain(ks[14], (N_A, A_VD)),
        "r_w_in": nrm(ks[15], (N_B, D_MODEL, 2 * R_QK + 2 * R_V), D_MODEL ** -0.5),
        "r_w_out": nrm(ks[16], (N_B, R_V, D_MODEL), R_V ** -0.5),
        "c_w_in": nrm(ks[17], (N_C, D_MODEL, 3 * M_W), D_MODEL ** -0.5),
        "c_w_out": nrm(ks[18], (N_C, M_W, D_MODEL), M_W ** -0.5),
        "c_v_gain": gain(ks[19], (N_C, M_W)),
        "c_w_s": nrm(ks[20], (N_C, M_GROUPS, M_CHUNK, M_CHUNK), M_CHUNK ** -0.5),
        "c_b_s": 1.0 + nrm(ks[21], (N_C, M_GROUPS, M_CHUNK), 0.1),
    }


def reference(x_prompt, x_sample, cache_k_attn, cache_v_attn, state_ret, norm_w,
              a_w_in, a_w_out, a_q_gain, a_k_gain, a_lam_q1, a_lam_k1, a_lam_q2, a_lam_k2, a_sub_gain,
              r_w_in, r_w_out, c_w_in, c_w_out, c_v_gain, c_w_s, c_b_s):
    hp, hs = x_prompt, x_sample
    s_len = x_prompt.shape[1]
    dec_len = x_sample.shape[1]
    past = cache_k_attn.shape[2]
    pos_p = jnp.arange(s_len, dtype=jnp.int32)
    pos_s = past + jnp.arange(dec_len, dtype=jnp.int32)

    kp_l, vp_l, ks_l, vs_l = [], [], [], []
    sp_l, ss_l, vm_l = [], [], []
    for i in range(DEPTH):
        kind = i % N_MIXERS
        j = i // N_MIXERS
        np_in = rms_norm(hp, norm_w[i])
        ns_in = rms_norm(hs, norm_w[i])
        if kind == 0:
            lam_init = 0.8 - 0.6 * math.exp(-0.3 * i)
            lam = diff_lambda(a_lam_q1[j], a_lam_k1[j], a_lam_q2[j], a_lam_k2[j], lam_init)
            yp, kp, vp = diff_attn_prompt(np_in, pos_p, a_w_in[j], a_w_out[j], a_q_gain[j], a_k_gain[j],
                                          lam, lam_init, a_sub_gain[j])
            ys, kn, vn = diff_attn_sample(ns_in, pos_s, cache_k_attn[j], cache_v_attn[j], a_w_in[j], a_w_out[j],
                                          a_q_gain[j], a_k_gain[j], lam, lam_init, a_sub_gain[j])
            kp_l.append(kp); vp_l.append(vp); ks_l.append(kn); vs_l.append(vn)
        elif kind == 1:
            yp, st_p = retention_prompt(np_in, pos_p, r_w_in[j], r_w_out[j])
            ys, st_s = retention_sample(ns_in, pos_s, state_ret[j], r_w_in[j], r_w_out[j])
            sp_l.append(st_p); ss_l.append(st_s)
        else:
            yp, _ = cmlp_branch(np_in, c_w_in[j], c_w_out[j], c_v_gain[j], c_w_s[j], c_b_s[j], M_CHUNK)
            ys, v_s = cmlp_branch(ns_in, c_w_in[j], c_w_out[j], c_v_gain[j], c_w_s[j], c_b_s[j], dec_len)
            vm_l.append(v_s)
        hp = hp + yp.astype(hp.dtype)
        hs = hs + ys.astype(hs.dtype)

    return (hp, hs, jnp.stack(kp_l), jnp.stack(vp_l), jnp.stack(ks_l), jnp.stack(vs_l),
            jnp.stack(sp_l), jnp.stack(ss_l), jnp.stack(vm_l))
```

```python
import functools
import math

import jax
import jax.numpy as jnp
from jax import lax
from jax.experimental import pallas as pl
from jax.experimental.pallas import tpu as pltpu

F32 = jnp.float32
BF16 = jnp.bfloat16

EPS = 1e-6
CHUNK = 64
ROPE_THETA = 10000.0
N_MIXERS = 3
A_HD = 64
A_VD = 2 * A_HD
R_DK = 256
R_DV = 2 * R_DK
M_GROUPS = 8
M_CHUNK = 128
LOG2E = 1.4426950408889634

LANES = 128
MXU_COLS = 256
VMEM_LIMIT = 56 * 1024 * 1024
NEG_BIG = -1e30


def _tile(n, pref):
    if n <= pref:
        return n
    t = pref
    while t >= 8:
        if n % t == 0:
            return t
        t -= 8
    return n


def _params(sem):
    return pltpu.CompilerParams(dimension_semantics=sem, vmem_limit_bytes=VMEM_LIMIT)


def _gelu(x):
    return 0.5 * x * (1.0 + jnp.tanh(0.7978845608028654 * (x + 0.044715 * (x * x * x))))


def _sigmoid(x):
    return 1.0 / (1.0 + jnp.exp(-x))


def _proj_kernel(*refs, epi, out_scale):
    if epi == "qk":
        x_ref, nw_ref, w_ref, cos_ref, sin_ref, gain_ref, gsum_ref, o_ref, xn_ref = refs
    elif epi == "rope256":
        x_ref, nw_ref, w_ref, cos_ref, sin_ref, o_ref, xn_ref = refs
    else:
        x_ref, nw_ref, w_ref, o_ref, xn_ref = refs

    @pl.when(pl.program_id(1) == 0)
    def _():
        x = x_ref[...]
        ms = jnp.mean(x * x, axis=-1, keepdims=True)
        xn_ref[...] = (x * lax.rsqrt(ms + EPS) * nw_ref[...]).astype(BF16)

    z = jnp.dot(xn_ref[...], w_ref[...], preferred_element_type=F32)
    tn = z.shape[1]
    if epi == "qk":
        cos = jnp.concatenate([cos_ref[...]] * 2, axis=1)
        sin = jnp.concatenate([sin_ref[...]] * 2, axis=1)
        gain = jnp.concatenate([gain_ref[...]] * 2, axis=1)
        lane = lax.broadcasted_iota(jnp.int32, (z.shape[0], MXU_COLS), 1)
        first_half = (lane % A_HD) < (A_HD // 2)
        for c in range(tn // MXU_COLS):
            zc = z[:, c * MXU_COLS:(c + 1) * MXU_COLS]
            ssq = jnp.dot((zc * zc).astype(BF16), gsum_ref[...], preferred_element_type=F32)
            zn = zc * lax.rsqrt(ssq * (1.0 / A_HD) + EPS) * gain
            partner = jnp.where(first_half,
                                pltpu.roll(zn, MXU_COLS - A_HD // 2, 1),
                                pltpu.roll(zn, A_HD // 2, 1))
            out = zn * cos + partner * sin
            if out_scale != 1.0:
                out = out * out_scale
            o_ref[:, c * MXU_COLS:(c + 1) * MXU_COLS] = out.astype(o_ref.dtype)
    elif epi == "rope256":
        cos = cos_ref[...]
        sin = sin_ref[...]
        for c in range(tn // R_DK):
            x1 = z[:, c * R_DK:c * R_DK + LANES]
            x2 = z[:, c * R_DK + LANES:(c + 1) * R_DK]
            o1 = x1 * cos - x2 * sin
            o2 = x2 * cos + x1 * sin
            if out_scale != 1.0:
                o1 = o1 * out_scale
                o2 = o2 * out_scale
            o_ref[:, c * R_DK:c * R_DK + LANES] = o1.astype(o_ref.dtype)
            o_ref[:, c * R_DK + LANES:(c + 1) * R_DK] = o2.astype(o_ref.dtype)
    else:
        o_ref[...] = z.astype(o_ref.dtype)


def _proj(x, nw, w, out_dtype, *, epi="none", cos=None, sin=None, gain=None,
          out_scale=1.0, name):
    m, d = x.shape
    n = w.shape[1]
    tm = _tile(m, 512)
    tn = _tile(n, 1024)
    in_specs = [pl.BlockSpec((tm, d), lambda i, j: (i, 0)),
                pl.BlockSpec((1, d), lambda i, j: (0, 0)),
                pl.BlockSpec((d, tn), lambda i, j: (0, j))]
    args = [x, nw.reshape(1, d), w]
    if epi in ("qk", "rope256"):
        in_specs += [pl.BlockSpec((tm, LANES), lambda i, j: (i, 0)),
                     pl.BlockSpec((tm, LANES), lambda i, j: (i, 0))]
        args += [cos, sin]
    if epi == "qk":
        gidx = jnp.arange(MXU_COLS) // A_HD
        gsum = (gidx[:, None] == gidx[None, :]).astype(BF16)
        in_specs += [pl.BlockSpec((1, LANES), lambda i, j: (0, 0)),
                     pl.BlockSpec((MXU_COLS, MXU_COLS), lambda i, j: (0, 0))]
        args += [jnp.tile(gain.reshape(1, A_HD), (1, LANES // A_HD)), gsum]
    return pl.pallas_call(
        functools.partial(_proj_kernel, epi=epi, out_scale=out_scale),
        out_shape=jax.ShapeDtypeStruct((m, n), out_dtype),
        grid=(m // tm, n // tn),
        in_specs=in_specs,
        out_specs=pl.BlockSpec((tm, tn), lambda i, j: (i, j)),
        scratch_shapes=[pltpu.VMEM((tm, d), BF16)],
        compiler_params=_params(("parallel", "arbitrary")),
        name=name,
    )(*args)


def _outproj_kernel(a_ref, g_ref, w_ref, h_ref, o_ref, ag_ref):
    @pl.when(pl.program_id(1) == 0)
    def _():
        g = g_ref[...].astype(F32)
        ag_ref[...] = (a_ref[...].astype(F32) * (g * _sigmoid(g))).astype(BF16)

    o_ref[...] = h_ref[...] + jnp.dot(ag_ref[...], w_ref[...], preferred_element_type=F32)


def _outproj(a, g, w, h, *, name):
    m, k = a.shape
    n = w.shape[1]
    tm = _tile(m, 512)
    tn = _tile(n, 512)
    return pl.pallas_call(
        _outproj_kernel,
        out_shape=jax.ShapeDtypeStruct((m, n), F32),
        grid=(m // tm, n // tn),
        in_specs=[pl.BlockSpec((tm, k), lambda i, j: (i, 0)),
                  pl.BlockSpec((tm, k), lambda i, j: (i, 0)),
                  pl.BlockSpec((k, tn), lambda i, j: (0, j)),
                  pl.BlockSpec((tm, tn), lambda i, j: (i, j))],
        out_specs=pl.BlockSpec((tm, tn), lambda i, j: (i, j)),
        scratch_shapes=[pltpu.VMEM((tm, k), BF16)],
        compiler_params=_params(("parallel", "arbitrary")),
        name=name,
    )(a, g, w, h)


def _lambda(lamp_ref, lam_init):
    lp = lamp_ref[...]
    s1 = jnp.sum(lp[0:1, :] * lp[1:2, :], axis=-1, keepdims=True)
    s2 = jnp.sum(lp[2:3, :] * lp[3:4, :], axis=-1, keepdims=True)
    return jnp.exp(s1) - jnp.exp(s2) + lam_init


def _attn_prompt_kernel(qt_ref, k_ref, vt_ref, lamp_ref, sg_ref, o_ref,
                        qpad_ref, m_ref, l_ref, acc_ref, *, tq, tk, lam_init):
    i = pl.program_id(1)
    qt = qt_ref[...]
    row = lax.broadcasted_iota(jnp.int32, qt.shape, 0)
    zero = jnp.zeros_like(qt)
    qpad_ref[:, :tq] = jnp.where(row < A_HD, qt, zero)
    qpad_ref[:, tq:] = jnp.where(row >= A_HD, qt, zero)
    m_ref[...] = jnp.full(m_ref.shape, NEG_BIG, F32)
    l_ref[...] = jnp.zeros(l_ref.shape, F32)
    acc_ref[...] = jnp.zeros(acc_ref.shape, F32)

    def step(j, mask):
        kj = k_ref[pl.ds(pl.multiple_of(j * tk, tk), tk), :]
        s = jnp.dot(kj, qpad_ref[...], preferred_element_type=F32)
        if mask is not None:
            s = jnp.where(mask, s, NEG_BIG)
        m_old = m_ref[...]
        m_new = jnp.maximum(m_old, jnp.max(s, axis=0, keepdims=True))
        alpha = jnp.exp2(m_old - m_new)
        p = jnp.exp2(s - m_new)
        l_ref[...] = alpha * l_ref[...] + jnp.sum(p, axis=0, keepdims=True)
        pv = jnp.dot(vt_ref[0, j], p.astype(BF16), preferred_element_type=F32)
        acc_ref[...] = alpha * acc_ref[...] + pv
        m_ref[...] = m_new

    n_sub = tq // tk
    n_full = i * n_sub

    def body(j, carry):
        step(j, None)
        return carry

    lax.fori_loop(0, n_full, body, 0)
    r = lax.broadcasted_iota(jnp.int32, (tk, 2 * tq), 0)
    c = lax.broadcasted_iota(jnp.int32, (tk, 2 * tq), 1)
    qchunk = (c % tq) // CHUNK
    for d in range(n_sub):
        step(n_full + d, (d * tk + r) // CHUNK <= qchunk)

    lam = _lambda(lamp_ref, lam_init)
    inv_l = 1.0 / l_ref[...]
    acc = acc_ref[...]
    ot = acc[:, :tq] * inv_l[:, :tq] - lam * (acc[:, tq:] * inv_l[:, tq:])
    ms = jnp.mean(ot * ot, axis=0, keepdims=True)
    ot = ot * lax.rsqrt(ms + EPS) * (sg_ref[...] * (1.0 - lam_init))
    o_ref[...] = ot.T.astype(o_ref.dtype)


def _attn_prompt(q, k, v, lamp, sub_gain, lam_init, *, name):
    s, w = q.shape
    heads = w // A_VD
    tq = _tile(s, 512)
    tk = tq
    qt = q.T
    kb = k.astype(BF16)
    vt = v.astype(BF16).reshape(s // tk, tk, heads, A_VD).transpose(2, 0, 3, 1)
    return pl.pallas_call(
        functools.partial(_attn_prompt_kernel, tq=tq, tk=tk, lam_init=lam_init),
        out_shape=jax.ShapeDtypeStruct((s, w), BF16),
        grid=(heads, s // tq),
        in_specs=[pl.BlockSpec((A_VD, tq), lambda h, i: (h, i)),
                  pl.BlockSpec((s, A_VD), lambda h, i: (0, h)),
                  pl.BlockSpec((1, s // tk, A_VD, tk), lambda h, i: (h, 0, 0, 0)),
                  pl.BlockSpec((4, A_HD), lambda h, i: (0, 0)),
                  pl.BlockSpec((A_VD, 1), lambda h, i: (0, 0))],
        out_specs=pl.BlockSpec((tq, A_VD), lambda h, i: (i, h)),
        scratch_shapes=[pltpu.VMEM((A_VD, 2 * tq), BF16),
                        pltpu.VMEM((1, 2 * tq), F32),
                        pltpu.VMEM((1, 2 * tq), F32),
                        pltpu.VMEM((A_VD, 2 * tq), F32)],
        compiler_params=_params(("parallel", "arbitrary")),
        name=name,
    )(qt, kb, vt, lamp, sub_gain.reshape(A_VD, 1))


def _attn_sample_kernel(q_ref, kn_ref, vn_ref, kc_ref, vc_ref, lamp_ref, sg_ref, o_ref, *, lam_init):
    q = q_ref[...]
    lane = lax.broadcasted_iota(jnp.int32, q.shape, 1)
    zero = jnp.zeros_like(q)
    kc = kc_ref[0].astype(BF16)
    kn = kn_ref[...].astype(BF16)
    nt = (((1,), (1,)), ((), ()))
    probs = []
    for half in range(2):
        qh = jnp.where((lane >= A_HD) == bool(half), q, zero)
        sc = lax.dot_general(qh, kc, nt, preferred_element_type=F32)
        sn = lax.dot_general(qh, kn, nt, preferred_element_type=F32)
        m = jnp.maximum(jnp.max(sc, axis=-1, keepdims=True), jnp.max(sn, axis=-1, keepdims=True))
        pc = jnp.exp2(sc - m)
        pn = jnp.exp2(sn - m)
        inv = 1.0 / (jnp.sum(pc, axis=-1, keepdims=True) + jnp.sum(pn, axis=-1, keepdims=True))
        probs.append((pc * inv, pn * inv))
    lam = _lambda(lamp_ref, lam_init)
    ac = (probs[0][0] - lam * probs[1][0]).astype(BF16)
    an = (probs[0][1] - lam * probs[1][1]).astype(BF16)
    o = (jnp.dot(ac, vc_ref[0].astype(BF16), preferred_element_type=F32)
         + jnp.dot(an, vn_ref[...].astype(BF16), preferred_element_type=F32))
    ms = jnp.mean(o * o, axis=-1, keepdims=True)
    o_ref[...] = (o * lax.rsqrt(ms + EPS) * (sg_ref[...] * (1.0 - lam_init))).astype(o_ref.dtype)


def _attn_sample(q, k, v, kc, vc, lamp, sub_gain, lam_init, *, name):
    bsz, past, w = kc.shape
    heads = w // A_VD
    ln = q.shape[0] // bsz
    assert past % CHUNK == 0 and ln <= CHUNK
    row = lambda b, h: (b, h)
    cache = lambda b, h: (b, 0, h)
    return pl.pallas_call(
        functools.partial(_attn_sample_kernel, lam_init=lam_init),
        out_shape=jax.ShapeDtypeStruct(q.shape, BF16),
        grid=(bsz, heads),
        in_specs=[pl.BlockSpec((ln, A_VD), row),
                  pl.BlockSpec((ln, A_VD), row),
                  pl.BlockSpec((ln, A_VD), row),
                  pl.BlockSpec((1, past, A_VD), cache),
                  pl.BlockSpec((1, past, A_VD), cache),
                  pl.BlockSpec((4, A_HD), lambda b, h: (0, 0)),
                  pl.BlockSpec((1, A_VD), lambda b, h: (0, 0))],
        out_specs=pl.BlockSpec((ln, A_VD), row),
        compiler_params=_params(("parallel", "parallel")),
        name=name,
    )(q, k, v, kc, vc, lamp, sub_gain.reshape(1, A_VD))


def _retention_kernel(q_ref, k_ref, v_ref, s0_ref, dec_ref, cd_ref, kd_ref, gl_ref,
                      o_ref, sout_ref, st_ref, *, lc):
    c = pl.program_id(2)

    @pl.when(c == 0)
    def _():
        st_ref[...] = s0_ref[0, 0]

    q = q_ref[...]
    k = k_ref[...]
    v = v_ref[...]
    st = st_ref[...]
    inner = lax.dot_general(q, k, (((1,), (1,)), ((), ())), preferred_element_type=F32) * dec_ref[0]
    cross = jnp.dot(q, st.astype(BF16), preferred_element_type=F32)
    cd = jnp.concatenate([cd_ref[0]] * (R_DV // LANES), axis=1)
    o = jnp.dot(inner.astype(BF16), v, preferred_element_type=F32) + cross * cd
    ks = k.astype(F32) * jnp.concatenate([kd_ref[0]] * (R_DK // LANES), axis=1)
    vv = v
    if lc < LANES:
        ks = jnp.concatenate([ks, jnp.zeros((LANES - lc, R_DK), F32)], axis=0)
        vv = jnp.concatenate([v, jnp.zeros((LANES - lc, R_DV), BF16)], axis=0)
    gl = jnp.concatenate([gl_ref[0]] * (R_DV // LANES), axis=1)
    st_new = st * gl + jnp.dot(ks.T.astype(BF16), vv, preferred_element_type=F32)
    st_ref[...] = st_new
    ms = jnp.mean(o * o, axis=-1, keepdims=True)
    o_ref[...] = (o * lax.rsqrt(ms + EPS)).astype(o_ref.dtype)

    @pl.when(c == pl.num_programs(2) - 1)
    def _():
        sout_ref[0, 0] = st_new


def _retention(q, k, v, s0, lc, *, name):
    bsz, heads = s0.shape[:2]
    t = q.shape[0] // bsz
    nc = t // lc
    lg = jnp.log1p(-(2.0 ** (-5.0 - jnp.arange(heads, dtype=F32))))
    idx = jnp.arange(lc, dtype=F32)
    diff = idx[:, None] - idx[None, :]
    dec = jnp.where(diff >= 0, jnp.exp(lg[:, None, None] * jnp.maximum(diff, 0.0)), 0.0)
    cd = jnp.broadcast_to(jnp.exp(lg[:, None] * (idx[None, :] + 1.0))[:, :, None], (heads, lc, LANES))
    kd = jnp.broadcast_to(jnp.exp(lg[:, None] * (lc - 1.0 - idx[None, :]))[:, :, None], (heads, lc, LANES))
    gl = jnp.broadcast_to(jnp.exp(lg * lc)[:, None, None], (heads, 1, LANES))
    rows = lambda b, h, c: (b * nc + c, h)
    tab = lambda b, h, c: (h, 0, 0)
    state = lambda b, h, c: (b, h, 0, 0)
    return pl.pallas_call(
        functools.partial(_retention_kernel, lc=lc),
        out_shape=(jax.ShapeDtypeStruct(v.shape, BF16),
                   jax.ShapeDtypeStruct(s0.shape, F32)),
        grid=(bsz, heads, nc),
        in_specs=[pl.BlockSpec((lc, R_DK), rows),
                  pl.BlockSpec((lc, R_DK), rows),
                  pl.BlockSpec((lc, R_DV), rows),
                  pl.BlockSpec((1, 1, R_DK, R_DV), state),
                  pl.BlockSpec((1, lc, lc), tab),
                  pl.BlockSpec((1, lc, LANES), tab),
                  pl.BlockSpec((1, lc, LANES), tab),
                  pl.BlockSpec((1, 1, LANES), tab)],
        out_specs=(pl.BlockSpec((lc, R_DV), rows),
                   pl.BlockSpec((1, 1, R_DK, R_DV), state)),
        scratch_shapes=[pltpu.VMEM((R_DK, R_DV), F32)],
        compiler_params=_params(("parallel", "parallel", "arbitrary")),
        name=name,
    )(q, k, v, s0, dec, cd, kd, gl)


def _cmlp_kernel(u_ref, v_ref, vg_ref, w_ref, b_ref, *out_refs, emit_v):
    if emit_v:
        a_ref, vn_ref = out_refs
    else:
        (a_ref,) = out_refs
    va = _gelu(v_ref[...])
    ms = jnp.mean(va * va, axis=-1, keepdims=True)
    vn = va * lax.rsqrt(ms + EPS) * vg_ref[...]
    if emit_v:
        vn_ref[...] = vn
    vb = vn.astype(BF16)
    gd = vn.shape[1] // M_GROUPS
    for g in range(M_GROUPS):
        sl = slice(g * gd, (g + 1) * gd)
        bias = jnp.concatenate([b_ref[g]] * (gd // LANES), axis=1)
        mix = jnp.dot(w_ref[g], vb[:, sl], preferred_element_type=F32) + bias
        a_ref[:, sl] = (_gelu(u_ref[:, sl].astype(F32)) * mix).astype(a_ref.dtype)


def _cmlp(u, v, v_gain, wmix, bmix, *, emit_v, name):
    m, w = v.shape
    t = wmix.shape[1]
    out_shape = [jax.ShapeDtypeStruct((m, w), BF16)]
    out_specs = [pl.BlockSpec((t, w), lambda i: (i, 0))]
    if emit_v:
        out_shape.append(jax.ShapeDtypeStruct((m, w), F32))
        out_specs.append(pl.BlockSpec((t, w), lambda i: (i, 0)))
    res = pl.pallas_call(
        functools.partial(_cmlp_kernel, emit_v=emit_v),
        out_shape=tuple(out_shape),
        grid=(m // t,),
        in_specs=[pl.BlockSpec((t, w), lambda i: (i, 0)),
                  pl.BlockSpec((t, w), lambda i: (i, 0)),
                  pl.BlockSpec((1, w), lambda i: (0, 0)),
                  pl.BlockSpec((M_GROUPS, t, t), lambda i: (0, 0, 0)),
                  pl.BlockSpec((M_GROUPS, t, LANES), lambda i: (0, 0, 0))],
        out_specs=tuple(out_specs),
        compiler_params=_params(("parallel",)),
        name=name,
    )(u, v, v_gain.reshape(1, w), wmix, bmix)
    return res if emit_v else res[0]


def _rope_tables(pos, d, signed):
    inv = ROPE_THETA ** (-jnp.arange(0, d, 2, dtype=F32) / d)
    ang = pos.astype(F32)[:, None] * inv[None, :]
    cos, sin = jnp.cos(ang), jnp.sin(ang)
    reps = LANES // (d // 2)
    if signed:
        return (jnp.tile(cos, (1, reps)),
                jnp.tile(jnp.concatenate([-sin, sin], axis=1), (1, reps // 2)))
    return jnp.tile(cos, (1, reps)), jnp.tile(sin, (1, reps))


def _diff_attn_layer(h, pos, nw, w_in, w_out, q_gain, k_gain, lamp, sub_gain, lam_init, cache, tag):
    w = w_out.shape[0]
    cos, sin = _rope_tables(pos, A_HD, signed=True)
    wq, wk, wv, wg = (w_in[:, i * w:(i + 1) * w].astype(BF16) for i in range(4))
    q = _proj(h, nw, wq, BF16, epi="qk", cos=cos, sin=sin, gain=q_gain,
              out_scale=(A_HD ** -0.5) * LOG2E, name=f"{tag}_proj_q")
    k = _proj(h, nw, wk, F32, epi="qk", cos=cos, sin=sin, gain=k_gain, name=f"{tag}_proj_k")
    v = _proj(h, nw, wv, F32, name=f"{tag}_proj_v")
    g = _proj(h, nw, wg, BF16, name=f"{tag}_proj_g")
    if cache is None:
        o = _attn_prompt(q, k, v, lamp, sub_gain, lam_init, name=f"{tag}_attn")
    else:
        kc, vc = cache
        o = _attn_sample(q, k, v, kc, vc, lamp, sub_gain, lam_init, name=f"{tag}_attn")
    return _outproj(o, g, w_out.astype(BF16), h, name=f"{tag}_out"), k, v


def _retention_layer(h, pos, nw, w_in, w_out, s0, lc, tag):
    heads = s0.shape[1]
    qk_w, v_w = heads * R_DK, heads * R_DV
    cos, sin = _rope_tables(pos, R_DK, signed=False)
    wq = w_in[:, :qk_w].astype(BF16)
    wk = w_in[:, qk_w:2 * qk_w].astype(BF16)
    wv = w_in[:, 2 * qk_w:2 * qk_w + v_w].astype(BF16)
    wg = w_in[:, 2 * qk_w + v_w:].astype(BF16)
    q = _proj(h, nw, wq, BF16, epi="rope256", cos=cos, sin=sin, name=f"{tag}_proj_q")
    k = _proj(h, nw, wk, BF16, epi="rope256", cos=cos, sin=sin, out_scale=R_DK ** -0.5,
              name=f"{tag}_proj_k")
    v = _proj(h, nw, wv, BF16, name=f"{tag}_proj_v")
    g = _proj(h, nw, wg, BF16, name=f"{tag}_proj_g")
    o, s_new = _retention(q, k, v, s0, lc, name=f"{tag}_ret")
    return _outproj(o, g, w_out.astype(BF16), h, name=f"{tag}_out"), s_new


def _cmlp_layer(h, nw, w_in, w_out, v_gain, wmix, bmix, emit_v, tag):
    w = w_out.shape[0]
    wu, wv, wg = (w_in[:, i * w:(i + 1) * w].astype(BF16) for i in range(3))
    u = _proj(h, nw, wu, BF16, name=f"{tag}_proj_u")
    v = _proj(h, nw, wv, F32, name=f"{tag}_proj_v")
    g = _proj(h, nw, wg, BF16, name=f"{tag}_proj_g")
    res = _cmlp(u, v, v_gain, wmix, bmix, emit_v=emit_v, name=f"{tag}_mix")
    a, vn = res if emit_v else (res, None)
    return _outproj(a, g, w_out.astype(BF16), h, name=f"{tag}_out"), vn


def _mix_tables(w_s, b_s, chunk_len):
    groups = w_s.shape[0]
    wl = jnp.tril(w_s[:, :chunk_len, :chunk_len])
    reps = M_CHUNK // chunk_len
    eye = jnp.eye(reps, dtype=F32)
    wt = jnp.einsum("ab,gij->gaibj", eye, wl).reshape(groups, M_CHUNK, M_CHUNK)
    bt = jnp.tile(b_s[:, :chunk_len], (1, reps))
    return wt.astype(BF16), jnp.broadcast_to(bt[:, :, None], (groups, M_CHUNK, LANES))


def kernel(x_prompt, x_sample, cache_k_attn, cache_v_attn, state_ret, norm_w, a_w_in, a_w_out, a_q_gain, a_k_gain, a_lam_q1, a_lam_k1, a_lam_q2, a_lam_k2, a_sub_gain, r_w_in, r_w_out, c_w_in, c_w_out, c_v_gain, c_w_s, c_b_s):
    batch, s_len, d = x_prompt.shape
    dec_b, dec_len, _ = x_sample.shape
    past = cache_k_attn.shape[2]
    depth = norm_w.shape[0]
    assert batch == 1 and M_CHUNK % dec_len == 0 and s_len % M_CHUNK == 0

    hp = x_prompt.reshape(s_len, d)
    hs = x_sample.reshape(dec_b * dec_len, d)
    pos_p = jnp.arange(s_len, dtype=jnp.int32)
    pos_s = jnp.tile(past + jnp.arange(dec_len, dtype=jnp.int32), dec_b)

    kp_l, vp_l, ks_l, vs_l, sp_l, ss_l, vm_l = [], [], [], [], [], [], []
    for i in range(depth):
        kind, j = i % N_MIXERS, i // N_MIXERS
        if kind == 0:
            lam_init = 0.8 - 0.6 * math.exp(-0.3 * i)
            lamp = jnp.stack([a_lam_q1[j], a_lam_k1[j], a_lam_q2[j], a_lam_k2[j]])
            aw = a_w_out.shape[1]
            args = (norm_w[i], a_w_in[j], a_w_out[j], a_q_gain[j], a_k_gain[j], lamp, a_sub_gain[j], lam_init)
            hp, kp, vp = _diff_attn_layer(hp, pos_p, *args, None, f"l{i}p")
            cache = (cache_k_attn[j].reshape(dec_b, past, aw), cache_v_attn[j].reshape(dec_b, past, aw))
            hs, kn, vn = _diff_attn_layer(hs, pos_s, *args, cache, f"l{i}s")
            kp_l.append(kp.reshape(batch, s_len, aw // A_HD, A_HD))
            vp_l.append(vp.reshape(batch, s_len, aw // A_VD, A_VD))
            ks_l.append(kn.reshape(dec_b, dec_len, aw // A_HD, A_HD))
            vs_l.append(vn.reshape(dec_b, dec_len, aw // A_VD, A_VD))
        elif kind == 1:
            heads = state_ret.shape[2]
            s0 = jnp.zeros((batch, heads, R_DK, R_DV), F32)
            hp, st_p = _retention_layer(hp, pos_p, norm_w[i], r_w_in[j], r_w_out[j], s0,
                                        _tile(s_len, 256), f"l{i}p")
            hs, st_s = _retention_layer(hs, pos_s, norm_w[i], r_w_in[j], r_w_out[j],
                                        state_ret[j].astype(F32), dec_len, f"l{i}s")
            sp_l.append(st_p)
            ss_l.append(st_s)
        else:
            wp, bp = _mix_tables(c_w_s[j], c_b_s[j], M_CHUNK)
            ws, bs = _mix_tables(c_w_s[j], c_b_s[j], dec_len)
            hp, _ = _cmlp_layer(hp, norm_w[i], c_w_in[j], c_w_out[j], c_v_gain[j], wp, bp, False, f"l{i}p")
            hs, v_s = _cmlp_layer(hs, norm_w[i], c_w_in[j], c_w_out[j], c_v_gain[j], ws, bs, True, f"l{i}s")
            vm_l.append(v_s.reshape(dec_b, dec_len, -1))

    return (hp.reshape(batch, s_len, d), hs.reshape(dec_b, dec_len, d),
            jnp.stack(kp_l), jnp.stack(vp_l), jnp.stack(ks_l), jnp.stack(vs_l),
            jnp.stack(sp_l), jnp.stack(ss_l), jnp.stack(vm_l))
```

```python
import functools
import math

import jax
import jax.numpy as jnp
from jax import lax
from jax.experimental import pallas as pl
from jax.experimental.pallas import tpu as pltpu

F32 = jnp.float32
BF16 = jnp.bfloat16

EPS = 1e-6
CHUNK = 64
ROPE_THETA = 10000.0
N_MIXERS = 3
A_HD = 64
A_VD = 2 * A_HD
R_DK = 256
R_DV = 2 * R_DK
M_GROUPS = 8
M_CHUNK = 128
LOG2E = 1.4426950408889634

LANES = 128
MXU_COLS = 256
VMEM_LIMIT = 56 * 1024 * 1024
NEG_BIG = -1e30


def _tile(n, pref):
    if n <= pref:
        return n
    t = pref
    while t >= 8:
        if n % t == 0:
            return t
        t -= 8
    return n


def _params(sem):
    return pltpu.CompilerParams(dimension_semantics=sem, vmem_limit_bytes=VMEM_LIMIT)


def _gelu(x):
    return 0.5 * x * (1.0 + jnp.tanh(0.7978845608028654 * (x + 0.044715 * (x * x * x))))


def _sigmoid(x):
    return 1.0 / (1.0 + jnp.exp(-x))


def _proj_kernel(*refs, epi, out_scale, n_out):
    xn_ref = refs[-1]
    o_refs = refs[-1 - n_out:-1]
    if epi == "qk":
        x_ref, nw_ref, w_ref, cos_ref, sin_ref, gain_ref, gsum_ref = refs[:7]
    elif epi == "rope256":
        x_ref, nw_ref, w_ref, cos_ref, sin_ref = refs[:5]
    else:
        x_ref, nw_ref, w_ref = refs[:3]

    def store(cols, val):
        for o_ref in o_refs:
            o_ref[:, cols] = val.astype(o_ref.dtype)

    @pl.when(pl.program_id(1) == 0)
    def _():
        x = x_ref[...]
        ms = jnp.mean(x * x, axis=-1, keepdims=True)
        xn_ref[...] = (x * lax.rsqrt(ms + EPS) * nw_ref[...]).astype(BF16)

    tn = w_ref.shape[1]
    if epi == "qk":
        cos = jnp.concatenate([cos_ref[...]] * 2, axis=1)
        sin = jnp.concatenate([sin_ref[...]] * 2, axis=1)
        gain = jnp.concatenate([gain_ref[...]] * 2, axis=1)
        lane = lax.broadcasted_iota(jnp.int32, (xn_ref.shape[0], MXU_COLS), 1)
        first_half = (lane % A_HD) < (A_HD // 2)
        slabs = [slice(c * MXU_COLS, (c + 1) * MXU_COLS) for c in range(tn // MXU_COLS)]

        def project(sl):
            return jnp.dot(xn_ref[...], w_ref[:, sl], preferred_element_type=F32)

        def finish(sl, zc):
            sq = jnp.dot((zc * zc).astype(BF16), gsum_ref[...], preferred_element_type=F32)
            zn = zc * lax.rsqrt(sq * (1.0 / A_HD) + EPS) * gain
            partner = jnp.where(first_half,
                                pltpu.roll(zn, MXU_COLS - A_HD // 2, 1),
                                pltpu.roll(zn, A_HD // 2, 1))
            out = zn * cos + partner * sin
            if out_scale != 1.0:
                out = out * out_scale
            store(sl, out)

        ahead = 2
        zs = [project(sl) for sl in slabs[:ahead]]
        for c, sl in enumerate(slabs):
            if c + ahead < len(slabs):
                zs.append(project(slabs[c + ahead]))
            finish(sl, zs[c])
        return
    z = jnp.dot(xn_ref[...], w_ref[...], preferred_element_type=F32)
    if epi == "rope256":
        cos = cos_ref[...]
        sin = sin_ref[...]
        for c in range(tn // R_DK):
            lo, hi = slice(c * R_DK, c * R_DK + LANES), slice(c * R_DK + LANES, (c + 1) * R_DK)
            x1, x2 = z[:, lo], z[:, hi]
            o1 = x1 * cos - x2 * sin
            o2 = x2 * cos + x1 * sin
            if out_scale != 1.0:
                o1 = o1 * out_scale
                o2 = o2 * out_scale
            store(lo, o1)
            store(hi, o2)
    else:
        store(slice(None), z)


def _proj(x, nw, w, out_dtypes, *, epi="none", cos=None, sin=None, gain=None,
          out_scale=1.0, name):
    m, d = x.shape
    n = w.shape[1]
    tm = _tile(m, 1024)
    tn = _tile(n, 1024)
    single = not isinstance(out_dtypes, tuple)
    if single:
        out_dtypes = (out_dtypes,)
    in_specs = [pl.BlockSpec((tm, d), lambda i, j: (i, 0)),
                pl.BlockSpec((1, d), lambda i, j: (0, 0)),
                pl.BlockSpec((d, tn), lambda i, j: (0, j))]
    args = [x, nw.reshape(1, d), w]
    if epi in ("qk", "rope256"):
        in_specs += [pl.BlockSpec((tm, LANES), lambda i, j: (i, 0)),
                     pl.BlockSpec((tm, LANES), lambda i, j: (i, 0))]
        args += [cos, sin]
    if epi == "qk":
        gidx = jnp.arange(MXU_COLS) // A_HD
        gsum = (gidx[:, None] == gidx[None, :]).astype(BF16)
        in_specs += [pl.BlockSpec((1, LANES), lambda i, j: (0, 0)),
                     pl.BlockSpec((MXU_COLS, MXU_COLS), lambda i, j: (0, 0))]
        args += [jnp.tile(gain.reshape(1, A_HD), (1, LANES // A_HD)), gsum]
    res = pl.pallas_call(
        functools.partial(_proj_kernel, epi=epi, out_scale=out_scale, n_out=len(out_dtypes)),
        out_shape=tuple(jax.ShapeDtypeStruct((m, n), dt) for dt in out_dtypes),
        grid=(m // tm, n // tn),
        in_specs=in_specs,
        out_specs=tuple(pl.BlockSpec((tm, tn), lambda i, j: (i, j)) for _ in out_dtypes),
        scratch_shapes=[pltpu.VMEM((tm, d), BF16)],
        compiler_params=_params(("parallel", "arbitrary")),
        name=name,
    )(*args)
    return res[0] if single else res


def _outproj_kernel(a_ref, w_ref, h_ref, o_ref):
    o_ref[...] = h_ref[...] + jnp.dot(a_ref[...], w_ref[...], preferred_element_type=F32)


def _outproj(a, w, h, *, name):
    m, k = a.shape
    n = w.shape[1]
    tm = _tile(m, 1024)
    tn = _tile(n, 1024)
    return pl.pallas_call(
        _outproj_kernel,
        out_shape=jax.ShapeDtypeStruct((m, n), F32),
        grid=(m // tm, n // tn),
        in_specs=[pl.BlockSpec((tm, k), lambda i, j: (i, 0)),
                  pl.BlockSpec((k, tn), lambda i, j: (0, j)),
                  pl.BlockSpec((tm, tn), lambda i, j: (i, j))],
        out_specs=pl.BlockSpec((tm, tn), lambda i, j: (i, j)),
        compiler_params=_params(("parallel", "parallel")),
        name=name,
    )(a, w, h)


def _lambda(lamp_ref, lam_init):
    lp = lamp_ref[...]
    s1 = jnp.sum(lp[0:1, :] * lp[1:2, :], axis=-1, keepdims=True)
    s2 = jnp.sum(lp[2:3, :] * lp[3:4, :], axis=-1, keepdims=True)
    return jnp.exp(s1) - jnp.exp(s2) + lam_init


V_AUG = A_VD + 16


LOOKAHEAD = 4


def _attn_prompt_kernel(q_ref, k_ref, v_ref, g_ref, lamp_ref, sg_ref, o_ref,
                        qpad_ref, vt_ref, m_ref, acc_ref, s_ref, mx_ref, *, tq, tk, lam_init):
    i = pl.program_id(1)

    @pl.when(i == 0)
    def _():
        aug = lax.broadcasted_iota(jnp.int32, (V_AUG - A_VD, tk), 0)
        ones_row = jnp.where(aug == 0, 1.0, 0.0).astype(BF16)

        def fill(t, carry):
            vj = v_ref[pl.ds(pl.multiple_of(t * tk, tk), tk), :].astype(F32)
            vt_ref[t, :A_VD, :] = vj.T.astype(BF16)
            vt_ref[t, A_VD:, :] = ones_row
            return carry

        lax.fori_loop(0, vt_ref.shape[0], fill, 0)

    qt = q_ref[...].astype(F32).T
    row = lax.broadcasted_iota(jnp.int32, qt.shape, 0)
    qpad_ref[:, :tq] = jnp.where(row < A_HD, qt, 0.0).astype(BF16)
    qpad_ref[:, tq:] = jnp.where(row >= A_HD, qt, 0.0).astype(BF16)
    m_ref[...] = jnp.full(m_ref.shape, NEG_BIG, F32)
    acc_ref[...] = jnp.zeros(acc_ref.shape, F32)

    sw = s_ref.shape[2]
    n_strips = 2 * tq // sw
    la = min(LOOKAHEAD, n_strips - 1)

    def scores(j, c):
        kj = k_ref[pl.ds(pl.multiple_of(j * tk, tk), tk), :]
        s = jnp.dot(kj, qpad_ref[:, c * sw:(c + 1) * sw], preferred_element_type=F32)
        s_ref[c] = s
        mx_ref[c] = jnp.max(s, axis=0, keepdims=True)

    def update(j, c, mask):
        cs = slice(c * sw, (c + 1) * sw)
        s = s_ref[c]
        if mask is None:
            mx = mx_ref[c]
        else:
            s = jnp.where(mask, s, NEG_BIG)
            mx = jnp.max(s, axis=0, keepdims=True)
        m_old = m_ref[:, cs]
        m_new = jnp.maximum(m_old, mx)
        alpha = jnp.exp2(m_old - m_new)
        p = jnp.exp2(s - m_new).astype(BF16)
        pv = jnp.dot(vt_ref[j], p, preferred_element_type=F32)
        acc_ref[:, cs] = alpha * acc_ref[:, cs] + pv
        m_ref[:, cs] = m_new

    n_sub = tq // tk
    n_full = i * n_sub

    for c in range(la):
        scores(0, c)

    def body(j, carry):
        for c in range(n_strips):
            update(j, c, None)
            if c + la < n_strips:
                scores(j, c + la)
            else:
                scores(j + 1, c + la - n_strips)
        return carry

    lax.fori_loop(0, n_full, body, 0)

    r = lax.broadcasted_iota(jnp.int32, (tk, sw), 0)
    cc = lax.broadcasted_iota(jnp.int32, (tk, sw), 1)
    units = []
    for d in range(n_sub):
        for c in range(n_strips):
            q_lo, k_lo = (c * sw) % tq, d * tk
            if q_lo + sw <= k_lo:
                continue
            full = q_lo >= k_lo + tk
            units.append((d, c, None if full else (k_lo + r) // CHUNK <= (q_lo + cc) // CHUNK))
    assert [u[:2] for u in units[:la]] == [(0, c) for c in range(la)]
    for n, (d, c, mask) in enumerate(units):
        update(n_full + d, c, mask)
        if n + la < len(units):
            nd, nc, _ = units[n + la]
            assert all(u[1] != nc for u in units[n + 1:n + la])
            scores(n_full + nd, nc)

    lam = _lambda(lamp_ref, lam_init)
    acc = acc_ref[...]
    inv_l = 1.0 / acc[A_VD:A_VD + 1, :]
    ot = acc[:A_VD, :tq] * inv_l[:, :tq] - lam * (acc[:A_VD, tq:] * inv_l[:, tq:])
    ms = jnp.mean(ot * ot, axis=0, keepdims=True)
    ot = ot * lax.rsqrt(ms + EPS) * (sg_ref[...] * (1.0 - lam_init))
    g = g_ref[...].astype(F32)
    o_ref[...] = (ot.T * (g * _sigmoid(g))).astype(o_ref.dtype)


def _attn_prompt(q, k, v, g, lamp, sub_gain, lam_init, *, name):
    s, w = q.shape
    heads = w // A_VD
    tq = _tile(s, 1024)
    tk = _tile(tq, 512)
    sw = min(MXU_COLS, tq)
    return pl.pallas_call(
        functools.partial(_attn_prompt_kernel, tq=tq, tk=tk, lam_init=lam_init),
        out_shape=jax.ShapeDtypeStruct((s, w), BF16),
        grid=(heads, s // tq),
        in_specs=[pl.BlockSpec((tq, A_VD), lambda h, i: (i, h)),
                  pl.BlockSpec((s, A_VD), lambda h, i: (0, h)),
                  pl.BlockSpec((s, A_VD), lambda h, i: (0, h)),
                  pl.BlockSpec((tq, A_VD), lambda h, i: (i, h)),
                  pl.BlockSpec((4, A_HD), lambda h, i: (0, 0)),
                  pl.BlockSpec((A_VD, 1), lambda h, i: (0, 0))],
        out_specs=pl.BlockSpec((tq, A_VD), lambda h, i: (i, h)),
        scratch_shapes=[pltpu.VMEM((A_VD, 2 * tq), BF16),
                        pltpu.VMEM((s // tk, V_AUG, tk), BF16),
                        pltpu.VMEM((1, 2 * tq), F32),
                        pltpu.VMEM((V_AUG, 2 * tq), F32),
                        pltpu.VMEM((2 * tq // sw, tk, sw), F32),
                        pltpu.VMEM((2 * tq // sw, 1, sw), F32)],
        compiler_params=_params(("parallel", "arbitrary")),
        name=name,
    )(q, k, v, g, lamp, sub_gain.reshape(A_VD, 1))


def _attn_sample_kernel(q_ref, kn_ref, vn_ref, g_ref, kc_ref, vc_ref, lamp_ref, sg_ref, o_ref, *, lam_init):
    q = q_ref[...]
    lane = lax.broadcasted_iota(jnp.int32, q.shape, 1)
    zero = jnp.zeros_like(q)
    kc = kc_ref[0].astype(BF16)
    kn = kn_ref[...].astype(BF16)
    nt = (((1,), (1,)), ((), ()))
    probs = []
    for half in range(2):
        qh = jnp.where((lane >= A_HD) == bool(half), q, zero)
        sc = lax.dot_general(qh, kc, nt, preferred_element_type=F32)
        sn = lax.dot_general(qh, kn, nt, preferred_element_type=F32)
        m = jnp.maximum(jnp.max(sc, axis=-1, keepdims=True), jnp.max(sn, axis=-1, keepdims=True))
        pc = jnp.exp2(sc - m)
        pn = jnp.exp2(sn - m)
        inv = 1.0 / (jnp.sum(pc, axis=-1, keepdims=True) + jnp.sum(pn, axis=-1, keepdims=True))
        probs.append((pc * inv, pn * inv))
    lam = _lambda(lamp_ref, lam_init)
    ac = (probs[0][0] - lam * probs[1][0]).astype(BF16)
    an = (probs[0][1] - lam * probs[1][1]).astype(BF16)
    o = (jnp.dot(ac, vc_ref[0].astype(BF16), preferred_element_type=F32)
         + jnp.dot(an, vn_ref[...].astype(BF16), preferred_element_type=F32))
    ms = jnp.mean(o * o, axis=-1, keepdims=True)
    g = g_ref[...].astype(F32)
    o = o * lax.rsqrt(ms + EPS) * (sg_ref[...] * (1.0 - lam_init))
    o_ref[...] = (o * (g * _sigmoid(g))).astype(o_ref.dtype)


def _attn_sample(q, k, v, g, kc, vc, lamp, sub_gain, lam_init, *, name):
    bsz, past, w = kc.shape
    heads = w // A_VD
    ln = q.shape[0] // bsz
    assert past % CHUNK == 0 and ln <= CHUNK
    row = lambda b, h: (b, h)
    cache = lambda b, h: (b, 0, h)
    return pl.pallas_call(
        functools.partial(_attn_sample_kernel, lam_init=lam_init),
        out_shape=jax.ShapeDtypeStruct(q.shape, BF16),
        grid=(bsz, heads),
        in_specs=[pl.BlockSpec((ln, A_VD), row),
                  pl.BlockSpec((ln, A_VD), row),
                  pl.BlockSpec((ln, A_VD), row),
                  pl.BlockSpec((ln, A_VD), row),
                  pl.BlockSpec((1, past, A_VD), cache),
                  pl.BlockSpec((1, past, A_VD), cache),
                  pl.BlockSpec((4, A_HD), lambda b, h: (0, 0)),
                  pl.BlockSpec((1, A_VD), lambda b, h: (0, 0))],
        out_specs=pl.BlockSpec((ln, A_VD), row),
        compiler_params=_params(("parallel", "parallel")),
        name=name,
    )(q, k, v, g, kc, vc, lamp, sub_gain.reshape(1, A_VD))


def _retention_kernel(q_ref, k_ref, v_ref, g_ref, s0_ref, dec_ref, cd_ref, kd_ref, gl_ref,
                      o_ref, sout_ref, st_ref, *, lc):
    c = pl.program_id(2)

    @pl.when(c == 0)
    def _():
        st_ref[...] = s0_ref[0, 0]

    q = q_ref[...]
    k = k_ref[...]
    v = v_ref[...]
    st = st_ref[...]
    inner = lax.dot_general(q, k, (((1,), (1,)), ((), ())), preferred_element_type=F32) * dec_ref[0]
    cross = jnp.dot(q, st.astype(BF16), preferred_element_type=F32)
    cd = jnp.concatenate([cd_ref[0]] * (R_DV // LANES), axis=1)
    o = jnp.dot(inner.astype(BF16), v, preferred_element_type=F32) + cross * cd
    ks = k.astype(F32) * jnp.concatenate([kd_ref[0]] * (R_DK // LANES), axis=1)
    vv = v
    if lc < LANES:
        ks = jnp.concatenate([ks, jnp.zeros((LANES - lc, R_DK), F32)], axis=0)
        vv = jnp.concatenate([v, jnp.zeros((LANES - lc, R_DV), BF16)], axis=0)
    gl = jnp.concatenate([gl_ref[0]] * (R_DV // LANES), axis=1)
    st_new = st * gl + jnp.dot(ks.T.astype(BF16), vv, preferred_element_type=F32)
    st_ref[...] = st_new
    ms = jnp.mean(o * o, axis=-1, keepdims=True)
    g = g_ref[...].astype(F32)
    o_ref[...] = (o * lax.rsqrt(ms + EPS) * (g * _sigmoid(g))).astype(o_ref.dtype)

    @pl.when(c == pl.num_programs(2) - 1)
    def _():
        sout_ref[0, 0] = st_new


def _retention(q, k, vg, s0, lc, *, name):
    bsz, heads = s0.shape[:2]
    t = q.shape[0] // bsz
    nc = t // lc
    lg = jnp.log1p(-(2.0 ** (-5.0 - jnp.arange(heads, dtype=F32))))
    idx = jnp.arange(lc, dtype=F32)
    diff = idx[:, None] - idx[None, :]
    dec = jnp.where(diff >= 0, jnp.exp(lg[:, None, None] * jnp.maximum(diff, 0.0)), 0.0)
    cd = jnp.broadcast_to(jnp.exp(lg[:, None] * (idx[None, :] + 1.0))[:, :, None], (heads, lc, LANES))
    kd = jnp.broadcast_to(jnp.exp(lg[:, None] * (lc - 1.0 - idx[None, :]))[:, :, None], (heads, lc, LANES))
    gl = jnp.broadcast_to(jnp.exp(lg * lc)[:, None, None], (heads, 1, LANES))
    rows = lambda b, h, c: (b * nc + c, h)
    gate = lambda b, h, c: (b * nc + c, heads + h)
    tab = lambda b, h, c: (h, 0, 0)
    state = lambda b, h, c: (b, h, 0, 0)
    return pl.pallas_call(
        functools.partial(_retention_kernel, lc=lc),
        out_shape=(jax.ShapeDtypeStruct((vg.shape[0], heads * R_DV), BF16),
                   jax.ShapeDtypeStruct(s0.shape, F32)),
        grid=(bsz, heads, nc),
        in_specs=[pl.BlockSpec((lc, R_DK), rows),
                  pl.BlockSpec((lc, R_DK), rows),
                  pl.BlockSpec((lc, R_DV), rows),
                  pl.BlockSpec((lc, R_DV), gate),
                  pl.BlockSpec((1, 1, R_DK, R_DV), state),
                  pl.BlockSpec((1, lc, lc), tab),
                  pl.BlockSpec((1, lc, LANES), tab),
                  pl.BlockSpec((1, lc, LANES), tab),
                  pl.BlockSpec((1, 1, LANES), tab)],
        out_specs=(pl.BlockSpec((lc, R_DV), rows),
                   pl.BlockSpec((1, 1, R_DK, R_DV), state)),
        scratch_shapes=[pltpu.VMEM((R_DK, R_DV), F32)],
        compiler_params=_params(("parallel", "parallel", "arbitrary")),
        name=name,
    )(q, k, vg, vg, s0, dec, cd, kd, gl)


def _cmlp_kernel(u_ref, g_ref, v_ref, vg_ref, w_ref, b_ref, *out_refs, emit_v):
    if emit_v:
        a_ref, vn_ref = out_refs
    else:
        (a_ref,) = out_refs
    va = _gelu(v_ref[...])
    ms = jnp.mean(va * va, axis=-1, keepdims=True)
    vn = va * lax.rsqrt(ms + EPS) * vg_ref[...]
    if emit_v:
        vn_ref[...] = vn
    vb = vn.astype(BF16)
    gd = vn.shape[1] // M_GROUPS
    for grp in range(M_GROUPS):
        sl = slice(grp * gd, (grp + 1) * gd)
        bias = jnp.concatenate([b_ref[grp]] * (gd // LANES), axis=1)
        mix = jnp.dot(w_ref[grp], vb[:, sl], preferred_element_type=F32) + bias
        g = g_ref[:, sl].astype(F32)
        a_ref[:, sl] = (_gelu(u_ref[:, sl].astype(F32)) * mix * (g * _sigmoid(g))).astype(a_ref.dtype)


def _cmlp(ug, v, v_gain, wmix, bmix, *, emit_v, name):
    m, w = v.shape
    t = wmix.shape[1]
    out_shape = [jax.ShapeDtypeStruct((m, w), BF16)]
    out_specs = [pl.BlockSpec((t, w), lambda i: (i, 0))]
    if emit_v:
        out_shape.append(jax.ShapeDtypeStruct((m, w), F32))
        out_specs.append(pl.BlockSpec((t, w), lambda i: (i, 0)))
    res = pl.pallas_call(
        functools.partial(_cmlp_kernel, emit_v=emit_v),
        out_shape=tuple(out_shape),
        grid=(m // t,),
        in_specs=[pl.BlockSpec((t, w), lambda i: (i, 0)),
                  pl.BlockSpec((t, w), lambda i: (i, 1)),
                  pl.BlockSpec((t, w), lambda i: (i, 0)),
                  pl.BlockSpec((1, w), lambda i: (0, 0)),
                  pl.BlockSpec((M_GROUPS, t, t), lambda i: (0, 0, 0)),
                  pl.BlockSpec((M_GROUPS, t, LANES), lambda i: (0, 0, 0))],
        out_specs=tuple(out_specs),
        compiler_params=_params(("parallel",)),
        name=name,
    )(ug, ug, v, v_gain.reshape(1, w), wmix, bmix)
    return res if emit_v else res[0]


def _rope_tables(pos, d, signed):
    inv = ROPE_THETA ** (-jnp.arange(0, d, 2, dtype=F32) / d)
    ang = pos.astype(F32)[:, None] * inv[None, :]
    cos, sin = jnp.cos(ang), jnp.sin(ang)
    reps = LANES // (d // 2)
    if signed:
        return (jnp.tile(cos, (1, reps)),
                jnp.tile(jnp.concatenate([-sin, sin], axis=1), (1, reps // 2)))
    return jnp.tile(cos, (1, reps)), jnp.tile(sin, (1, reps))


def _diff_attn_layer(h, pos, nw, w_in, w_out, q_gain, k_gain, lamp, sub_gain, lam_init, cache, tag):
    w = w_out.shape[0]
    cos, sin = _rope_tables(pos, A_HD, signed=True)
    wq, wk, wv, wg = (w_in[:, i * w:(i + 1) * w].astype(BF16) for i in range(4))
    q = _proj(h, nw, wq, BF16, epi="qk", cos=cos, sin=sin, gain=q_gain,
              out_scale=(A_HD ** -0.5) * LOG2E, name=f"{tag}_proj_q")
    k, kb = _proj(h, nw, wk, (F32, BF16), epi="qk", cos=cos, sin=sin, gain=k_gain, name=f"{tag}_proj_k")
    v, vb = _proj(h, nw, wv, (F32, BF16), name=f"{tag}_proj_v")
    g = _proj(h, nw, wg, BF16, name=f"{tag}_proj_g")
    if cache is None:
        o = _attn_prompt(q, kb, vb, g, lamp, sub_gain, lam_init, name=f"{tag}_attn")
    else:
        kc, vc = cache
        o = _attn_sample(q, k, v, g, kc, vc, lamp, sub_gain, lam_init, name=f"{tag}_attn")
    return _outproj(o, w_out.astype(BF16), h, name=f"{tag}_out"), k, v


def _retention_layer(h, pos, nw, w_in, w_out, s0, lc, tag):
    heads = s0.shape[1]
    qk_w, v_w = heads * R_DK, heads * R_DV
    cos, sin = _rope_tables(pos, R_DK, signed=False)
    wq = w_in[:, :qk_w].astype(BF16)
    wk = w_in[:, qk_w:2 * qk_w].astype(BF16)
    assert w_in.shape[1] == 2 * qk_w + 2 * v_w
    wvg = w_in[:, 2 * qk_w:].astype(BF16)
    q = _proj(h, nw, wq, BF16, epi="rope256", cos=cos, sin=sin, name=f"{tag}_proj_q")
    k = _proj(h, nw, wk, BF16, epi="rope256", cos=cos, sin=sin, out_scale=R_DK ** -0.5,
              name=f"{tag}_proj_k")
    vg = _proj(h, nw, wvg, BF16, name=f"{tag}_proj_vg")
    o, s_new = _retention(q, k, vg, s0, lc, name=f"{tag}_ret")
    return _outproj(o, w_out.astype(BF16), h, name=f"{tag}_out"), s_new


def _cmlp_layer(h, nw, w_in, w_out, v_gain, wmix, bmix, emit_v, tag):
    w = w_out.shape[0]
    wu, wv, wg = (w_in[:, i * w:(i + 1) * w].astype(BF16) for i in range(3))
    ug = _proj(h, nw, jnp.concatenate([wu, wg], axis=1), BF16, name=f"{tag}_proj_ug")
    v = _proj(h, nw, wv, F32, name=f"{tag}_proj_v")
    res = _cmlp(ug, v, v_gain, wmix, bmix, emit_v=emit_v, name=f"{tag}_mix")
    a, vn = res if emit_v else (res, None)
    return _outproj(a, w_out.astype(BF16), h, name=f"{tag}_out"), vn


def _mix_tables(w_s, b_s, chunk_len):
    groups = w_s.shape[0]
    wl = jnp.tril(w_s[:, :chunk_len, :chunk_len])
    reps = M_CHUNK // chunk_len
    eye = jnp.eye(reps, dtype=F32)
    wt = jnp.einsum("ab,gij->gaibj", eye, wl).reshape(groups, M_CHUNK, M_CHUNK)
    bt = jnp.tile(b_s[:, :chunk_len], (1, reps))
    return wt.astype(BF16), jnp.broadcast_to(bt[:, :, None], (groups, M_CHUNK, LANES))


def kernel(x_prompt, x_sample, cache_k_attn, cache_v_attn, state_ret, norm_w, a_w_in, a_w_out, a_q_gain, a_k_gain, a_lam_q1, a_lam_k1, a_lam_q2, a_lam_k2, a_sub_gain, r_w_in, r_w_out, c_w_in, c_w_out, c_v_gain, c_w_s, c_b_s):
    batch, s_len, d = x_prompt.shape
    dec_b, dec_len, _ = x_sample.shape
    past = cache_k_attn.shape[2]
    depth = norm_w.shape[0]
    assert batch == 1 and M_CHUNK % dec_len == 0 and s_len % M_CHUNK == 0

    hp = x_prompt.reshape(s_len, d)
    hs = x_sample.reshape(dec_b * dec_len, d)
    pos_p = jnp.arange(s_len, dtype=jnp.int32)
    pos_s = jnp.tile(past + jnp.arange(dec_len, dtype=jnp.int32), dec_b)

    kp_l, vp_l, ks_l, vs_l, sp_l, ss_l, vm_l = [], [], [], [], [], [], []
    for i in range(depth):
        kind, j = i % N_MIXERS, i // N_MIXERS
        if kind == 0:
            lam_init = 0.8 - 0.6 * math.exp(-0.3 * i)
            lamp = jnp.stack([a_lam_q1[j], a_lam_k1[j], a_lam_q2[j], a_lam_k2[j]])
            aw = a_w_out.shape[1]
            args = (norm_w[i], a_w_in[j], a_w_out[j], a_q_gain[j], a_k_gain[j], lamp, a_sub_gain[j], lam_init)
            hp, kp, vp = _diff_attn_layer(hp, pos_p, *args, None, f"l{i}p")
            cache = (cache_k_attn[j].reshape(dec_b, past, aw), cache_v_attn[j].reshape(dec_b, past, aw))
            hs, kn, vn = _diff_attn_layer(hs, pos_s, *args, cache, f"l{i}s")
            kp_l.append(kp.reshape(batch, s_len, aw // A_HD, A_HD))
            vp_l.append(vp.reshape(batch, s_len, aw // A_VD, A_VD))
            ks_l.append(kn.reshape(dec_b, dec_len, aw // A_HD, A_HD))
            vs_l.append(vn.reshape(dec_b, dec_len, aw // A_VD, A_VD))
        elif kind == 1:
            heads = state_ret.shape[2]
            s0 = jnp.zeros((batch, heads, R_DK, R_DV), F32)
            hp, st_p = _retention_layer(hp, pos_p, norm_w[i], r_w_in[j], r_w_out[j], s0,
                                        _tile(s_len, 256), f"l{i}p")
            hs, st_s = _retention_layer(hs, pos_s, norm_w[i], r_w_in[j], r_w_out[j],
                                        state_ret[j].astype(F32), dec_len, f"l{i}s")
            sp_l.append(st_p)
            ss_l.append(st_s)
        else:
            wp, bp = _mix_tables(c_w_s[j], c_b_s[j], M_CHUNK)
            ws, bs = _mix_tables(c_w_s[j], c_b_s[j], dec_len)
            hp, _ = _cmlp_layer(hp, norm_w[i], c_w_in[j], c_w_out[j], c_v_gain[j], wp, bp, False, f"l{i}p")
            hs, v_s = _cmlp_layer(hs, norm_w[i], c_w_in[j], c_w_out[j], c_v_gain[j], ws, bs, True, f"l{i}s")
            vm_l.append(v_s.reshape(dec_b, dec_len, -1))

    return (hp.reshape(batch, s_len, d), hs.reshape(dec_b, dec_len, d),
            jnp.stack(kp_l), jnp.stack(vp_l), jnp.stack(ks_l), jnp.stack(vs_l),
            jnp.stack(sp_l), jnp.stack(ss_l), jnp.stack(vm_l))
```

```python
import functools
import math

import jax
import jax.numpy as jnp
from jax import lax
from jax.experimental import pallas as pl
from jax.experimental.pallas import tpu as pltpu

F32 = jnp.float32
BF16 = jnp.bfloat16

EPS = 1e-6
CHUNK = 64
ROPE_THETA = 10000.0
N_MIXERS = 3
A_HD = 64
A_VD = 2 * A_HD
R_DK = 256
R_DV = 2 * R_DK
M_GROUPS = 8
M_CHUNK = 128
LOG2E = 1.4426950408889634

LANES = 128
MXU_COLS = 256
VMEM_LIMIT = 56 * 1024 * 1024
NEG_BIG = -1e30


def _tile(n, pref):
    if n <= pref:
        return n
    t = pref
    while t >= 8:
        if n % t == 0:
            return t
        t -= 8
    return n


def _params(sem):
    return pltpu.CompilerParams(dimension_semantics=sem, vmem_limit_bytes=VMEM_LIMIT)


def _gelu(x):
    return 0.5 * x * (1.0 + jnp.tanh(0.7978845608028654 * (x + 0.044715 * (x * x * x))))


def _sigmoid(x):
    return 1.0 / (1.0 + jnp.exp(-x))


def _proj_kernel(*refs, epi, out_scale, n_out):
    xn_ref = refs[-1]
    o_refs = refs[-1 - n_out:-1]
    if epi == "qk":
        x_ref, nw_ref, w_ref, cos_ref, sin_ref, gain_ref, gsum_ref = refs[:7]
    elif epi == "rope256":
        x_ref, nw_ref, w_ref, cos_ref, sin_ref = refs[:5]
    else:
        x_ref, nw_ref, w_ref = refs[:3]

    def store(cols, val):
        for o_ref in o_refs:
            o_ref[:, cols] = val.astype(o_ref.dtype)

    @pl.when(pl.program_id(1) == 0)
    def _():
        x = x_ref[...]
        ms = jnp.mean(x * x, axis=-1, keepdims=True)
        xn_ref[...] = (x * lax.rsqrt(ms + EPS) * nw_ref[...]).astype(BF16)

    tn = w_ref.shape[1]
    if epi == "qk":
        cos = jnp.concatenate([cos_ref[...]] * 2, axis=1)
        sin = jnp.concatenate([sin_ref[...]] * 2, axis=1)
        gain = jnp.concatenate([gain_ref[...]] * 2, axis=1)
        lane = lax.broadcasted_iota(jnp.int32, (xn_ref.shape[0], MXU_COLS), 1)
        first_half = (lane % A_HD) < (A_HD // 2)
        slabs = [slice(c * MXU_COLS, (c + 1) * MXU_COLS) for c in range(tn // MXU_COLS)]

        def project(sl):
            return jnp.dot(xn_ref[...], w_ref[:, sl], preferred_element_type=F32)

        def finish(sl, zc):
            sq = jnp.dot((zc * zc).astype(BF16), gsum_ref[...], preferred_element_type=F32)
            zn = zc * lax.rsqrt(sq * (1.0 / A_HD) + EPS) * gain
            partner = jnp.where(first_half,
                                pltpu.roll(zn, MXU_COLS - A_HD // 2, 1),
                                pltpu.roll(zn, A_HD // 2, 1))
            out = zn * cos + partner * sin
            if out_scale != 1.0:
                out = out * out_scale
            store(sl, out)

        ahead = 2
        zs = [project(sl) for sl in slabs[:ahead]]
        for c, sl in enumerate(slabs):
            if c + ahead < len(slabs):
                zs.append(project(slabs[c + ahead]))
            finish(sl, zs[c])
        return
    z = jnp.dot(xn_ref[...], w_ref[...], preferred_element_type=F32)
    if epi == "rope256":
        cos = cos_ref[...]
        sin = sin_ref[...]
        for c in range(tn // R_DK):
            lo, hi = slice(c * R_DK, c * R_DK + LANES), slice(c * R_DK + LANES, (c + 1) * R_DK)
            x1, x2 = z[:, lo], z[:, hi]
            o1 = x1 * cos - x2 * sin
            o2 = x2 * cos + x1 * sin
            if out_scale != 1.0:
                o1 = o1 * out_scale
                o2 = o2 * out_scale
            store(lo, o1)
            store(hi, o2)
    else:
        store(slice(None), z)


def _proj(x, nw, w, out_dtypes, *, epi="none", cos=None, sin=None, gain=None,
          out_scale=1.0, stack=None, name):
    m, d = x.shape
    n = w.shape[1]
    tm = _tile(m, 1024)
    tn = _tile(n, 1024)
    single = not isinstance(out_dtypes, tuple)
    if single:
        out_dtypes = (out_dtypes,)
    out_shape = [jax.ShapeDtypeStruct((m, n), dt) for dt in out_dtypes]
    out_specs = [pl.BlockSpec((tm, tn), lambda i, j: (i, j)) for _ in out_dtypes]
    aliases = {}
    in_specs = [pl.BlockSpec((tm, d), lambda i, j: (i, 0)),
                pl.BlockSpec((1, d), lambda i, j: (0, 0)),
                pl.BlockSpec((d, tn), lambda i, j: (0, j))]
    args = [x, nw.reshape(1, d), w]
    if epi in ("qk", "rope256"):
        in_specs += [pl.BlockSpec((tm, LANES), lambda i, j: (i, 0)),
                     pl.BlockSpec((tm, LANES), lambda i, j: (i, 0))]
        args += [cos, sin]
    if epi == "qk":
        gidx = jnp.arange(MXU_COLS) // A_HD
        gsum = (gidx[:, None] == gidx[None, :]).astype(BF16)
        in_specs += [pl.BlockSpec((1, LANES), lambda i, j: (0, 0)),
                     pl.BlockSpec((MXU_COLS, MXU_COLS), lambda i, j: (0, 0))]
        args += [jnp.tile(gain.reshape(1, A_HD), (1, LANES // A_HD)), gsum]
    if stack is not None:
        layers, layer, buf = stack
        out_shape[0] = jax.ShapeDtypeStruct((layers, m, n), out_dtypes[0])
        out_specs[0] = pl.BlockSpec((None, tm, tn), lambda i, j: (layer, i, j))
        if buf is not None:
            aliases = {len(args): 0}
            in_specs.append(pl.BlockSpec(memory_space=pl.ANY))
            args.append(buf)
    res = pl.pallas_call(
        functools.partial(_proj_kernel, epi=epi, out_scale=out_scale, n_out=len(out_dtypes)),
        out_shape=tuple(out_shape),
        grid=(m // tm, n // tn),
        in_specs=in_specs,
        out_specs=tuple(out_specs),
        scratch_shapes=[pltpu.VMEM((tm, d), BF16)],
        input_output_aliases=aliases,
        compiler_params=_params(("parallel", "arbitrary")),
        name=name,
    )(*args)
    return res[0] if single else res


def _outproj_kernel(a_ref, w_ref, h_ref, o_ref):
    o_ref[...] = h_ref[...] + jnp.dot(a_ref[...], w_ref[...], preferred_element_type=F32)


def _outproj(a, w, h, *, name):
    m, k = a.shape
    n = w.shape[1]
    tm = _tile(m, 1024)
    tn = _tile(n, 1024)
    return pl.pallas_call(
        _outproj_kernel,
        out_shape=jax.ShapeDtypeStruct((m, n), F32),
        grid=(m // tm, n // tn),
        in_specs=[pl.BlockSpec((tm, k), lambda i, j: (i, 0)),
                  pl.BlockSpec((k, tn), lambda i, j: (0, j)),
                  pl.BlockSpec((tm, tn), lambda i, j: (i, j))],
        out_specs=pl.BlockSpec((tm, tn), lambda i, j: (i, j)),
        compiler_params=_params(("parallel", "parallel")),
        name=name,
    )(a, w, h)


def _lambda(lamp_ref, lam_init):
    lp = lamp_ref[...]
    s1 = jnp.sum(lp[0:1, :] * lp[1:2, :], axis=-1, keepdims=True)
    s2 = jnp.sum(lp[2:3, :] * lp[3:4, :], axis=-1, keepdims=True)
    return jnp.exp(s1) - jnp.exp(s2) + lam_init


V_AUG = A_VD + 16


LOOKAHEAD = 4


def _attn_prompt_kernel(q_ref, k_ref, v_ref, g_ref, lamp_ref, sg_ref, o_ref,
                        qpad_ref, vt_ref, m_ref, acc_ref, s_ref, mx_ref, *, tq, tk, lam_init):
    i = pl.program_id(1)

    @pl.when(i == 0)
    def _():
        aug = lax.broadcasted_iota(jnp.int32, (V_AUG - A_VD, tk), 0)
        ones_row = jnp.where(aug == 0, 1.0, 0.0).astype(BF16)

        def fill(t, carry):
            vj = v_ref[pl.ds(pl.multiple_of(t * tk, tk), tk), :].astype(F32)
            vt_ref[t, :A_VD, :] = vj.T.astype(BF16)
            vt_ref[t, A_VD:, :] = ones_row
            return carry

        lax.fori_loop(0, vt_ref.shape[0], fill, 0)

    qt = q_ref[...].astype(F32).T
    row = lax.broadcasted_iota(jnp.int32, qt.shape, 0)
    qpad_ref[:, :tq] = jnp.where(row < A_HD, qt, 0.0).astype(BF16)
    qpad_ref[:, tq:] = jnp.where(row >= A_HD, qt, 0.0).astype(BF16)
    m_ref[...] = jnp.full(m_ref.shape, NEG_BIG, F32)
    acc_ref[...] = jnp.zeros(acc_ref.shape, F32)

    sw = s_ref.shape[2]
    n_strips = 2 * tq // sw
    la = min(LOOKAHEAD, n_strips - 1)

    def scores(j, c):
        kj = k_ref[pl.ds(pl.multiple_of(j * tk, tk), tk), :]
        s = jnp.dot(kj, qpad_ref[:, c * sw:(c + 1) * sw], preferred_element_type=F32)
        s_ref[c] = s
        mx_ref[c] = jnp.max(s, axis=0, keepdims=True)

    def update(j, c, mask):
        cs = slice(c * sw, (c + 1) * sw)
        s = s_ref[c]
        if mask is None:
            mx = mx_ref[c]
        else:
            s = jnp.where(mask, s, NEG_BIG)
            mx = jnp.max(s, axis=0, keepdims=True)
        m_old = m_ref[:, cs]
        m_new = jnp.maximum(m_old, mx)
        alpha = jnp.exp2(m_old - m_new)
        p = jnp.exp2(s - m_new).astype(BF16)
        pv = jnp.dot(vt_ref[j], p, preferred_element_type=F32)
        acc_ref[:, cs] = alpha * acc_ref[:, cs] + pv
        m_ref[:, cs] = m_new

    n_sub = tq // tk
    n_full = i * n_sub

    for c in range(la):
        scores(0, c)

    def body(j, carry):
        for c in range(n_strips):
            update(j, c, None)
            if c + la < n_strips:
                scores(j, c + la)
            else:
                scores(j + 1, c + la - n_strips)
        return carry

    lax.fori_loop(0, n_full, body, 0)

    r = lax.broadcasted_iota(jnp.int32, (tk, sw), 0)
    cc = lax.broadcasted_iota(jnp.int32, (tk, sw), 1)
    units = []
    for d in range(n_sub):
        for c in range(n_strips):
            q_lo, k_lo = (c * sw) % tq, d * tk
            if q_lo + sw <= k_lo:
                continue
            full = q_lo >= k_lo + tk
            units.append((d, c, None if full else (k_lo + r) // CHUNK <= (q_lo + cc) // CHUNK))
    assert [u[:2] for u in units[:la]] == [(0, c) for c in range(la)]
    for n, (d, c, mask) in enumerate(units):
        update(n_full + d, c, mask)
        if n + la < len(units):
            nd, nc, _ = units[n + la]
            assert all(u[1] != nc for u in units[n + 1:n + la])
            scores(n_full + nd, nc)

    lam = _lambda(lamp_ref, lam_init)
    acc = acc_ref[...]
    inv_l = 1.0 / acc[A_VD:A_VD + 1, :]
    ot = acc[:A_VD, :tq] * inv_l[:, :tq] - lam * (acc[:A_VD, tq:] * inv_l[:, tq:])
    ms = jnp.mean(ot * ot, axis=0, keepdims=True)
    ot = ot * lax.rsqrt(ms + EPS) * (sg_ref[...] * (1.0 - lam_init))
    g = g_ref[...].astype(F32)
    o_ref[...] = (ot.T * (g * _sigmoid(g))).astype(o_ref.dtype)


def _attn_prompt(q, k, v, g, lamp, sub_gain, lam_init, *, name):
    s, w = q.shape
    heads = w // A_VD
    tq = _tile(s, 1024)
    tk = _tile(tq, 512)
    sw = min(MXU_COLS, tq)
    return pl.pallas_call(
        functools.partial(_attn_prompt_kernel, tq=tq, tk=tk, lam_init=lam_init),
        out_shape=jax.ShapeDtypeStruct((s, w), BF16),
        grid=(heads, s // tq),
        in_specs=[pl.BlockSpec((tq, A_VD), lambda h, i: (i, h)),
                  pl.BlockSpec((s, A_VD), lambda h, i: (0, h)),
                  pl.BlockSpec((s, A_VD), lambda h, i: (0, h)),
                  pl.BlockSpec((tq, A_VD), lambda h, i: (i, h)),
                  pl.BlockSpec((4, A_HD), lambda h, i: (0, 0)),
                  pl.BlockSpec((A_VD, 1), lambda h, i: (0, 0))],
        out_specs=pl.BlockSpec((tq, A_VD), lambda h, i: (i, h)),
        scratch_shapes=[pltpu.VMEM((A_VD, 2 * tq), BF16),
                        pltpu.VMEM((s // tk, V_AUG, tk), BF16),
                        pltpu.VMEM((1, 2 * tq), F32),
                        pltpu.VMEM((V_AUG, 2 * tq), F32),
                        pltpu.VMEM((2 * tq // sw, tk, sw), F32),
                        pltpu.VMEM((2 * tq // sw, 1, sw), F32)],
        compiler_params=_params(("parallel", "arbitrary")),
        name=name,
    )(q, k, v, g, lamp, sub_gain.reshape(A_VD, 1))


def _attn_sample_kernel(q_ref, kn_ref, vn_ref, g_ref, kc_ref, vc_ref, lamp_ref, sg_ref, o_ref, *, lam_init):
    q = q_ref[...]
    lane = lax.broadcasted_iota(jnp.int32, q.shape, 1)
    zero = jnp.zeros_like(q)
    kc = kc_ref[0].astype(BF16)
    kn = kn_ref[...].astype(BF16)
    nt = (((1,), (1,)), ((), ()))
    probs = []
    for half in range(2):
        qh = jnp.where((lane >= A_HD) == bool(half), q, zero)
        sc = lax.dot_general(qh, kc, nt, preferred_element_type=F32)
        sn = lax.dot_general(qh, kn, nt, preferred_element_type=F32)
        m = jnp.maximum(jnp.max(sc, axis=-1, keepdims=True), jnp.max(sn, axis=-1, keepdims=True))
        pc = jnp.exp2(sc - m)
        pn = jnp.exp2(sn - m)
        inv = 1.0 / (jnp.sum(pc, axis=-1, keepdims=True) + jnp.sum(pn, axis=-1, keepdims=True))
        probs.append((pc * inv, pn * inv))
    lam = _lambda(lamp_ref, lam_init)
    ac = (probs[0][0] - lam * probs[1][0]).astype(BF16)
    an = (probs[0][1] - lam * probs[1][1]).astype(BF16)
    o = (jnp.dot(ac, vc_ref[0].astype(BF16), preferred_element_type=F32)
         + jnp.dot(an, vn_ref[...].astype(BF16), preferred_element_type=F32))
    ms = jnp.mean(o * o, axis=-1, keepdims=True)
    g = g_ref[...].astype(F32)
    o = o * lax.rsqrt(ms + EPS) * (sg_ref[...] * (1.0 - lam_init))
    o_ref[...] = (o * (g * _sigmoid(g))).astype(o_ref.dtype)


def _attn_sample(q, k, v, g, kc, vc, layer, lamp, sub_gain, lam_init, *, name):
    _, bsz, past, w = kc.shape
    heads = w // A_VD
    ln = q.shape[0] // bsz
    assert past % CHUNK == 0 and ln <= CHUNK
    row = lambda b, h: (b, h)
    cache = lambda b, h: (layer, b, 0, h)
    return pl.pallas_call(
        functools.partial(_attn_sample_kernel, lam_init=lam_init),
        out_shape=jax.ShapeDtypeStruct(q.shape, BF16),
        grid=(bsz, heads),
        in_specs=[pl.BlockSpec((ln, A_VD), row),
                  pl.BlockSpec((ln, A_VD), row),
                  pl.BlockSpec((ln, A_VD), row),
                  pl.BlockSpec((ln, A_VD), row),
                  pl.BlockSpec((None, 1, past, A_VD), cache),
                  pl.BlockSpec((None, 1, past, A_VD), cache),
                  pl.BlockSpec((4, A_HD), lambda b, h: (0, 0)),
                  pl.BlockSpec((1, A_VD), lambda b, h: (0, 0))],
        out_specs=pl.BlockSpec((ln, A_VD), row),
        compiler_params=_params(("parallel", "parallel")),
        name=name,
    )(q, k, v, g, kc, vc, lamp, sub_gain.reshape(1, A_VD))


RET_HEADS_PER_STEP = 2


def _retention_kernel(q_ref, k_ref, v_ref, g_ref, s0_ref, dec_ref, cd_ref, kd_ref, gl_ref,
                      o_ref, sout_ref, st_ref, *, lc, hps):
    c = pl.program_id(2)

    @pl.when(c == 0)
    def _():
        st_ref[...] = s0_ref[0]

    hd = range(hps)
    qs = [q_ref[:, h * R_DK:(h + 1) * R_DK] for h in hd]
    ks = [k_ref[:, h * R_DK:(h + 1) * R_DK] for h in hd]
    vs = [v_ref[:, h * R_DV:(h + 1) * R_DV] for h in hd]
    sts = [st_ref[h] for h in hd]
    nt = (((1,), (1,)), ((), ()))
    inner = [lax.dot_general(qs[h], ks[h], nt, preferred_element_type=F32) for h in hd]
    cross = [jnp.dot(qs[h], sts[h].astype(BF16), preferred_element_type=F32) for h in hd]
    kdec = []
    for h in hd:
        kd = ks[h].astype(F32) * jnp.concatenate([kd_ref[h]] * (R_DK // LANES), axis=1)
        if lc < LANES:
            kd = jnp.concatenate([kd, jnp.zeros((LANES - lc, R_DK), F32)], axis=0)
        kdec.append(kd.T.astype(BF16))
    inner = [(inner[h] * dec_ref[h]).astype(BF16) for h in hd]
    o = [jnp.dot(inner[h], vs[h], preferred_element_type=F32) for h in hd]
    upd = []
    for h in hd:
        vv = vs[h]
        if lc < LANES:
            vv = jnp.concatenate([vv, jnp.zeros((LANES - lc, R_DV), BF16)], axis=0)
        upd.append(jnp.dot(kdec[h], vv, preferred_element_type=F32))
    for h in hd:
        cd = jnp.concatenate([cd_ref[h]] * (R_DV // LANES), axis=1)
        oh = o[h] + cross[h] * cd
        ms = jnp.mean(oh * oh, axis=-1, keepdims=True)
        g = g_ref[:, h * R_DV:(h + 1) * R_DV].astype(F32)
        o_ref[:, h * R_DV:(h + 1) * R_DV] = (oh * lax.rsqrt(ms + EPS) * (g * _sigmoid(g))).astype(o_ref.dtype)
    st_new = []
    for h in hd:
        gl = jnp.concatenate([gl_ref[h]] * (R_DV // LANES), axis=1)
        st_new.append(sts[h] * gl + upd[h])
        st_ref[h] = st_new[h]

    @pl.when(c == pl.num_programs(2) - 1)
    def _():
        for h in hd:
            sout_ref[0, h] = st_new[h]


def _retention(q, k, vg, s0, lc, *, name):
    bsz, heads = s0.shape[:2]
    t = q.shape[0] // bsz
    nc = t // lc
    lg = jnp.log1p(-(2.0 ** (-5.0 - jnp.arange(heads, dtype=F32))))
    idx = jnp.arange(lc, dtype=F32)
    diff = idx[:, None] - idx[None, :]
    dec = jnp.where(diff >= 0, jnp.exp(lg[:, None, None] * jnp.maximum(diff, 0.0)), 0.0)
    cd = jnp.broadcast_to(jnp.exp(lg[:, None] * (idx[None, :] + 1.0))[:, :, None], (heads, lc, LANES))
    kd = jnp.broadcast_to(jnp.exp(lg[:, None] * (lc - 1.0 - idx[None, :]))[:, :, None], (heads, lc, LANES))
    gl = jnp.broadcast_to(jnp.exp(lg * lc)[:, None, None], (heads, 1, LANES))
    hps = RET_HEADS_PER_STEP
    assert heads % hps == 0
    rows = lambda b, h, c: (b * nc + c, h)
    gate = lambda b, h, c: (b * nc + c, heads // hps + h)
    tab = lambda b, h, c: (h, 0, 0)
    state = lambda b, h, c: (b, h, 0, 0)
    return pl.pallas_call(
        functools.partial(_retention_kernel, lc=lc, hps=hps),
        out_shape=(jax.ShapeDtypeStruct((vg.shape[0], heads * R_DV), BF16),
                   jax.ShapeDtypeStruct(s0.shape, F32)),
        grid=(bsz, heads // hps, nc),
        in_specs=[pl.BlockSpec((lc, hps * R_DK), rows),
                  pl.BlockSpec((lc, hps * R_DK), rows),
                  pl.BlockSpec((lc, hps * R_DV), rows),
                  pl.BlockSpec((lc, hps * R_DV), gate),
                  pl.BlockSpec((1, hps, R_DK, R_DV), state),
                  pl.BlockSpec((hps, lc, lc), tab),
                  pl.BlockSpec((hps, lc, LANES), tab),
                  pl.BlockSpec((hps, lc, LANES), tab),
                  pl.BlockSpec((hps, 1, LANES), tab)],
        out_specs=(pl.BlockSpec((lc, hps * R_DV), rows),
                   pl.BlockSpec((1, hps, R_DK, R_DV), state)),
        scratch_shapes=[pltpu.VMEM((hps, R_DK, R_DV), F32)],
        compiler_params=_params(("parallel", "parallel", "arbitrary")),
        name=name,
    )(q, k, vg, vg, s0, dec, cd, kd, gl)


def _cmlp_kernel(u_ref, g_ref, v_ref, vg_ref, w_ref, b_ref, *out_refs, emit_v):
    if emit_v:
        a_ref, vn_ref = out_refs
    else:
        (a_ref,) = out_refs
    va = _gelu(v_ref[...])
    ms = jnp.mean(va * va, axis=-1, keepdims=True)
    vn = va * lax.rsqrt(ms + EPS) * vg_ref[...]
    if emit_v:
        vn_ref[...] = vn
    vb = vn.astype(BF16)
    gd = vn.shape[1] // M_GROUPS
    for grp in range(M_GROUPS):
        sl = slice(grp * gd, (grp + 1) * gd)
        bias = jnp.concatenate([b_ref[grp]] * (gd // LANES), axis=1)
        mix = jnp.dot(w_ref[grp], vb[:, sl], preferred_element_type=F32) + bias
        g = g_ref[:, sl].astype(F32)
        a_ref[:, sl] = (_gelu(u_ref[:, sl].astype(F32)) * mix * (g * _sigmoid(g))).astype(a_ref.dtype)


def _cmlp(ug, v, v_gain, wmix, bmix, *, emit_v, name):
    m, w = v.shape
    t = wmix.shape[1]
    out_shape = [jax.ShapeDtypeStruct((m, w), BF16)]
    out_specs = [pl.BlockSpec((t, w), lambda i: (i, 0))]
    if emit_v:
        out_shape.append(jax.ShapeDtypeStruct((m, w), F32))
        out_specs.append(pl.BlockSpec((t, w), lambda i: (i, 0)))
    res = pl.pallas_call(
        functools.partial(_cmlp_kernel, emit_v=emit_v),
        out_shape=tuple(out_shape),
        grid=(m // t,),
        in_specs=[pl.BlockSpec((t, w), lambda i: (i, 0)),
                  pl.BlockSpec((t, w), lambda i: (i, 1)),
                  pl.BlockSpec((t, w), lambda i: (i, 0)),
                  pl.BlockSpec((1, w), lambda i: (0, 0)),
                  pl.BlockSpec((M_GROUPS, t, t), lambda i: (0, 0, 0)),
                  pl.BlockSpec((M_GROUPS, t, LANES), lambda i: (0, 0, 0))],
        out_specs=tuple(out_specs),
        compiler_params=_params(("parallel",)),
        name=name,
    )(ug, ug, v, v_gain.reshape(1, w), wmix, bmix)
    return res if emit_v else res[0]


def _rope_tables(pos, d, signed):
    inv = ROPE_THETA ** (-jnp.arange(0, d, 2, dtype=F32) / d)
    ang = pos.astype(F32)[:, None] * inv[None, :]
    cos, sin = jnp.cos(ang), jnp.sin(ang)
    reps = LANES // (d // 2)
    if signed:
        return (jnp.tile(cos, (1, reps)),
                jnp.tile(jnp.concatenate([-sin, sin], axis=1), (1, reps // 2)))
    return jnp.tile(cos, (1, reps)), jnp.tile(sin, (1, reps))


def _diff_attn_layer(h, pos, nw, w_in, w_out, q_gain, k_gain, lamp, sub_gain, lam_init,
                     layers, layer, kbuf, vbuf, cache, tag):
    w = w_out.shape[0]
    cos, sin = _rope_tables(pos, A_HD, signed=True)
    wq, wk, wv, wg = (w_in[:, i * w:(i + 1) * w].astype(BF16) for i in range(4))
    q = _proj(h, nw, wq, BF16, epi="qk", cos=cos, sin=sin, gain=q_gain,
              out_scale=(A_HD ** -0.5) * LOG2E, name=f"{tag}_proj_q")
    k, kb = _proj(h, nw, wk, (F32, BF16), epi="qk", cos=cos, sin=sin, gain=k_gain,
                  stack=(layers, layer, kbuf), name=f"{tag}_proj_k")
    v, vb = _proj(h, nw, wv, (F32, BF16), stack=(layers, layer, vbuf), name=f"{tag}_proj_v")
    g = _proj(h, nw, wg, BF16, name=f"{tag}_proj_g")
    if cache is None:
        o = _attn_prompt(q, kb, vb, g, lamp, sub_gain, lam_init, name=f"{tag}_attn")
    else:
        kc, vc = cache
        o = _attn_sample(q, kb, vb, g, kc, vc, layer, lamp, sub_gain, lam_init, name=f"{tag}_attn")
    return _outproj(o, w_out.astype(BF16), h, name=f"{tag}_out"), k, v


def _retention_layer(h, pos, nw, w_in, w_out, s0, lc, tag):
    heads = s0.shape[1]
    qk_w, v_w = heads * R_DK, heads * R_DV
    cos, sin = _rope_tables(pos, R_DK, signed=False)
    wq = w_in[:, :qk_w].astype(BF16)
    wk = w_in[:, qk_w:2 * qk_w].astype(BF16)
    assert w_in.shape[1] == 2 * qk_w + 2 * v_w
    wvg = w_in[:, 2 * qk_w:].astype(BF16)
    q = _proj(h, nw, wq, BF16, epi="rope256", cos=cos, sin=sin, name=f"{tag}_proj_q")
    k = _proj(h, nw, wk, BF16, epi="rope256", cos=cos, sin=sin, out_scale=R_DK ** -0.5,
              name=f"{tag}_proj_k")
    vg = _proj(h, nw, wvg, BF16, name=f"{tag}_proj_vg")
    o, s_new = _retention(q, k, vg, s0, lc, name=f"{tag}_ret")
    return _outproj(o, w_out.astype(BF16), h, name=f"{tag}_out"), s_new


def _cmlp_layer(h, nw, w_in, w_out, v_gain, wmix, bmix, emit_v, tag):
    w = w_out.shape[0]
    wu, wv, wg = (w_in[:, i * w:(i + 1) * w].astype(BF16) for i in range(3))
    ug = _proj(h, nw, jnp.concatenate([wu, wg], axis=1), BF16, name=f"{tag}_proj_ug")
    v = _proj(h, nw, wv, F32, name=f"{tag}_proj_v")
    res = _cmlp(ug, v, v_gain, wmix, bmix, emit_v=emit_v, name=f"{tag}_mix")
    a, vn = res if emit_v else (res, None)
    return _outproj(a, w_out.astype(BF16), h, name=f"{tag}_out"), vn


def _mix_tables(w_s, b_s, chunk_len):
    groups = w_s.shape[0]
    wl = jnp.tril(w_s[:, :chunk_len, :chunk_len])
    reps = M_CHUNK // chunk_len
    eye = jnp.eye(reps, dtype=F32)
    wt = jnp.einsum("ab,gij->gaibj", eye, wl).reshape(groups, M_CHUNK, M_CHUNK)
    bt = jnp.tile(b_s[:, :chunk_len], (1, reps))
    return wt.astype(BF16), jnp.broadcast_to(bt[:, :, None], (groups, M_CHUNK, LANES))


def kernel(x_prompt, x_sample, cache_k_attn, cache_v_attn, state_ret, norm_w, a_w_in, a_w_out, a_q_gain, a_k_gain, a_lam_q1, a_lam_k1, a_lam_q2, a_lam_k2, a_sub_gain, r_w_in, r_w_out, c_w_in, c_w_out, c_v_gain, c_w_s, c_b_s):
    batch, s_len, d = x_prompt.shape
    dec_b, dec_len, _ = x_sample.shape
    past = cache_k_attn.shape[2]
    depth = norm_w.shape[0]
    assert batch == 1 and M_CHUNK % dec_len == 0 and s_len % M_CHUNK == 0

    hp = x_prompt.reshape(s_len, d)
    hs = x_sample.reshape(dec_b * dec_len, d)
    pos_p = jnp.arange(s_len, dtype=jnp.int32)
    pos_s = jnp.tile(past + jnp.arange(dec_len, dtype=jnp.int32), dec_b)

    n_a = a_w_in.shape[0]
    aw = a_w_out.shape[1]
    cache = (cache_k_attn.reshape(n_a, dec_b, past, aw), cache_v_attn.reshape(n_a, dec_b, past, aw))
    kp = vp = kn = vn = None
    sp_l, ss_l, vm_l = [], [], []
    for i in range(depth):
        kind, j = i % N_MIXERS, i // N_MIXERS
        if kind == 0:
            lam_init = 0.8 - 0.6 * math.exp(-0.3 * i)
            lamp = jnp.stack([a_lam_q1[j], a_lam_k1[j], a_lam_q2[j], a_lam_k2[j]])
            args = (norm_w[i], a_w_in[j], a_w_out[j], a_q_gain[j], a_k_gain[j], lamp, a_sub_gain[j], lam_init)
            hp, kp, vp = _diff_attn_layer(hp, pos_p, *args, n_a, j, kp, vp, None, f"l{i}p")
            hs, kn, vn = _diff_attn_layer(hs, pos_s, *args, n_a, j, kn, vn, cache, f"l{i}s")
        elif kind == 1:
            heads = state_ret.shape[2]
            s0 = jnp.zeros((batch, heads, R_DK, R_DV), F32)
            hp, st_p = _retention_layer(hp, pos_p, norm_w[i], r_w_in[j], r_w_out[j], s0,
                                        _tile(s_len, 256), f"l{i}p")
            hs, st_s = _retention_layer(hs, pos_s, norm_w[i], r_w_in[j], r_w_out[j],
                                        state_ret[j].astype(F32), dec_len, f"l{i}s")
            sp_l.append(st_p)
            ss_l.append(st_s)
        else:
            wp, bp = _mix_tables(c_w_s[j], c_b_s[j], M_CHUNK)
            ws, bs = _mix_tables(c_w_s[j], c_b_s[j], dec_len)
            hp, _ = _cmlp_layer(hp, norm_w[i], c_w_in[j], c_w_out[j], c_v_gain[j], wp, bp, False, f"l{i}p")
            hs, v_s = _cmlp_layer(hs, norm_w[i], c_w_in[j], c_w_out[j], c_v_gain[j], ws, bs, True, f"l{i}s")
            vm_l.append(v_s.reshape(dec_b, dec_len, -1))

    return (hp.reshape(batch, s_len, d), hs.reshape(dec_b, dec_len, d),
            kp.reshape(n_a, batch, s_len, aw // A_HD, A_HD), vp.reshape(n_a, batch, s_len, aw // A_VD, A_VD),
            kn.reshape(n_a, dec_b, dec_len, aw // A_HD, A_HD), vn.reshape(n_a, dec_b, dec_len, aw // A_VD, A_VD),
            jnp.stack(sp_l), jnp.stack(ss_l), jnp.stack(vm_l))
```

```python
import functools
import math

import jax
import jax.numpy as jnp
from jax import lax
from jax.experimental import pallas as pl
from jax.experimental.pallas import tpu as pltpu

F32 = jnp.float32
BF16 = jnp.bfloat16

EPS = 1e-6
CHUNK = 64
ROPE_THETA = 10000.0
N_MIXERS = 3
A_HD = 64
A_VD = 2 * A_HD
R_DK = 256
R_DV = 2 * R_DK
M_GROUPS = 8
M_CHUNK = 128
LOG2E = 1.4426950408889634

LANES = 128
MXU_COLS = 256
VMEM_LIMIT = 56 * 1024 * 1024
NEG_BIG = -1e30


def _tile(n, pref):
    if n <= pref:
        return n
    t = pref
    while t >= 8:
        if n % t == 0:
            return t
        t -= 8
    return n


def _params(sem):
    return pltpu.CompilerParams(dimension_semantics=sem, vmem_limit_bytes=VMEM_LIMIT)


def _gelu(x):
    return 0.5 * x * (1.0 + jnp.tanh(0.7978845608028654 * (x + 0.044715 * (x * x * x))))


def _sigmoid(x):
    return 1.0 / (1.0 + jnp.exp(-x))


def _proj_kernel(*refs, plan):
    xn_ref = refs[-1]
    it = iter(refs)
    x_ref, nw_ref = next(it), next(it)
    epis = [p[0] for p in plan]
    cos_ref, sin_ref = (next(it), next(it)) if any(e != "none" for e in epis) else (None, None)
    gsum_ref = next(it) if "qk" in epis else None
    w_refs, gain_refs = [], []
    for e in epis:
        w_refs.append(next(it))
        gain_refs.append(next(it) if e == "qk" else None)
    outs = list(refs[len(refs) - 1 - sum(p[2] for p in plan):-1])
    o_refs = []
    for p in plan:
        o_refs.append(outs[:p[2]])
        outs = outs[p[2]:]

    def store(seg, cols, val):
        scale = plan[seg][1]
        if scale != 1.0:
            val = val * scale
        for o_ref in o_refs[seg]:
            o_ref[:, cols] = val.astype(o_ref.dtype)

    @pl.when(pl.program_id(1) == 0)
    def _():
        x = x_ref[...]
        ms = jnp.mean(x * x, axis=-1, keepdims=True)
        xn_ref[...] = (x * lax.rsqrt(ms + EPS) * nw_ref[...]).astype(BF16)

    qk = [s for s, e in enumerate(epis) if e == "qk"]
    rest = [s for s, e in enumerate(epis) if e == "rope256"] + [s for s, e in enumerate(epis) if e == "none"]
    slabs = {s: [slice(c * MXU_COLS, (c + 1) * MXU_COLS) for c in range(w_refs[s].shape[1] // MXU_COLS)]
             for s in qk}
    zq = {(s, c): jnp.dot(xn_ref[...], w_refs[s][:, sl], preferred_element_type=F32)
          for s in qk for c, sl in enumerate(slabs[s])}
    sq = {key: jnp.dot((z * z).astype(BF16), gsum_ref[...], preferred_element_type=F32) for key, z in zq.items()}
    zr = {s: jnp.dot(xn_ref[...], w_refs[s][...], preferred_element_type=F32) for s in rest}

    if qk:
        cos = jnp.concatenate([cos_ref[...]] * 2, axis=1)
        sin = jnp.concatenate([sin_ref[...]] * 2, axis=1)
        lane = lax.broadcasted_iota(jnp.int32, (xn_ref.shape[0], MXU_COLS), 1)
        first_half = (lane % A_HD) < (A_HD // 2)
        for (s, c), z in zq.items():
            gain = jnp.concatenate([gain_refs[s][...]] * 2, axis=1)
            zn = z * lax.rsqrt(sq[(s, c)] * (1.0 / A_HD) + EPS) * gain
            partner = jnp.where(first_half,
                                pltpu.roll(zn, MXU_COLS - A_HD // 2, 1),
                                pltpu.roll(zn, A_HD // 2, 1))
            store(s, slabs[s][c], zn * cos + partner * sin)
    for s in rest:
        z = zr[s]
        if epis[s] == "rope256":
            cos = cos_ref[...]
            sin = sin_ref[...]
            for c in range(z.shape[1] // R_DK):
                lo, hi = slice(c * R_DK, c * R_DK + LANES), slice(c * R_DK + LANES, (c + 1) * R_DK)
                x1, x2 = z[:, lo], z[:, hi]
                store(s, lo, x1 * cos - x2 * sin)
                store(s, hi, x2 * cos + x1 * sin)
        else:
            store(s, slice(None), z)


def _seg(w, out_dtypes, epi="none", out_scale=1.0, gain=None, stack=None):
    return dict(w=w, out_dtypes=out_dtypes, epi=epi, out_scale=out_scale, gain=gain, stack=stack)


PROJ_VMEM_BUDGET = 48 * 1024 * 1024


def _proj(x, nw, segs, *, cos=None, sin=None, name):
    m, d = x.shape
    tm = _tile(m, 1024)
    epis = [s["epi"] for s in segs]
    assert not ("qk" in epis and "rope256" in epis)
    unit = {"qk": MXU_COLS, "rope256": R_DK, "none": LANES}

    def vmem_bytes(nj):
        total = 2 * tm * d * 4 + tm * d * 2
        for s in segs:
            tn = s["w"].shape[1] // nj
            total += 2 * d * tn * 2 + tm * tn * 4
            total += sum(2 * tm * tn * jnp.dtype(dt).itemsize for dt in s["out_dtypes"])
        return total

    nj = next(c for c in (1, 2, 4, 8, 16, 32)
              if all(s["w"].shape[1] % (c * unit[s["epi"]]) == 0 for s in segs) and vmem_bytes(c) <= PROJ_VMEM_BUDGET)
    in_specs = [pl.BlockSpec((tm, d), lambda i, j: (i, 0)),
                pl.BlockSpec((1, d), lambda i, j: (0, 0))]
    args = [x, nw.reshape(1, d)]
    if any(e != "none" for e in epis):
        in_specs += [pl.BlockSpec((tm, LANES), lambda i, j: (i, 0))] * 2
        args += [cos, sin]
    if "qk" in epis:
        gidx = jnp.arange(MXU_COLS) // A_HD
        in_specs.append(pl.BlockSpec((MXU_COLS, MXU_COLS), lambda i, j: (0, 0)))
        args.append((gidx[:, None] == gidx[None, :]).astype(BF16))
    out_shape, out_specs, bufs = [], [], []
    for s in segs:
        n = s["w"].shape[1]
        tn = n // nj
        in_specs.append(pl.BlockSpec((d, tn), lambda i, j: (0, j)))
        args.append(s["w"])
        if s["epi"] == "qk":
            in_specs.append(pl.BlockSpec((1, LANES), lambda i, j: (0, 0)))
            args.append(jnp.tile(s["gain"].reshape(1, A_HD), (1, LANES // A_HD)))
        for o, dt in enumerate(s["out_dtypes"]):
            if o == 0 and s["stack"] is not None:
                layers, layer, buf = s["stack"]
                out_specs.append(pl.BlockSpec((None, tm, tn), lambda i, j, layer=layer: (layer, i, j)))
                out_shape.append(jax.ShapeDtypeStruct((layers, m, n), dt))
                if buf is not None:
                    bufs.append((len(out_shape) - 1, buf))
            else:
                out_specs.append(pl.BlockSpec((tm, tn), lambda i, j: (i, j)))
                out_shape.append(jax.ShapeDtypeStruct((m, n), dt))
    aliases = {}
    for out_idx, buf in bufs:
        aliases[len(args)] = out_idx
        in_specs.append(pl.BlockSpec(memory_space=pl.ANY))
        args.append(buf)
    res = list(pl.pallas_call(
        functools.partial(_proj_kernel, plan=tuple((s["epi"], s["out_scale"], len(s["out_dtypes"])) for s in segs)),
        out_shape=tuple(out_shape),
        grid=(m // tm, nj),
        in_specs=in_specs,
        out_specs=tuple(out_specs),
        scratch_shapes=[pltpu.VMEM((tm, d), BF16)],
        input_output_aliases=aliases,
        compiler_params=_params(("parallel", "arbitrary")),
        name=name,
    )(*args))
    out = []
    for s in segs:
        out.append(res[:len(s["out_dtypes"])])
        res = res[len(s["out_dtypes"]):]
    return out


def _outproj_kernel(a_ref, w_ref, h_ref, o_ref):
    o_ref[...] = h_ref[...] + jnp.dot(a_ref[...], w_ref[...], preferred_element_type=F32)


def _outproj(a, w, h, *, name):
    m, k = a.shape
    n = w.shape[1]
    tm = _tile(m, 1024)
    tn = _tile(n, 1024)
    return pl.pallas_call(
        _outproj_kernel,
        out_shape=jax.ShapeDtypeStruct((m, n), F32),
        grid=(m // tm, n // tn),
        in_specs=[pl.BlockSpec((tm, k), lambda i, j: (i, 0)),
                  pl.BlockSpec((k, tn), lambda i, j: (0, j)),
                  pl.BlockSpec((tm, tn), lambda i, j: (i, j))],
        out_specs=pl.BlockSpec((tm, tn), lambda i, j: (i, j)),
        compiler_params=_params(("parallel", "parallel")),
        name=name,
    )(a, w, h)


def _lambda(lamp_ref, lam_init):
    lp = lamp_ref[...]
    s1 = jnp.sum(lp[0:1, :] * lp[1:2, :], axis=-1, keepdims=True)
    s2 = jnp.sum(lp[2:3, :] * lp[3:4, :], axis=-1, keepdims=True)
    return jnp.exp(s1) - jnp.exp(s2) + lam_init


V_AUG = A_VD + 16


LOOKAHEAD = 5


def _attn_prompt_kernel(q_ref, k_ref, v_ref, g_ref, lamp_ref, sg_ref, o_ref,
                        qpad_ref, vt_ref, m_ref, acc_ref, s_ref, mx_ref, *, tq, tk, lam_init):
    i = pl.program_id(1)

    @pl.when(i == 0)
    def _():
        aug = lax.broadcasted_iota(jnp.int32, (V_AUG - A_VD, tk), 0)
        ones_row = jnp.where(aug == 0, 1.0, 0.0).astype(BF16)

        def fill(t, carry):
            vj = v_ref[pl.ds(pl.multiple_of(t * tk, tk), tk), :].astype(F32)
            vt_ref[t, :A_VD, :] = vj.T.astype(BF16)
            vt_ref[t, A_VD:, :] = ones_row
            return carry

        lax.fori_loop(0, vt_ref.shape[0], fill, 0)

    qt = q_ref[...].astype(F32).T
    row = lax.broadcasted_iota(jnp.int32, qt.shape, 0)
    qpad_ref[:, :tq] = jnp.where(row < A_HD, qt, 0.0).astype(BF16)
    qpad_ref[:, tq:] = jnp.where(row >= A_HD, qt, 0.0).astype(BF16)
    m_ref[...] = jnp.full(m_ref.shape, NEG_BIG, F32)
    acc_ref[...] = jnp.zeros(acc_ref.shape, F32)

    sw = s_ref.shape[2]
    n_strips = 2 * tq // sw
    la = min(LOOKAHEAD, n_strips - 1)

    def scores(j, c):
        kj = k_ref[pl.ds(pl.multiple_of(j * tk, tk), tk), :]
        s = jnp.dot(kj, qpad_ref[:, c * sw:(c + 1) * sw], preferred_element_type=F32)
        s_ref[c] = s
        mx_ref[c] = jnp.max(s, axis=0, keepdims=True)

    def update(j, c, mask):
        cs = slice(c * sw, (c + 1) * sw)
        s = s_ref[c]
        if mask is None:
            mx = mx_ref[c]
        else:
            s = jnp.where(mask, s, NEG_BIG)
            mx = jnp.max(s, axis=0, keepdims=True)
        m_old = m_ref[:, cs]
        m_new = jnp.maximum(m_old, mx)
        alpha = jnp.exp2(m_old - m_new)
        p = jnp.exp2(s - m_new).astype(BF16)
        pv = jnp.dot(vt_ref[j], p, preferred_element_type=F32)
        acc_ref[:, cs] = alpha * acc_ref[:, cs] + pv
        m_ref[:, cs] = m_new

    n_sub = tq // tk
    n_full = i * n_sub
    order = sorted(range(n_strips), key=lambda c: -((c * sw) % tq))

    for c in order[:la]:
        scores(0, c)

    def body(jj, carry):
        for t in range(n_sub):
            j = jj * n_sub + t
            for n, c in enumerate(order):
                update(j, c, None)
                if n + la < n_strips:
                    scores(j, order[n + la])
                else:
                    scores(j + 1, order[n + la - n_strips])
        return carry

    lax.fori_loop(0, i, body, 0)

    r = lax.broadcasted_iota(jnp.int32, (tk, sw), 0)
    cc = lax.broadcasted_iota(jnp.int32, (tk, sw), 1)
    units = []
    for d in range(n_sub):
        for c in order:
            q_lo, k_lo = (c * sw) % tq, d * tk
            if q_lo + sw <= k_lo:
                continue
            full = q_lo >= k_lo + tk
            units.append((d, c, None if full else (k_lo + r) // CHUNK <= (q_lo + cc) // CHUNK))
    assert [u[:2] for u in units[:la]] == [(0, c) for c in order[:la]]
    for n, (d, c, mask) in enumerate(units):
        update(n_full + d, c, mask)
        if n + la < len(units):
            nd, nc, _ = units[n + la]
            assert all(u[1] != nc for u in units[n + 1:n + la])
            scores(n_full + nd, nc)

    lam = _lambda(lamp_ref, lam_init)
    acc = acc_ref[...]
    inv_l = 1.0 / acc[A_VD:A_VD + 1, :]
    ot = acc[:A_VD, :tq] * inv_l[:, :tq] - lam * (acc[:A_VD, tq:] * inv_l[:, tq:])
    ms = jnp.mean(ot * ot, axis=0, keepdims=True)
    ot = ot * lax.rsqrt(ms + EPS) * (sg_ref[...] * (1.0 - lam_init))
    g = g_ref[...].astype(F32)
    o_ref[...] = (ot.T * (g * _sigmoid(g))).astype(o_ref.dtype)


def _attn_prompt(q, k, v, g, lamp, sub_gain, lam_init, *, name):
    s, w = q.shape
    heads = w // A_VD
    tq = _tile(s, 1024)
    tk = _tile(tq, 512)
    sw = min(MXU_COLS, tq)
    return pl.pallas_call(
        functools.partial(_attn_prompt_kernel, tq=tq, tk=tk, lam_init=lam_init),
        out_shape=jax.ShapeDtypeStruct((s, w), BF16),
        grid=(heads, s // tq),
        in_specs=[pl.BlockSpec((tq, A_VD), lambda h, i: (i, h)),
                  pl.BlockSpec((s, A_VD), lambda h, i: (0, h)),
                  pl.BlockSpec((s, A_VD), lambda h, i: (0, h)),
                  pl.BlockSpec((tq, A_VD), lambda h, i: (i, h)),
                  pl.BlockSpec((4, A_HD), lambda h, i: (0, 0)),
                  pl.BlockSpec((A_VD, 1), lambda h, i: (0, 0))],
        out_specs=pl.BlockSpec((tq, A_VD), lambda h, i: (i, h)),
        scratch_shapes=[pltpu.VMEM((A_VD, 2 * tq), BF16),
                        pltpu.VMEM((s // tk, V_AUG, tk), BF16),
                        pltpu.VMEM((1, 2 * tq), F32),
                        pltpu.VMEM((V_AUG, 2 * tq), F32),
                        pltpu.VMEM((2 * tq // sw, tk, sw), F32),
                        pltpu.VMEM((2 * tq // sw, 1, sw), F32)],
        compiler_params=_params(("parallel", "arbitrary")),
        name=name,
    )(q, k, v, g, lamp, sub_gain.reshape(A_VD, 1))


def _attn_sample_kernel(q_ref, kn_ref, vn_ref, g_ref, kc_ref, vc_ref, lamp_ref, sg_ref, o_ref, *, lam_init):
    q = q_ref[...]
    lane = lax.broadcasted_iota(jnp.int32, q.shape, 1)
    zero = jnp.zeros_like(q)
    kc = kc_ref[0].astype(BF16)
    kn = kn_ref[...].astype(BF16)
    nt = (((1,), (1,)), ((), ()))
    probs = []
    for half in range(2):
        qh = jnp.where((lane >= A_HD) == bool(half), q, zero)
        sc = lax.dot_general(qh, kc, nt, preferred_element_type=F32)
        sn = lax.dot_general(qh, kn, nt, preferred_element_type=F32)
        m = jnp.maximum(jnp.max(sc, axis=-1, keepdims=True), jnp.max(sn, axis=-1, keepdims=True))
        pc = jnp.exp2(sc - m)
        pn = jnp.exp2(sn - m)
        inv = 1.0 / (jnp.sum(pc, axis=-1, keepdims=True) + jnp.sum(pn, axis=-1, keepdims=True))
        probs.append((pc * inv, pn * inv))
    lam = _lambda(lamp_ref, lam_init)
    ac = (probs[0][0] - lam * probs[1][0]).astype(BF16)
    an = (probs[0][1] - lam * probs[1][1]).astype(BF16)
    o = (jnp.dot(ac, vc_ref[0].astype(BF16), preferred_element_type=F32)
         + jnp.dot(an, vn_ref[...].astype(BF16), preferred_element_type=F32))
    ms = jnp.mean(o * o, axis=-1, keepdims=True)
    g = g_ref[...].astype(F32)
    o = o * lax.rsqrt(ms + EPS) * (sg_ref[...] * (1.0 - lam_init))
    o_ref[...] = (o * (g * _sigmoid(g))).astype(o_ref.dtype)


def _attn_sample(q, k, v, g, kc, vc, layer, lamp, sub_gain, lam_init, *, name):
    _, bsz, past, w = kc.shape
    heads = w // A_VD
    ln = q.shape[0] // bsz
    assert past % CHUNK == 0 and ln <= CHUNK
    row = lambda b, h: (b, h)
    cache = lambda b, h: (layer, b, 0, h)
    return pl.pallas_call(
        functools.partial(_attn_sample_kernel, lam_init=lam_init),
        out_shape=jax.ShapeDtypeStruct(q.shape, BF16),
        grid=(bsz, heads),
        in_specs=[pl.BlockSpec((ln, A_VD), row),
                  pl.BlockSpec((ln, A_VD), row),
                  pl.BlockSpec((ln, A_VD), row),
                  pl.BlockSpec((ln, A_VD), row),
                  pl.BlockSpec((None, 1, past, A_VD), cache),
                  pl.BlockSpec((None, 1, past, A_VD), cache),
                  pl.BlockSpec((4, A_HD), lambda b, h: (0, 0)),
                  pl.BlockSpec((1, A_VD), lambda b, h: (0, 0))],
        out_specs=pl.BlockSpec((ln, A_VD), row),
        compiler_params=_params(("parallel", "parallel")),
        name=name,
    )(q, k, v, g, kc, vc, lamp, sub_gain.reshape(1, A_VD))


RET_HEADS_PER_STEP = 2


def _retention_kernel(q_ref, k_ref, v_ref, g_ref, s0_ref, dec_ref, cd_ref, kd_ref, gl_ref,
                      o_ref, sout_ref, st_ref, *, lc, hps):
    c = pl.program_id(2)

    @pl.when(c == 0)
    def _():
        st_ref[...] = s0_ref[0]

    hd = range(hps)
    qs = [q_ref[:, h * R_DK:(h + 1) * R_DK] for h in hd]
    ks = [k_ref[:, h * R_DK:(h + 1) * R_DK] for h in hd]
    vs = [v_ref[:, h * R_DV:(h + 1) * R_DV] for h in hd]
    sts = [st_ref[h] for h in hd]
    nt = (((1,), (1,)), ((), ()))
    inner = [lax.dot_general(qs[h], ks[h], nt, preferred_element_type=F32) for h in hd]
    cross = [jnp.dot(qs[h], sts[h].astype(BF16), preferred_element_type=F32) for h in hd]
    kdec = []
    for h in hd:
        kd = ks[h].astype(F32) * jnp.concatenate([kd_ref[h]] * (R_DK // LANES), axis=1)
        if lc < LANES:
            kd = jnp.concatenate([kd, jnp.zeros((LANES - lc, R_DK), F32)], axis=0)
        kdec.append(kd.T.astype(BF16))
    inner = [(inner[h] * dec_ref[h]).astype(BF16) for h in hd]
    o = [jnp.dot(inner[h], vs[h], preferred_element_type=F32) for h in hd]
    upd = []
    for h in hd:
        vv = vs[h]
        if lc < LANES:
            vv = jnp.concatenate([vv, jnp.zeros((LANES - lc, R_DV), BF16)], axis=0)
        upd.append(jnp.dot(kdec[h], vv, preferred_element_type=F32))
    for h in hd:
        cd = jnp.concatenate([cd_ref[h]] * (R_DV // LANES), axis=1)
        oh = o[h] + cross[h] * cd
        ms = jnp.mean(oh * oh, axis=-1, keepdims=True)
        g = g_ref[:, h * R_DV:(h + 1) * R_DV].astype(F32)
        o_ref[:, h * R_DV:(h + 1) * R_DV] = (oh * lax.rsqrt(ms + EPS) * (g * _sigmoid(g))).astype(o_ref.dtype)
    st_new = []
    for h in hd:
        gl = jnp.concatenate([gl_ref[h]] * (R_DV // LANES), axis=1)
        st_new.append(sts[h] * gl + upd[h])
        st_ref[h] = st_new[h]

    @pl.when(c == pl.num_programs(2) - 1)
    def _():
        for h in hd:
            sout_ref[0, h] = st_new[h]


def _retention(q, k, vg, s0, lc, *, name):
    bsz, heads = s0.shape[:2]
    t = q.shape[0] // bsz
    nc = t // lc
    lg = jnp.log1p(-(2.0 ** (-5.0 - jnp.arange(heads, dtype=F32))))
    idx = jnp.arange(lc, dtype=F32)
    diff = idx[:, None] - idx[None, :]
    dec = jnp.where(diff >= 0, jnp.exp(lg[:, None, None] * jnp.maximum(diff, 0.0)), 0.0)
    cd = jnp.broadcast_to(jnp.exp(lg[:, None] * (idx[None, :] + 1.0))[:, :, None], (heads, lc, LANES))
    kd = jnp.broadcast_to(jnp.exp(lg[:, None] * (lc - 1.0 - idx[None, :]))[:, :, None], (heads, lc, LANES))
    gl = jnp.broadcast_to(jnp.exp(lg * lc)[:, None, None], (heads, 1, LANES))
    hps = RET_HEADS_PER_STEP
    assert heads % hps == 0
    rows = lambda b, h, c: (b * nc + c, h)
    gate = lambda b, h, c: (b * nc + c, heads // hps + h)
    tab = lambda b, h, c: (h, 0, 0)
    state = lambda b, h, c: (b, h, 0, 0)
    return pl.pallas_call(
        functools.partial(_retention_kernel, lc=lc, hps=hps),
        out_shape=(jax.ShapeDtypeStruct((vg.shape[0], heads * R_DV), BF16),
                   jax.ShapeDtypeStruct(s0.shape, F32)),
        grid=(bsz, heads // hps, nc),
        in_specs=[pl.BlockSpec((lc, hps * R_DK), rows),
                  pl.BlockSpec((lc, hps * R_DK), rows),
                  pl.BlockSpec((lc, hps * R_DV), rows),
                  pl.BlockSpec((lc, hps * R_DV), gate),
                  pl.BlockSpec((1, hps, R_DK, R_DV), state),
                  pl.BlockSpec((hps, lc, lc), tab),
                  pl.BlockSpec((hps, lc, LANES), tab),
                  pl.BlockSpec((hps, lc, LANES), tab),
                  pl.BlockSpec((hps, 1, LANES), tab)],
        out_specs=(pl.BlockSpec((lc, hps * R_DV), rows),
                   pl.BlockSpec((1, hps, R_DK, R_DV), state)),
        scratch_shapes=[pltpu.VMEM((hps, R_DK, R_DV), F32)],
        compiler_params=_params(("parallel", "parallel", "arbitrary")),
        name=name,
    )(q, k, vg, vg, s0, dec, cd, kd, gl)


def _cmlp_kernel(u_ref, g_ref, v_ref, vg_ref, w_ref, b_ref, *out_refs, emit_v):
    if emit_v:
        a_ref, vn_ref = out_refs
    else:
        (a_ref,) = out_refs
    va = _gelu(v_ref[...])
    ms = jnp.mean(va * va, axis=-1, keepdims=True)
    vn = va * lax.rsqrt(ms + EPS) * vg_ref[...]
    if emit_v:
        vn_ref[...] = vn
    vb = vn.astype(BF16)
    gd = vn.shape[1] // M_GROUPS
    for grp in range(M_GROUPS):
        sl = slice(grp * gd, (grp + 1) * gd)
        bias = jnp.concatenate([b_ref[grp]] * (gd // LANES), axis=1)
        mix = jnp.dot(w_ref[grp], vb[:, sl], preferred_element_type=F32) + bias
        g = g_ref[:, sl].astype(F32)
        a_ref[:, sl] = (_gelu(u_ref[:, sl].astype(F32)) * mix * (g * _sigmoid(g))).astype(a_ref.dtype)


def _cmlp(ug, v, v_gain, wmix, bmix, *, emit_v, name):
    m, w = v.shape
    t = wmix.shape[1]
    out_shape = [jax.ShapeDtypeStruct((m, w), BF16)]
    out_specs = [pl.BlockSpec((t, w), lambda i: (i, 0))]
    if emit_v:
        out_shape.append(jax.ShapeDtypeStruct((m, w), F32))
        out_specs.append(pl.BlockSpec((t, w), lambda i: (i, 0)))
    res = pl.pallas_call(
        functools.partial(_cmlp_kernel, emit_v=emit_v),
        out_shape=tuple(out_shape),
        grid=(m // t,),
        in_specs=[pl.BlockSpec((t, w), lambda i: (i, 0)),
                  pl.BlockSpec((t, w), lambda i: (i, 1)),
                  pl.BlockSpec((t, w), lambda i: (i, 0)),
                  pl.BlockSpec((1, w), lambda i: (0, 0)),
                  pl.BlockSpec((M_GROUPS, t, t), lambda i: (0, 0, 0)),
                  pl.BlockSpec((M_GROUPS, t, LANES), lambda i: (0, 0, 0))],
        out_specs=tuple(out_specs),
        compiler_params=_params(("parallel",)),
        name=name,
    )(ug, ug, v, v_gain.reshape(1, w), wmix, bmix)
    return res if emit_v else res[0]


def _rope_tables(pos, d, signed):
    inv = ROPE_THETA ** (-jnp.arange(0, d, 2, dtype=F32) / d)
    ang = pos.astype(F32)[:, None] * inv[None, :]
    cos, sin = jnp.cos(ang), jnp.sin(ang)
    reps = LANES // (d // 2)
    if signed:
        return (jnp.tile(cos, (1, reps)),
                jnp.tile(jnp.concatenate([-sin, sin], axis=1), (1, reps // 2)))
    return jnp.tile(cos, (1, reps)), jnp.tile(sin, (1, reps))


def _diff_attn_layer(h, pos, nw, w_in, w_out, q_gain, k_gain, lamp, sub_gain, lam_init,
                     layers, layer, kbuf, vbuf, cache, tag):
    w = w_out.shape[0]
    cos, sin = _rope_tables(pos, A_HD, signed=True)
    wq, wk, wv, wg = (w_in[:, i * w:(i + 1) * w].astype(BF16) for i in range(4))
    (q,), (k, kb), (v, vb), (g,) = _proj(h, nw, [
        _seg(wq, (BF16,), "qk", (A_HD ** -0.5) * LOG2E, q_gain),
        _seg(wk, (F32, BF16), "qk", gain=k_gain, stack=(layers, layer, kbuf)),
        _seg(wv, (F32, BF16), stack=(layers, layer, vbuf)),
        _seg(wg, (BF16,))], cos=cos, sin=sin, name=f"{tag}_proj")
    if cache is None:
        o = _attn_prompt(q, kb, vb, g, lamp, sub_gain, lam_init, name=f"{tag}_attn")
    else:
        kc, vc = cache
        o = _attn_sample(q, kb, vb, g, kc, vc, layer, lamp, sub_gain, lam_init, name=f"{tag}_attn")
    return _outproj(o, w_out.astype(BF16), h, name=f"{tag}_out"), k, v


def _retention_layer(h, pos, nw, w_in, w_out, s0, lc, tag):
    heads = s0.shape[1]
    qk_w, v_w = heads * R_DK, heads * R_DV
    cos, sin = _rope_tables(pos, R_DK, signed=False)
    wq = w_in[:, :qk_w].astype(BF16)
    wk = w_in[:, qk_w:2 * qk_w].astype(BF16)
    assert w_in.shape[1] == 2 * qk_w + 2 * v_w
    wvg = w_in[:, 2 * qk_w:].astype(BF16)
    (q,), (k,), (vg,) = _proj(h, nw, [
        _seg(wq, (BF16,), "rope256"),
        _seg(wk, (BF16,), "rope256", R_DK ** -0.5),
        _seg(wvg, (BF16,))], cos=cos, sin=sin, name=f"{tag}_proj")
    o, s_new = _retention(q, k, vg, s0, lc, name=f"{tag}_ret")
    return _outproj(o, w_out.astype(BF16), h, name=f"{tag}_out"), s_new


def _cmlp_layer(h, nw, w_in, w_out, v_gain, wmix, bmix, emit_v, tag):
    w = w_out.shape[0]
    wu, wv, wg = (w_in[:, i * w:(i + 1) * w].astype(BF16) for i in range(3))
    (ug,), (v,) = _proj(h, nw, [_seg(jnp.concatenate([wu, wg], axis=1), (BF16,)), _seg(wv, (F32,))],
                        name=f"{tag}_proj")
    res = _cmlp(ug, v, v_gain, wmix, bmix, emit_v=emit_v, name=f"{tag}_mix")
    a, vn = res if emit_v else (res, None)
    return _outproj(a, w_out.astype(BF16), h, name=f"{tag}_out"), vn


def _mix_tables(w_s, b_s, chunk_len):
    groups = w_s.shape[0]
    wl = jnp.tril(w_s[:, :chunk_len, :chunk_len])
    reps = M_CHUNK // chunk_len
    eye = jnp.eye(reps, dtype=F32)
    wt = jnp.einsum("ab,gij->gaibj", eye, wl).reshape(groups, M_CHUNK, M_CHUNK)
    bt = jnp.tile(b_s[:, :chunk_len], (1, reps))
    return wt.astype(BF16), jnp.broadcast_to(bt[:, :, None], (groups, M_CHUNK, LANES))


def kernel(x_prompt, x_sample, cache_k_attn, cache_v_attn, state_ret, norm_w, a_w_in, a_w_out, a_q_gain, a_k_gain, a_lam_q1, a_lam_k1, a_lam_q2, a_lam_k2, a_sub_gain, r_w_in, r_w_out, c_w_in, c_w_out, c_v_gain, c_w_s, c_b_s):
    batch, s_len, d = x_prompt.shape
    dec_b, dec_len, _ = x_sample.shape
    past = cache_k_attn.shape[2]
    depth = norm_w.shape[0]
    assert batch == 1 and M_CHUNK % dec_len == 0 and s_len % M_CHUNK == 0

    hp = x_prompt.reshape(s_len, d)
    hs = x_sample.reshape(dec_b * dec_len, d)
    pos_p = jnp.arange(s_len, dtype=jnp.int32)
    pos_s = jnp.tile(past + jnp.arange(dec_len, dtype=jnp.int32), dec_b)

    n_a = a_w_in.shape[0]
    aw = a_w_out.shape[1]
    cache = (cache_k_attn.reshape(n_a, dec_b, past, aw), cache_v_attn.reshape(n_a, dec_b, past, aw))
    kp = vp = kn = vn = None
    sp_l, ss_l, vm_l = [], [], []
    for i in range(depth):
        kind, j = i % N_MIXERS, i // N_MIXERS
        if kind == 0:
            lam_init = 0.8 - 0.6 * math.exp(-0.3 * i)
            lamp = jnp.stack([a_lam_q1[j], a_lam_k1[j], a_lam_q2[j], a_lam_k2[j]])
            args = (norm_w[i], a_w_in[j], a_w_out[j], a_q_gain[j], a_k_gain[j], lamp, a_sub_gain[j], lam_init)
            hp, kp, vp = _diff_attn_layer(hp, pos_p, *args, n_a, j, kp, vp, None, f"l{i}p")
            hs, kn, vn = _diff_attn_layer(hs, pos_s, *args, n_a, j, kn, vn, cache, f"l{i}s")
        elif kind == 1:
            heads = state_ret.shape[2]
            s0 = jnp.zeros((batch, heads, R_DK, R_DV), F32)
            hp, st_p = _retention_layer(hp, pos_p, norm_w[i], r_w_in[j], r_w_out[j], s0,
                                        _tile(s_len, 256), f"l{i}p")
            hs, st_s = _retention_layer(hs, pos_s, norm_w[i], r_w_in[j], r_w_out[j],
                                        state_ret[j].astype(F32), dec_len, f"l{i}s")
            sp_l.append(st_p)
            ss_l.append(st_s)
        else:
            wp, bp = _mix_tables(c_w_s[j], c_b_s[j], M_CHUNK)
            ws, bs = _mix_tables(c_w_s[j], c_b_s[j], dec_len)
            hp, _ = _cmlp_layer(hp, norm_w[i], c_w_in[j], c_w_out[j], c_v_gain[j], wp, bp, False, f"l{i}p")
            hs, v_s = _cmlp_layer(hs, norm_w[i], c_w_in[j], c_w_out[j], c_v_gain[j], ws, bs, True, f"l{i}s")
            vm_l.append(v_s.reshape(dec_b, dec_len, -1))

    return (hp.reshape(batch, s_len, d), hs.reshape(dec_b, dec_len, d),
            kp.reshape(n_a, batch, s_len, aw // A_HD, A_HD), vp.reshape(n_a, batch, s_len, aw // A_VD, A_VD),
            kn.reshape(n_a, dec_b, dec_len, aw // A_HD, A_HD), vn.reshape(n_a, dec_b, dec_len, aw // A_VD, A_VD),
            jnp.stack(sp_l), jnp.stack(ss_l), jnp.stack(vm_l))
```

```python
import functools
import math

import jax
import jax.numpy as jnp
from jax import lax
from jax.experimental import pallas as pl
from jax.experimental.pallas import tpu as pltpu

F32 = jnp.float32
BF16 = jnp.bfloat16

EPS = 1e-6
CHUNK = 64
ROPE_THETA = 10000.0
N_MIXERS = 3
A_HD = 64
A_VD = 2 * A_HD
R_DK = 256
R_DV = 2 * R_DK
M_GROUPS = 8
M_CHUNK = 128
LOG2E = 1.4426950408889634

LANES = 128
MXU_COLS = 256
VMEM_LIMIT = 56 * 1024 * 1024
NEG_BIG = -1e30


def _tile(n, pref):
    if n <= pref:
        return n
    t = pref
    while t >= 8:
        if n % t == 0:
            return t
        t -= 8
    return n


def _params(sem):
    return pltpu.CompilerParams(dimension_semantics=sem, vmem_limit_bytes=VMEM_LIMIT)


def _gelu(x):
    return 0.5 * x * (1.0 + jnp.tanh(0.7978845608028654 * (x + 0.044715 * (x * x * x))))


def _sigmoid(x):
    return 1.0 / (1.0 + jnp.exp(-x))


def _proj_kernel(*refs, plan):
    xn_ref = refs[-1]
    it = iter(refs)
    x_ref, nw_ref = next(it), next(it)
    epis = [p[0] for p in plan]
    cos_ref, sin_ref = (next(it), next(it)) if any(e != "none" for e in epis) else (None, None)
    gsum_ref = next(it) if "qk" in epis else None
    w_refs, gain_refs = [], []
    for e in epis:
        w_refs.append(next(it))
        gain_refs.append(next(it) if e == "qk" else None)
    outs = list(refs[len(refs) - 1 - sum(p[2] for p in plan):-1])
    o_refs = []
    for p in plan:
        o_refs.append(outs[:p[2]])
        outs = outs[p[2]:]

    def store(seg, cols, val):
        scale = plan[seg][1]
        if scale != 1.0:
            val = val * scale
        for o_ref in o_refs[seg]:
            o_ref[:, cols] = val.astype(o_ref.dtype)

    @pl.when(pl.program_id(1) == 0)
    def _():
        x = x_ref[...]
        ms = jnp.mean(x * x, axis=-1, keepdims=True)
        xn_ref[...] = (x * lax.rsqrt(ms + EPS) * nw_ref[...]).astype(BF16)

    qk = [s for s, e in enumerate(epis) if e == "qk"]
    rest = [s for s, e in enumerate(epis) if e == "rope256"] + [s for s, e in enumerate(epis) if e == "none"]
    slabs = {s: [slice(c * MXU_COLS, (c + 1) * MXU_COLS) for c in range(w_refs[s].shape[1] // MXU_COLS)]
             for s in qk}
    zq = {(s, c): jnp.dot(xn_ref[...], w_refs[s][:, sl], preferred_element_type=F32)
          for s in qk for c, sl in enumerate(slabs[s])}
    sq = {key: jnp.dot((z * z).astype(BF16), gsum_ref[...], preferred_element_type=F32) for key, z in zq.items()}
    zr = {s: jnp.dot(xn_ref[...], w_refs[s][...], preferred_element_type=F32) for s in rest}

    if qk:
        cos = jnp.concatenate([cos_ref[...]] * 2, axis=1)
        sin = jnp.concatenate([sin_ref[...]] * 2, axis=1)
        lane = lax.broadcasted_iota(jnp.int32, (xn_ref.shape[0], MXU_COLS), 1)
        first_half = (lane % A_HD) < (A_HD // 2)
        for (s, c), z in zq.items():
            gain = jnp.concatenate([gain_refs[s][...]] * 2, axis=1)
            zn = z * lax.rsqrt(sq[(s, c)] * (1.0 / A_HD) + EPS) * gain
            partner = jnp.where(first_half,
                                pltpu.roll(zn, MXU_COLS - A_HD // 2, 1),
                                pltpu.roll(zn, A_HD // 2, 1))
            store(s, slabs[s][c], zn * cos + partner * sin)
    for s in rest:
        z = zr[s]
        if epis[s] == "rope256":
            cos = cos_ref[...]
            sin = sin_ref[...]
            for c in range(z.shape[1] // R_DK):
                lo, hi = slice(c * R_DK, c * R_DK + LANES), slice(c * R_DK + LANES, (c + 1) * R_DK)
                x1, x2 = z[:, lo], z[:, hi]
                store(s, lo, x1 * cos - x2 * sin)
                store(s, hi, x2 * cos + x1 * sin)
        else:
            store(s, slice(None), z)


def _seg(w, out_dtypes, epi="none", out_scale=1.0, gain=None, stack=None):
    return dict(w=w, out_dtypes=out_dtypes, epi=epi, out_scale=out_scale, gain=gain, stack=stack)


PROJ_VMEM_BUDGET = 48 * 1024 * 1024


def _proj(x, nw, segs, *, cos=None, sin=None, name):
    m, d = x.shape
    tm = _tile(m, 1024)
    epis = [s["epi"] for s in segs]
    assert not ("qk" in epis and "rope256" in epis)
    unit = {"qk": MXU_COLS, "rope256": R_DK, "none": LANES}

    def vmem_bytes(nj):
        total = 2 * tm * d * 4 + tm * d * 2
        for s in segs:
            tn = s["w"].shape[1] // nj
            total += 2 * d * tn * 2 + tm * tn * 4
            total += sum(2 * tm * tn * jnp.dtype(dt).itemsize for dt in s["out_dtypes"])
        return total

    nj = next(c for c in (1, 2, 4, 8, 16, 32)
              if all(s["w"].shape[1] % (c * unit[s["epi"]]) == 0 for s in segs) and vmem_bytes(c) <= PROJ_VMEM_BUDGET)
    in_specs = [pl.BlockSpec((tm, d), lambda i, j: (i, 0)),
                pl.BlockSpec((1, d), lambda i, j: (0, 0))]
    args = [x, nw.reshape(1, d)]
    if any(e != "none" for e in epis):
        in_specs += [pl.BlockSpec((tm, LANES), lambda i, j: (i, 0))] * 2
        args += [cos, sin]
    if "qk" in epis:
        gidx = jnp.arange(MXU_COLS) // A_HD
        in_specs.append(pl.BlockSpec((MXU_COLS, MXU_COLS), lambda i, j: (0, 0)))
        args.append((gidx[:, None] == gidx[None, :]).astype(BF16))
    out_shape, out_specs, bufs = [], [], []
    for s in segs:
        n = s["w"].shape[1]
        tn = n // nj
        in_specs.append(pl.BlockSpec((d, tn), lambda i, j: (0, j)))
        args.append(s["w"])
        if s["epi"] == "qk":
            in_specs.append(pl.BlockSpec((1, LANES), lambda i, j: (0, 0)))
            args.append(jnp.tile(s["gain"].reshape(1, A_HD), (1, LANES // A_HD)))
        for o, dt in enumerate(s["out_dtypes"]):
            if o == 0 and s["stack"] is not None:
                layers, layer, buf = s["stack"]
                out_specs.append(pl.BlockSpec((None, tm, tn), lambda i, j, layer=layer: (layer, i, j)))
                out_shape.append(jax.ShapeDtypeStruct((layers, m, n), dt))
                if buf is not None:
                    bufs.append((len(out_shape) - 1, buf))
            else:
                out_specs.append(pl.BlockSpec((tm, tn), lambda i, j: (i, j)))
                out_shape.append(jax.ShapeDtypeStruct((m, n), dt))
    aliases = {}
    for out_idx, buf in bufs:
        aliases[len(args)] = out_idx
        in_specs.append(pl.BlockSpec(memory_space=pl.ANY))
        args.append(buf)
    res = list(pl.pallas_call(
        functools.partial(_proj_kernel, plan=tuple((s["epi"], s["out_scale"], len(s["out_dtypes"])) for s in segs)),
        out_shape=tuple(out_shape),
        grid=(m // tm, nj),
        in_specs=in_specs,
        out_specs=tuple(out_specs),
        scratch_shapes=[pltpu.VMEM((tm, d), BF16)],
        input_output_aliases=aliases,
        compiler_params=_params(("parallel", "arbitrary")),
        name=name,
    )(*args))
    out = []
    for s in segs:
        out.append(res[:len(s["out_dtypes"])])
        res = res[len(s["out_dtypes"]):]
    return out


def _outproj_kernel(a_ref, w_ref, h_ref, o_ref):
    o_ref[...] = h_ref[...] + jnp.dot(a_ref[...], w_ref[...], preferred_element_type=F32)


def _outproj(a, w, h, *, name):
    m, k = a.shape
    n = w.shape[1]
    tm = _tile(m, 1024)
    tn = _tile(n, 1024)
    return pl.pallas_call(
        _outproj_kernel,
        out_shape=jax.ShapeDtypeStruct((m, n), F32),
        grid=(m // tm, n // tn),
        in_specs=[pl.BlockSpec((tm, k), lambda i, j: (i, 0)),
                  pl.BlockSpec((k, tn), lambda i, j: (0, j)),
                  pl.BlockSpec((tm, tn), lambda i, j: (i, j))],
        out_specs=pl.BlockSpec((tm, tn), lambda i, j: (i, j)),
        compiler_params=_params(("parallel", "parallel")),
        name=name,
    )(a, w, h)


def _lambda(lamp_ref, lam_init):
    lp = lamp_ref[...]
    s1 = jnp.sum(lp[0:1, :] * lp[1:2, :], axis=-1, keepdims=True)
    s2 = jnp.sum(lp[2:3, :] * lp[3:4, :], axis=-1, keepdims=True)
    return jnp.exp(s1) - jnp.exp(s2) + lam_init


V_AUG = A_VD + 16


LOOKAHEAD = 5


def _attn_prompt_kernel(q_ref, k_ref, v_ref, g_ref, lamp_ref, sg_ref, o_ref,
                        qpad_ref, vt_ref, m_ref, acc_ref, s_ref, mx_ref, *, tq, tk, lam_init):
    i = pl.program_id(1)

    @pl.when(i == 0)
    def _():
        aug = lax.broadcasted_iota(jnp.int32, (V_AUG - A_VD, tk), 0)
        ones_row = jnp.where(aug == 0, 1.0, 0.0).astype(BF16)

        def fill(t, carry):
            vj = v_ref[pl.ds(pl.multiple_of(t * tk, tk), tk), :].astype(F32)
            vt_ref[t, :A_VD, :] = vj.T.astype(BF16)
            vt_ref[t, A_VD:, :] = ones_row
            return carry

        lax.fori_loop(0, vt_ref.shape[0], fill, 0)

    qt = q_ref[...].astype(F32).T
    row = lax.broadcasted_iota(jnp.int32, qt.shape, 0)
    qpad_ref[:, :tq] = jnp.where(row < A_HD, qt, 0.0).astype(BF16)
    qpad_ref[:, tq:] = jnp.where(row >= A_HD, qt, 0.0).astype(BF16)
    m_ref[...] = jnp.full(m_ref.shape, NEG_BIG, F32)
    acc_ref[...] = jnp.zeros(acc_ref.shape, F32)

    sw = s_ref.shape[2]
    n_strips = 2 * tq // sw
    la = min(LOOKAHEAD, n_strips - 1)

    def scores(j, c, rows=tk):
        kj = k_ref[pl.ds(pl.multiple_of(j * tk, tk), rows), :]
        s = jnp.dot(kj, qpad_ref[:, c * sw:(c + 1) * sw], preferred_element_type=F32)
        s_ref[c, :rows] = s
        mx_ref[c] = jnp.max(s, axis=0, keepdims=True)

    def update(j, c, mask, rows=tk):
        cs = slice(c * sw, (c + 1) * sw)
        s = s_ref[c, :rows]
        if mask is None:
            mx = mx_ref[c]
        else:
            s = jnp.where(mask[:rows], s, NEG_BIG)
            mx = jnp.max(s, axis=0, keepdims=True)
        m_old = m_ref[:, cs]
        m_new = jnp.maximum(m_old, mx)
        alpha = jnp.exp2(m_old - m_new)
        p = jnp.exp2(s - m_new).astype(BF16)
        pv = jnp.dot(vt_ref[j][:, :rows], p, preferred_element_type=F32)
        acc_ref[:, cs] = alpha * acc_ref[:, cs] + pv
        m_ref[:, cs] = m_new

    n_sub = tq // tk
    n_full = i * n_sub
    order = sorted(range(n_strips), key=lambda c: -((c * sw) % tq))

    for c in order[:la]:
        scores(0, c)

    def full_tiles(j0, count):
        for j in range(j0, j0 + count) if isinstance(j0, int) else [j0 + t for t in range(count)]:
            for n, c in enumerate(order):
                update(j, c, None)
                if n + la < n_strips:
                    scores(j, order[n + la])
                else:
                    scores(j + 1, order[n + la - n_strips])

    odd = i % 2

    @pl.when(odd == 1)
    def _():
        full_tiles(0, n_sub)

    def body(jj, carry):
        full_tiles(odd * n_sub + jj * (2 * n_sub), 2 * n_sub)
        return carry

    lax.fori_loop(0, i // 2, body, 0)

    r = lax.broadcasted_iota(jnp.int32, (tk, sw), 0)
    cc = lax.broadcasted_iota(jnp.int32, (tk, sw), 1)
    units = []
    for d in range(n_sub):
        for c in order:
            q_lo, k_lo = (c * sw) % tq, d * tk
            if q_lo + sw <= k_lo:
                continue
            full = q_lo >= k_lo + tk
            rows = min(tk, q_lo + sw - k_lo)
            units.append((d, c, None if full else (k_lo + r) // CHUNK <= (q_lo + cc) // CHUNK, rows))
    assert [u[:2] + u[3:] for u in units[:la]] == [(0, c, tk) for c in order[:la]]
    for n, (d, c, mask, rows) in enumerate(units):
        update(n_full + d, c, mask, rows)
        if n + la < len(units):
            nd, nc, _, nrows = units[n + la]
            assert all(u[1] != nc for u in units[n + 1:n + la])
            scores(n_full + nd, nc, nrows)

    lam = _lambda(lamp_ref, lam_init)
    acc = acc_ref[...]
    inv_l = 1.0 / acc[A_VD:A_VD + 1, :]
    ot = acc[:A_VD, :tq] * inv_l[:, :tq] - lam * (acc[:A_VD, tq:] * inv_l[:, tq:])
    ms = jnp.mean(ot * ot, axis=0, keepdims=True)
    ot = ot * lax.rsqrt(ms + EPS) * (sg_ref[...] * (1.0 - lam_init))
    g = g_ref[...].astype(F32)
    o_ref[...] = (ot.T * (g * _sigmoid(g))).astype(o_ref.dtype)


def _attn_prompt(q, k, v, g, lamp, sub_gain, lam_init, *, name):
    s, w = q.shape
    heads = w // A_VD
    tq = _tile(s, 1024)
    tk = _tile(tq, 512)
    sw = min(MXU_COLS, tq)
    return pl.pallas_call(
        functools.partial(_attn_prompt_kernel, tq=tq, tk=tk, lam_init=lam_init),
        out_shape=jax.ShapeDtypeStruct((s, w), BF16),
        grid=(heads, s // tq),
        in_specs=[pl.BlockSpec((tq, A_VD), lambda h, i: (i, h)),
                  pl.BlockSpec((s, A_VD), lambda h, i: (0, h)),
                  pl.BlockSpec((s, A_VD), lambda h, i: (0, h)),
                  pl.BlockSpec((tq, A_VD), lambda h, i: (i, h)),
                  pl.BlockSpec((4, A_HD), lambda h, i: (0, 0)),
                  pl.BlockSpec((A_VD, 1), lambda h, i: (0, 0))],
        out_specs=pl.BlockSpec((tq, A_VD), lambda h, i: (i, h)),
        scratch_shapes=[pltpu.VMEM((A_VD, 2 * tq), BF16),
                        pltpu.VMEM((s // tk, V_AUG, tk), BF16),
                        pltpu.VMEM((1, 2 * tq), F32),
                        pltpu.VMEM((V_AUG, 2 * tq), F32),
                        pltpu.VMEM((2 * tq // sw, tk, sw), F32),
                        pltpu.VMEM((2 * tq // sw, 1, sw), F32)],
        compiler_params=_params(("parallel", "arbitrary")),
        name=name,
    )(q, k, v, g, lamp, sub_gain.reshape(A_VD, 1))


SAMPLE_HEADS_PER_STEP = 4


def _attn_sample_kernel(q_ref, kn_ref, vn_ref, g_ref, kc_ref, vc_ref, lamp_ref, sg_ref, o_ref, *, lam_init, hps):
    lam = _lambda(lamp_ref, lam_init)
    nt = (((1,), (1,)), ((), ()))
    lane = lax.broadcasted_iota(jnp.int32, (q_ref.shape[0], A_VD), 1)
    heads = [slice(h * A_VD, (h + 1) * A_VD) for h in range(hps)]
    scores = []
    for hs in heads:
        q = q_ref[:, hs]
        kc = kc_ref[0, :, hs].astype(BF16)
        kn = kn_ref[:, hs].astype(BF16)
        for half in range(2):
            qh = jnp.where((lane >= A_HD) == bool(half), q, jnp.zeros_like(q))
            scores.append((lax.dot_general(qh, kc, nt, preferred_element_type=F32),
                           lax.dot_general(qh, kn, nt, preferred_element_type=F32)))
    probs = []
    for sc, sn in scores:
        m = jnp.maximum(jnp.max(sc, axis=-1, keepdims=True), jnp.max(sn, axis=-1, keepdims=True))
        pc = jnp.exp2(sc - m)
        pn = jnp.exp2(sn - m)
        inv = 1.0 / (jnp.sum(pc, axis=-1, keepdims=True) + jnp.sum(pn, axis=-1, keepdims=True))
        probs.append((pc * inv, pn * inv))
    outs = []
    for h, hs in enumerate(heads):
        ac = (probs[2 * h][0] - lam * probs[2 * h + 1][0]).astype(BF16)
        an = (probs[2 * h][1] - lam * probs[2 * h + 1][1]).astype(BF16)
        outs.append(jnp.dot(ac, vc_ref[0, :, hs].astype(BF16), preferred_element_type=F32)
                    + jnp.dot(an, vn_ref[:, hs].astype(BF16), preferred_element_type=F32))
    for hs, o in zip(heads, outs):
        ms = jnp.mean(o * o, axis=-1, keepdims=True)
        g = g_ref[:, hs].astype(F32)
        o = o * lax.rsqrt(ms + EPS) * (sg_ref[...] * (1.0 - lam_init))
        o_ref[:, hs] = (o * (g * _sigmoid(g))).astype(o_ref.dtype)


def _attn_sample(q, k, v, g, kc, vc, layer, lamp, sub_gain, lam_init, *, name):
    _, bsz, past, w = kc.shape
    heads = w // A_VD
    ln = q.shape[0] // bsz
    assert past % CHUNK == 0 and ln <= CHUNK
    hps = SAMPLE_HEADS_PER_STEP
    assert heads % hps == 0
    row = lambda b, h: (b, h)
    cache = lambda b, h: (layer, b, 0, h)
    return pl.pallas_call(
        functools.partial(_attn_sample_kernel, lam_init=lam_init, hps=hps),
        out_shape=jax.ShapeDtypeStruct(q.shape, BF16),
        grid=(bsz, heads // hps),
        in_specs=[pl.BlockSpec((ln, hps * A_VD), row),
                  pl.BlockSpec((ln, hps * A_VD), row),
                  pl.BlockSpec((ln, hps * A_VD), row),
                  pl.BlockSpec((ln, hps * A_VD), row),
                  pl.BlockSpec((None, 1, past, hps * A_VD), cache),
                  pl.BlockSpec((None, 1, past, hps * A_VD), cache),
                  pl.BlockSpec((4, A_HD), lambda b, h: (0, 0)),
                  pl.BlockSpec((1, A_VD), lambda b, h: (0, 0))],
        out_specs=pl.BlockSpec((ln, hps * A_VD), row),
        compiler_params=_params(("parallel", "parallel")),
        name=name,
    )(q, k, v, g, kc, vc, lamp, sub_gain.reshape(1, A_VD))


RET_HEADS_PER_STEP = 2


def _retention_kernel(q_ref, k_ref, v_ref, g_ref, s0_ref, dec_ref, cd_ref, kd_ref, gl_ref,
                      o_ref, sout_ref, st_ref, *, lc, hps):
    c = pl.program_id(2)

    @pl.when(c == 0)
    def _():
        st_ref[...] = s0_ref[0]

    hd = range(hps)
    qs = [q_ref[:, h * R_DK:(h + 1) * R_DK] for h in hd]
    ks = [k_ref[:, h * R_DK:(h + 1) * R_DK] for h in hd]
    vs = [v_ref[:, h * R_DV:(h + 1) * R_DV] for h in hd]
    sts = [st_ref[h] for h in hd]
    nt = (((1,), (1,)), ((), ()))
    inner = [lax.dot_general(qs[h], ks[h], nt, preferred_element_type=F32) for h in hd]
    cross = [jnp.dot(qs[h], sts[h].astype(BF16), preferred_element_type=F32) for h in hd]
    kdec = []
    for h in hd:
        kd = ks[h].astype(F32) * jnp.concatenate([kd_ref[h]] * (R_DK // LANES), axis=1)
        if lc < LANES:
            kd = jnp.concatenate([kd, jnp.zeros((LANES - lc, R_DK), F32)], axis=0)
        kdec.append(kd.T.astype(BF16))
    inner = [(inner[h] * dec_ref[h]).astype(BF16) for h in hd]
    o = [jnp.dot(inner[h], vs[h], preferred_element_type=F32) for h in hd]
    upd = []
    for h in hd:
        vv = vs[h]
        if lc < LANES:
            vv = jnp.concatenate([vv, jnp.zeros((LANES - lc, R_DV), BF16)], axis=0)
        upd.append(jnp.dot(kdec[h], vv, preferred_element_type=F32))
    for h in hd:
        cd = jnp.concatenate([cd_ref[h]] * (R_DV // LANES), axis=1)
        oh = o[h] + cross[h] * cd
        ms = jnp.mean(oh * oh, axis=-1, keepdims=True)
        g = g_ref[:, h * R_DV:(h + 1) * R_DV].astype(F32)
        o_ref[:, h * R_DV:(h + 1) * R_DV] = (oh * lax.rsqrt(ms + EPS) * (g * _sigmoid(g))).astype(o_ref.dtype)
    st_new = []
    for h in hd:
        gl = jnp.concatenate([gl_ref[h]] * (R_DV // LANES), axis=1)
        st_new.append(sts[h] * gl + upd[h])
        st_ref[h] = st_new[h]

    @pl.when(c == pl.num_programs(2) - 1)
    def _():
        for h in hd:
            sout_ref[0, h] = st_new[h]


def _retention(q, k, vg, s0, lc, *, name):
    bsz, heads = s0.shape[:2]
    t = q.shape[0] // bsz
    nc = t // lc
    lg = jnp.log1p(-(2.0 ** (-5.0 - jnp.arange(heads, dtype=F32))))
    idx = jnp.arange(lc, dtype=F32)
    diff = idx[:, None] - idx[None, :]
    dec = jnp.where(diff >= 0, jnp.exp(lg[:, None, None] * jnp.maximum(diff, 0.0)), 0.0)
    cd = jnp.broadcast_to(jnp.exp(lg[:, None] * (idx[None, :] + 1.0))[:, :, None], (heads, lc, LANES))
    kd = jnp.broadcast_to(jnp.exp(lg[:, None] * (lc - 1.0 - idx[None, :]))[:, :, None], (heads, lc, LANES))
    gl = jnp.broadcast_to(jnp.exp(lg * lc)[:, None, None], (heads, 1, LANES))
    hps = RET_HEADS_PER_STEP
    assert heads % hps == 0
    rows = lambda b, h, c: (b * nc + c, h)
    gate = lambda b, h, c: (b * nc + c, heads // hps + h)
    tab = lambda b, h, c: (h, 0, 0)
    state = lambda b, h, c: (b, h, 0, 0)
    return pl.pallas_call(
        functools.partial(_retention_kernel, lc=lc, hps=hps),
        out_shape=(jax.ShapeDtypeStruct((vg.shape[0], heads * R_DV), BF16),
                   jax.ShapeDtypeStruct(s0.shape, F32)),
        grid=(bsz, heads // hps, nc),
        in_specs=[pl.BlockSpec((lc, hps * R_DK), rows),
                  pl.BlockSpec((lc, hps * R_DK), rows),
                  pl.BlockSpec((lc, hps * R_DV), rows),
                  pl.BlockSpec((lc, hps * R_DV), gate),
                  pl.BlockSpec((1, hps, R_DK, R_DV), state),
                  pl.BlockSpec((hps, lc, lc), tab),
                  pl.BlockSpec((hps, lc, LANES), tab),
                  pl.BlockSpec((hps, lc, LANES), tab),
                  pl.BlockSpec((hps, 1, LANES), tab)],
        out_specs=(pl.BlockSpec((lc, hps * R_DV), rows),
                   pl.BlockSpec((1, hps, R_DK, R_DV), state)),
        scratch_shapes=[pltpu.VMEM((hps, R_DK, R_DV), F32)],
        compiler_params=_params(("parallel", "parallel", "arbitrary")),
        name=name,
    )(q, k, vg, vg, s0, dec, cd, kd, gl)


def _cmlp_kernel(u_ref, g_ref, v_ref, vg_ref, w_ref, b_ref, *out_refs, emit_v):
    if emit_v:
        a_ref, vn_ref = out_refs
    else:
        (a_ref,) = out_refs
    va = _gelu(v_ref[...])
    ms = jnp.mean(va * va, axis=-1, keepdims=True)
    vn = va * lax.rsqrt(ms + EPS) * vg_ref[...]
    if emit_v:
        vn_ref[...] = vn
    vb = vn.astype(BF16)
    gd = vn.shape[1] // M_GROUPS
    for grp in range(M_GROUPS):
        sl = slice(grp * gd, (grp + 1) * gd)
        bias = jnp.concatenate([b_ref[grp]] * (gd // LANES), axis=1)
        mix = jnp.dot(w_ref[grp], vb[:, sl], preferred_element_type=F32) + bias
        g = g_ref[:, sl].astype(F32)
        a_ref[:, sl] = (_gelu(u_ref[:, sl].astype(F32)) * mix * (g * _sigmoid(g))).astype(a_ref.dtype)


def _cmlp(ug, v, v_gain, wmix, bmix, *, emit_v, name):
    m, w = v.shape
    t = wmix.shape[1]
    out_shape = [jax.ShapeDtypeStruct((m, w), BF16)]
    out_specs = [pl.BlockSpec((t, w), lambda i: (i, 0))]
    if emit_v:
        out_shape.append(jax.ShapeDtypeStruct((m, w), F32))
        out_specs.append(pl.BlockSpec((t, w), lambda i: (i, 0)))
    res = pl.pallas_call(
        functools.partial(_cmlp_kernel, emit_v=emit_v),
        out_shape=tuple(out_shape),
        grid=(m // t,),
        in_specs=[pl.BlockSpec((t, w), lambda i: (i, 0)),
                  pl.BlockSpec((t, w), lambda i: (i, 1)),
                  pl.BlockSpec((t, w), lambda i: (i, 0)),
                  pl.BlockSpec((1, w), lambda i: (0, 0)),
                  pl.BlockSpec((M_GROUPS, t, t), lambda i: (0, 0, 0)),
                  pl.BlockSpec((M_GROUPS, t, LANES), lambda i: (0, 0, 0))],
        out_specs=tuple(out_specs),
        compiler_params=_params(("parallel",)),
        name=name,
    )(ug, ug, v, v_gain.reshape(1, w), wmix, bmix)
    return res if emit_v else res[0]


def _rope_tables(pos, d, signed):
    inv = ROPE_THETA ** (-jnp.arange(0, d, 2, dtype=F32) / d)
    ang = pos.astype(F32)[:, None] * inv[None, :]
    cos, sin = jnp.cos(ang), jnp.sin(ang)
    reps = LANES // (d // 2)
    if signed:
        return (jnp.tile(cos, (1, reps)),
                jnp.tile(jnp.concatenate([-sin, sin], axis=1), (1, reps // 2)))
    return jnp.tile(cos, (1, reps)), jnp.tile(sin, (1, reps))


def _diff_attn_layer(h, pos, nw, w_in, w_out, q_gain, k_gain, lamp, sub_gain, lam_init,
                     layers, layer, kbuf, vbuf, cache, tag):
    w = w_out.shape[0]
    cos, sin = _rope_tables(pos, A_HD, signed=True)
    wq, wk, wv, wg = (w_in[:, i * w:(i + 1) * w].astype(BF16) for i in range(4))
    (q,), (k, kb), (v, vb), (g,) = _proj(h, nw, [
        _seg(wq, (BF16,), "qk", (A_HD ** -0.5) * LOG2E, q_gain),
        _seg(wk, (F32, BF16), "qk", gain=k_gain, stack=(layers, layer, kbuf)),
        _seg(wv, (F32, BF16), stack=(layers, layer, vbuf)),
        _seg(wg, (BF16,))], cos=cos, sin=sin, name=f"{tag}_proj")
    if cache is None:
        o = _attn_prompt(q, kb, vb, g, lamp, sub_gain, lam_init, name=f"{tag}_attn")
    else:
        kc, vc = cache
        o = _attn_sample(q, kb, vb, g, kc, vc, layer, lamp, sub_gain, lam_init, name=f"{tag}_attn")
    return _outproj(o, w_out.astype(BF16), h, name=f"{tag}_out"), k, v


def _retention_layer(h, pos, nw, w_in, w_out, s0, lc, tag):
    heads = s0.shape[1]
    qk_w, v_w = heads * R_DK, heads * R_DV
    cos, sin = _rope_tables(pos, R_DK, signed=False)
    wq = w_in[:, :qk_w].astype(BF16)
    wk = w_in[:, qk_w:2 * qk_w].astype(BF16)
    assert w_in.shape[1] == 2 * qk_w + 2 * v_w
    wvg = w_in[:, 2 * qk_w:].astype(BF16)
    (q,), (k,), (vg,) = _proj(h, nw, [
        _seg(wq, (BF16,), "rope256"),
        _seg(wk, (BF16,), "rope256", R_DK ** -0.5),
        _seg(wvg, (BF16,))], cos=cos, sin=sin, name=f"{tag}_proj")
    o, s_new = _retention(q, k, vg, s0, lc, name=f"{tag}_ret")
    return _outproj(o, w_out.astype(BF16), h, name=f"{tag}_out"), s_new


def _cmlp_layer(h, nw, w_in, w_out, v_gain, wmix, bmix, emit_v, tag):
    w = w_out.shape[0]
    wu, wv, wg = (w_in[:, i * w:(i + 1) * w].astype(BF16) for i in range(3))
    (ug,), (v,) = _proj(h, nw, [_seg(jnp.concatenate([wu, wg], axis=1), (BF16,)), _seg(wv, (F32,))],
                        name=f"{tag}_proj")
    res = _cmlp(ug, v, v_gain, wmix, bmix, emit_v=emit_v, name=f"{tag}_mix")
    a, vn = res if emit_v else (res, None)
    return _outproj(a, w_out.astype(BF16), h, name=f"{tag}_out"), vn


def _mix_tables(w_s, b_s, chunk_len):
    groups = w_s.shape[0]
    wl = jnp.tril(w_s[:, :chunk_len, :chunk_len])
    reps = M_CHUNK // chunk_len
    eye = jnp.eye(reps, dtype=F32)
    wt = jnp.einsum("ab,gij->gaibj", eye, wl).reshape(groups, M_CHUNK, M_CHUNK)
    bt = jnp.tile(b_s[:, :chunk_len], (1, reps))
    return wt.astype(BF16), jnp.broadcast_to(bt[:, :, None], (groups, M_CHUNK, LANES))


def kernel(x_prompt, x_sample, cache_k_attn, cache_v_attn, state_ret, norm_w, a_w_in, a_w_out, a_q_gain, a_k_gain, a_lam_q1, a_lam_k1, a_lam_q2, a_lam_k2, a_sub_gain, r_w_in, r_w_out, c_w_in, c_w_out, c_v_gain, c_w_s, c_b_s):
    batch, s_len, d = x_prompt.shape
    dec_b, dec_len, _ = x_sample.shape
    past = cache_k_attn.shape[2]
    depth = norm_w.shape[0]
    assert batch == 1 and M_CHUNK % dec_len == 0 and s_len % M_CHUNK == 0

    hp = x_prompt.reshape(s_len, d)
    hs = x_sample.reshape(dec_b * dec_len, d)
    pos_p = jnp.arange(s_len, dtype=jnp.int32)
    pos_s = jnp.tile(past + jnp.arange(dec_len, dtype=jnp.int32), dec_b)

    n_a = a_w_in.shape[0]
    aw = a_w_out.shape[1]
    cache = (cache_k_attn.reshape(n_a, dec_b, past, aw), cache_v_attn.reshape(n_a, dec_b, past, aw))
    kp = vp = kn = vn = None
    sp_l, ss_l, vm_l = [], [], []
    for i in range(depth):
        kind, j = i % N_MIXERS, i // N_MIXERS
        if kind == 0:
            lam_init = 0.8 - 0.6 * math.exp(-0.3 * i)
            lamp = jnp.stack([a_lam_q1[j], a_lam_k1[j], a_lam_q2[j], a_lam_k2[j]])
            args = (norm_w[i], a_w_in[j], a_w_out[j], a_q_gain[j], a_k_gain[j], lamp, a_sub_gain[j], lam_init)
            hp, kp, vp = _diff_attn_layer(hp, pos_p, *args, n_a, j, kp, vp, None, f"l{i}p")
            hs, kn, vn = _diff_attn_layer(hs, pos_s, *args, n_a, j, kn, vn, cache, f"l{i}s")
        elif kind == 1:
            heads = state_ret.shape[2]
            s0 = jnp.zeros((batch, heads, R_DK, R_DV), F32)
            hp, st_p = _retention_layer(hp, pos_p, norm_w[i], r_w_in[j], r_w_out[j], s0,
                                        _tile(s_len, 256), f"l{i}p")
            hs, st_s = _retention_layer(hs, pos_s, norm_w[i], r_w_in[j], r_w_out[j],
                                        state_ret[j].astype(F32), dec_len, f"l{i}s")
            sp_l.append(st_p)
            ss_l.append(st_s)
        else:
            wp, bp = _mix_tables(c_w_s[j], c_b_s[j], M_CHUNK)
            ws, bs = _mix_tables(c_w_s[j], c_b_s[j], dec_len)
            hp, _ = _cmlp_layer(hp, norm_w[i], c_w_in[j], c_w_out[j], c_v_gain[j], wp, bp, False, f"l{i}p")
            hs, v_s = _cmlp_layer(hs, norm_w[i], c_w_in[j], c_w_out[j], c_v_gain[j], ws, bs, True, f"l{i}s")
            vm_l.append(v_s.reshape(dec_b, dec_len, -1))

    return (hp.reshape(batch, s_len, d), hs.reshape(dec_b, dec_len, d),
            kp.reshape(n_a, batch, s_len, aw // A_HD, A_HD), vp.reshape(n_a, batch, s_len, aw // A_VD, A_VD),
            kn.reshape(n_a, dec_b, dec_len, aw // A_HD, A_HD), vn.reshape(n_a, dec_b, dec_len, aw // A_VD, A_VD),
            jnp.stack(sp_l), jnp.stack(ss_l), jnp.stack(vm_l))
```

```python
import functools
import math

import jax
import jax.numpy as jnp
from jax import lax
from jax.experimental import pallas as pl
from jax.experimental.pallas import tpu as pltpu

F32 = jnp.float32
BF16 = jnp.bfloat16

EPS = 1e-6
CHUNK = 64
ROPE_THETA = 10000.0
N_MIXERS = 3
A_HD = 64
A_VD = 2 * A_HD
R_DK = 256
R_DV = 2 * R_DK
M_GROUPS = 8
M_CHUNK = 128
LOG2E = 1.4426950408889634

LANES = 128
MXU_COLS = 256
VMEM_LIMIT = 56 * 1024 * 1024
NEG_BIG = -1e30


def _tile(n, pref):
    if n <= pref:
        return n
    t = pref
    while t >= 8:
        if n % t == 0:
            return t
        t -= 8
    return n


def _params(sem):
    return pltpu.CompilerParams(dimension_semantics=sem, vmem_limit_bytes=VMEM_LIMIT)


def _gelu(x):
    return 0.5 * x * (1.0 + jnp.tanh(0.7978845608028654 * (x + 0.044715 * (x * x * x))))


def _sigmoid(x):
    return 1.0 / (1.0 + jnp.exp(-x))


def _proj_kernel(*refs, plan):
    xn_ref = refs[-1]
    it = iter(refs)
    x_ref, nw_ref = next(it), next(it)
    epis = [p[0] for p in plan]
    cos_ref, sin_ref = (next(it), next(it)) if any(e != "none" for e in epis) else (None, None)
    gsum_ref = next(it) if "qk" in epis else None
    w_refs, gain_refs = [], []
    for e in epis:
        w_refs.append(next(it))
        gain_refs.append(next(it) if e == "qk" else None)
    outs = list(refs[len(refs) - 1 - sum(p[2] for p in plan):-1])
    o_refs = []
    for p in plan:
        o_refs.append(outs[:p[2]])
        outs = outs[p[2]:]

    def store(seg, cols, val):
        scale = plan[seg][1]
        if scale != 1.0:
            val = val * scale
        for o_ref in o_refs[seg]:
            o_ref[:, cols] = val.astype(o_ref.dtype)

    @pl.when(pl.program_id(1) == 0)
    def _():
        x = x_ref[...]
        ms = jnp.mean(x * x, axis=-1, keepdims=True)
        xn_ref[...] = (x * lax.rsqrt(ms + EPS) * nw_ref[...]).astype(BF16)

    qk = [s for s, e in enumerate(epis) if e == "qk"]
    rest = [s for s, e in enumerate(epis) if e == "rope256"] + [s for s, e in enumerate(epis) if e == "none"]
    slabs = {s: [slice(c * MXU_COLS, (c + 1) * MXU_COLS) for c in range(w_refs[s].shape[1] // MXU_COLS)]
             for s in qk}
    zq = {(s, c): jnp.dot(xn_ref[...], w_refs[s][:, sl], preferred_element_type=F32)
          for s in qk for c, sl in enumerate(slabs[s])}
    sq = {key: jnp.dot((z * z).astype(BF16), gsum_ref[...], preferred_element_type=F32) for key, z in zq.items()}
    zr = {s: jnp.dot(xn_ref[...], w_refs[s][...], preferred_element_type=F32) for s in rest}

    if qk:
        cos = jnp.concatenate([cos_ref[...]] * 2, axis=1)
        sin = jnp.concatenate([sin_ref[...]] * 2, axis=1)
        lane = lax.broadcasted_iota(jnp.int32, (xn_ref.shape[0], MXU_COLS), 1)
        first_half = (lane % A_HD) < (A_HD // 2)
        for (s, c), z in zq.items():
            gain = jnp.concatenate([gain_refs[s][...]] * 2, axis=1)
            zn = z * lax.rsqrt(sq[(s, c)] * (1.0 / A_HD) + EPS) * gain
            partner = jnp.where(first_half,
                                pltpu.roll(zn, MXU_COLS - A_HD // 2, 1),
                                pltpu.roll(zn, A_HD // 2, 1))
            store(s, slabs[s][c], zn * cos + partner * sin)
    for s in rest:
        z = zr[s]
        if epis[s] == "rope256":
            cos = cos_ref[...]
            sin = sin_ref[...]
            for c in range(z.shape[1] // R_DK):
                lo, hi = slice(c * R_DK, c * R_DK + LANES), slice(c * R_DK + LANES, (c + 1) * R_DK)
                x1, x2 = z[:, lo], z[:, hi]
                store(s, lo, x1 * cos - x2 * sin)
                store(s, hi, x2 * cos + x1 * sin)
        else:
            store(s, slice(None), z)


def _seg(w, out_dtypes, epi="none", out_scale=1.0, gain=None, stack=None):
    return dict(w=w, out_dtypes=out_dtypes, epi=epi, out_scale=out_scale, gain=gain, stack=stack)


PROJ_VMEM_BUDGET = 48 * 1024 * 1024


def _proj(x, nw, segs, *, cos=None, sin=None, name):
    m, d = x.shape
    tm = _tile(m, 1024)
    epis = [s["epi"] for s in segs]
    assert not ("qk" in epis and "rope256" in epis)
    unit = {"qk": MXU_COLS, "rope256": R_DK, "none": LANES}

    def vmem_bytes(nj):
        total = 2 * tm * d * 4 + tm * d * 2
        for s in segs:
            tn = s["w"].shape[1] // nj
            total += 2 * d * tn * 2 + tm * tn * 4
            total += sum(2 * tm * tn * jnp.dtype(dt).itemsize for dt in s["out_dtypes"])
        return total

    nj = next(c for c in (1, 2, 4, 8, 16, 32)
              if all(s["w"].shape[1] % (c * unit[s["epi"]]) == 0 for s in segs) and vmem_bytes(c) <= PROJ_VMEM_BUDGET)
    in_specs = [pl.BlockSpec((tm, d), lambda i, j: (i, 0)),
                pl.BlockSpec((1, d), lambda i, j: (0, 0))]
    args = [x, nw.reshape(1, d)]
    if any(e != "none" for e in epis):
        in_specs += [pl.BlockSpec((tm, LANES), lambda i, j: (i, 0))] * 2
        args += [cos, sin]
    if "qk" in epis:
        gidx = jnp.arange(MXU_COLS) // A_HD
        in_specs.append(pl.BlockSpec((MXU_COLS, MXU_COLS), lambda i, j: (0, 0)))
        args.append((gidx[:, None] == gidx[None, :]).astype(BF16))
    out_shape, out_specs, bufs = [], [], []
    for s in segs:
        n = s["w"].shape[1]
        tn = n // nj
        in_specs.append(pl.BlockSpec((d, tn), lambda i, j: (0, j)))
        args.append(s["w"])
        if s["epi"] == "qk":
            in_specs.append(pl.BlockSpec((1, LANES), lambda i, j: (0, 0)))
            args.append(jnp.tile(s["gain"].reshape(1, A_HD), (1, LANES // A_HD)))
        for o, dt in enumerate(s["out_dtypes"]):
            if o == 0 and s["stack"] is not None:
                layers, layer, buf = s["stack"]
                out_specs.append(pl.BlockSpec((None, tm, tn), lambda i, j, layer=layer: (layer, i, j)))
                out_shape.append(jax.ShapeDtypeStruct((layers, m, n), dt))
                if buf is not None:
                    bufs.append((len(out_shape) - 1, buf))
            else:
                out_specs.append(pl.BlockSpec((tm, tn), lambda i, j: (i, j)))
                out_shape.append(jax.ShapeDtypeStruct((m, n), dt))
    aliases = {}
    for out_idx, buf in bufs:
        aliases[len(args)] = out_idx
        in_specs.append(pl.BlockSpec(memory_space=pl.ANY))
        args.append(buf)
    res = list(pl.pallas_call(
        functools.partial(_proj_kernel, plan=tuple((s["epi"], s["out_scale"], len(s["out_dtypes"])) for s in segs)),
        out_shape=tuple(out_shape),
        grid=(m // tm, nj),
        in_specs=in_specs,
        out_specs=tuple(out_specs),
        scratch_shapes=[pltpu.VMEM((tm, d), BF16)],
        input_output_aliases=aliases,
        compiler_params=_params(("parallel", "arbitrary")),
        name=name,
    )(*args))
    out = []
    for s in segs:
        out.append(res[:len(s["out_dtypes"])])
        res = res[len(s["out_dtypes"]):]
    return out


def _outproj_kernel(a_ref, w_ref, h_ref, o_ref):
    o_ref[...] = h_ref[...] + jnp.dot(a_ref[...], w_ref[...], preferred_element_type=F32)


def _outproj(a, w, h, *, name):
    m, k = a.shape
    n = w.shape[1]
    tm = _tile(m, 1024)
    tn = _tile(n, 1024)
    return pl.pallas_call(
        _outproj_kernel,
        out_shape=jax.ShapeDtypeStruct((m, n), F32),
        grid=(m // tm, n // tn),
        in_specs=[pl.BlockSpec((tm, k), lambda i, j: (i, 0)),
                  pl.BlockSpec((k, tn), lambda i, j: (0, j)),
                  pl.BlockSpec((tm, tn), lambda i, j: (i, j))],
        out_specs=pl.BlockSpec((tm, tn), lambda i, j: (i, j)),
        compiler_params=_params(("parallel", "parallel")),
        name=name,
    )(a, w, h)


def _lambda(lamp_ref, lam_init):
    lp = lamp_ref[...]
    s1 = jnp.sum(lp[0:1, :] * lp[1:2, :], axis=-1, keepdims=True)
    s2 = jnp.sum(lp[2:3, :] * lp[3:4, :], axis=-1, keepdims=True)
    return jnp.exp(s1) - jnp.exp(s2) + lam_init


V_AUG = A_VD + 16


LOOKAHEAD = 5


def _attn_prompt_kernel(q_ref, qn_ref, k_ref, v_ref, g_ref, lamp_ref, sg_ref, o_ref,
                        qpad_ref, vt_ref, m_ref, acc_ref, s_ref, mx_ref, *, tq, tk, lam_init):
    i = pl.program_id(1)
    slot = i % 2

    sw = s_ref.shape[2]
    n_strips = 2 * tq // sw
    la = min(LOOKAHEAD, n_strips - 1)
    order = sorted(range(n_strips), key=lambda c: -((c * sw) % tq))

    def load_queries(src_ref, dst):
        qt = src_ref[...].astype(F32).T
        row = lax.broadcasted_iota(jnp.int32, qt.shape, 0)
        qpad_ref[dst, :, :tq] = jnp.where(row < A_HD, qt, 0.0).astype(BF16)
        qpad_ref[dst, :, tq:] = jnp.where(row >= A_HD, qt, 0.0).astype(BF16)

    def scores(j, c, rows=tk, src=slot):
        kj = k_ref[pl.ds(pl.multiple_of(j * tk, tk), rows), :]
        s = jnp.dot(kj, qpad_ref[src, :, c * sw:(c + 1) * sw], preferred_element_type=F32)
        s_ref[c, :rows] = s
        mx_ref[c] = jnp.max(s, axis=0, keepdims=True)

    @pl.when(i == 0)
    def _():
        aug = lax.broadcasted_iota(jnp.int32, (V_AUG - A_VD, tk), 0)
        ones_row = jnp.where(aug == 0, 1.0, 0.0).astype(BF16)

        def fill(t, carry):
            vj = v_ref[pl.ds(pl.multiple_of(t * tk, tk), tk), :].astype(F32)
            vt_ref[t, :A_VD, :] = vj.T.astype(BF16)
            vt_ref[t, A_VD:, :] = ones_row
            return carry

        lax.fori_loop(0, vt_ref.shape[0], fill, 0)
        load_queries(q_ref, 0)
        for c in order[:la]:
            scores(0, c, src=0)

    m_ref[...] = jnp.full(m_ref.shape, NEG_BIG, F32)
    acc_ref[...] = jnp.zeros(acc_ref.shape, F32)

    def update(j, c, mask, rows=tk):
        cs = slice(c * sw, (c + 1) * sw)
        s = s_ref[c, :rows]
        if mask is None:
            mx = mx_ref[c]
        else:
            s = jnp.where(mask[:rows], s, NEG_BIG)
            mx = jnp.max(s, axis=0, keepdims=True)
        m_old = m_ref[:, cs]
        m_new = jnp.maximum(m_old, mx)
        alpha = jnp.exp2(m_old - m_new)
        p = jnp.exp2(s - m_new).astype(BF16)
        pv = jnp.dot(vt_ref[j][:, :rows], p, preferred_element_type=F32)
        acc_ref[:, cs] = alpha * acc_ref[:, cs] + pv
        m_ref[:, cs] = m_new

    n_sub = tq // tk
    n_full = i * n_sub

    def full_tiles(j0, count):
        for j in range(j0, j0 + count) if isinstance(j0, int) else [j0 + t for t in range(count)]:
            for n, c in enumerate(order):
                update(j, c, None)
                if n + la < n_strips:
                    scores(j, order[n + la])
                else:
                    scores(j + 1, order[n + la - n_strips])

    odd = i % 2

    @pl.when(odd == 1)
    def _():
        full_tiles(0, n_sub)

    def body(jj, carry):
        full_tiles(odd * n_sub + jj * (2 * n_sub), 2 * n_sub)
        return carry

    lax.fori_loop(0, i // 2, body, 0)

    r = lax.broadcasted_iota(jnp.int32, (tk, sw), 0)
    cc = lax.broadcasted_iota(jnp.int32, (tk, sw), 1)
    units = []
    for d in range(n_sub):
        for c in order:
            q_lo, k_lo = (c * sw) % tq, d * tk
            if q_lo + sw <= k_lo:
                continue
            full = q_lo >= k_lo + tk
            rows = min(tk, q_lo + sw - k_lo)
            units.append((d, c, None if full else (k_lo + r) // CHUNK <= (q_lo + cc) // CHUNK, rows))
    assert [u[:2] + u[3:] for u in units[:la]] == [(0, c, tk) for c in order[:la]]
    load_queries(qn_ref, 1 - slot)
    for n, (d, c, mask, rows) in enumerate(units):
        update(n_full + d, c, mask, rows)
        if n + la < len(units):
            nd, nc, _, nrows = units[n + la]
            assert all(u[1] != nc for u in units[n + 1:n + la])
            scores(n_full + nd, nc, nrows)
    for c in order[:la]:
        scores(0, c, src=1 - slot)

    lam = _lambda(lamp_ref, lam_init)
    acc = acc_ref[...]
    inv_l = 1.0 / acc[A_VD:A_VD + 1, :]
    ot = acc[:A_VD, :tq] * inv_l[:, :tq] - lam * (acc[:A_VD, tq:] * inv_l[:, tq:])
    ms = jnp.mean(ot * ot, axis=0, keepdims=True)
    ot = ot * lax.rsqrt(ms + EPS) * (sg_ref[...] * (1.0 - lam_init))
    g = g_ref[...].astype(F32)
    o_ref[...] = (ot.T * (g * _sigmoid(g))).astype(o_ref.dtype)


def _attn_prompt(q, k, v, g, lamp, sub_gain, lam_init, *, name):
    s, w = q.shape
    heads = w // A_VD
    tq = _tile(s, 1024)
    tk = _tile(tq, 512)
    sw = min(MXU_COLS, tq)
    nq = s // tq
    return pl.pallas_call(
        functools.partial(_attn_prompt_kernel, tq=tq, tk=tk, lam_init=lam_init),
        out_shape=jax.ShapeDtypeStruct((s, w), BF16),
        grid=(heads, nq),
        in_specs=[pl.BlockSpec((tq, A_VD), lambda h, i: (i, h)),
                  pl.BlockSpec((tq, A_VD), lambda h, i: (jnp.minimum(i + 1, nq - 1), h)),
                  pl.BlockSpec((s, A_VD), lambda h, i: (0, h)),
                  pl.BlockSpec((s, A_VD), lambda h, i: (0, h)),
                  pl.BlockSpec((tq, A_VD), lambda h, i: (i, h)),
                  pl.BlockSpec((4, A_HD), lambda h, i: (0, 0)),
                  pl.BlockSpec((A_VD, 1), lambda h, i: (0, 0))],
        out_specs=pl.BlockSpec((tq, A_VD), lambda h, i: (i, h)),
        scratch_shapes=[pltpu.VMEM((2, A_VD, 2 * tq), BF16),
                        pltpu.VMEM((s // tk, V_AUG, tk), BF16),
                        pltpu.VMEM((1, 2 * tq), F32),
                        pltpu.VMEM((V_AUG, 2 * tq), F32),
                        pltpu.VMEM((2 * tq // sw, tk, sw), F32),
                        pltpu.VMEM((2 * tq // sw, 1, sw), F32)],
        compiler_params=_params(("parallel", "arbitrary")),
        name=name,
    )(q, q, k, v, g, lamp, sub_gain.reshape(A_VD, 1))


SAMPLE_HEADS_PER_STEP = 4


def _attn_sample_kernel(q_ref, kn_ref, vn_ref, g_ref, kc_ref, vc_ref, lamp_ref, sg_ref, o_ref, *, lam_init, hps):
    lam = _lambda(lamp_ref, lam_init)
    nt = (((1,), (1,)), ((), ()))
    lane = lax.broadcasted_iota(jnp.int32, (q_ref.shape[0], A_VD), 1)
    heads = [slice(h * A_VD, (h + 1) * A_VD) for h in range(hps)]
    scores = []
    for hs in heads:
        q = q_ref[:, hs]
        kc = kc_ref[0, :, hs].astype(BF16)
        kn = kn_ref[:, hs].astype(BF16)
        for half in range(2):
            qh = jnp.where((lane >= A_HD) == bool(half), q, jnp.zeros_like(q))
            scores.append((lax.dot_general(qh, kc, nt, preferred_element_type=F32),
                           lax.dot_general(qh, kn, nt, preferred_element_type=F32)))
    probs = []
    for sc, sn in scores:
        m = jnp.maximum(jnp.max(sc, axis=-1, keepdims=True), jnp.max(sn, axis=-1, keepdims=True))
        pc = jnp.exp2(sc - m)
        pn = jnp.exp2(sn - m)
        inv = 1.0 / (jnp.sum(pc, axis=-1, keepdims=True) + jnp.sum(pn, axis=-1, keepdims=True))
        probs.append((pc * inv, pn * inv))
    outs = []
    for h, hs in enumerate(heads):
        ac = (probs[2 * h][0] - lam * probs[2 * h + 1][0]).astype(BF16)
        an = (probs[2 * h][1] - lam * probs[2 * h + 1][1]).astype(BF16)
        outs.append(jnp.dot(ac, vc_ref[0, :, hs].astype(BF16), preferred_element_type=F32)
                    + jnp.dot(an, vn_ref[:, hs].astype(BF16), preferred_element_type=F32))
    for hs, o in zip(heads, outs):
        ms = jnp.mean(o * o, axis=-1, keepdims=True)
        g = g_ref[:, hs].astype(F32)
        o = o * lax.rsqrt(ms + EPS) * (sg_ref[...] * (1.0 - lam_init))
        o_ref[:, hs] = (o * (g * _sigmoid(g))).astype(o_ref.dtype)


def _attn_sample(q, k, v, g, kc, vc, layer, lamp, sub_gain, lam_init, *, name):
    _, bsz, past, w = kc.shape
    heads = w // A_VD
    ln = q.shape[0] // bsz
    assert past % CHUNK == 0 and ln <= CHUNK
    hps = SAMPLE_HEADS_PER_STEP
    assert heads % hps == 0
    row = lambda b, h: (b, h)
    cache = lambda b, h: (layer, b, 0, h)
    return pl.pallas_call(
        functools.partial(_attn_sample_kernel, lam_init=lam_init, hps=hps),
        out_shape=jax.ShapeDtypeStruct(q.shape, BF16),
        grid=(bsz, heads // hps),
        in_specs=[pl.BlockSpec((ln, hps * A_VD), row),
                  pl.BlockSpec((ln, hps * A_VD), row),
                  pl.BlockSpec((ln, hps * A_VD), row),
                  pl.BlockSpec((ln, hps * A_VD), row),
                  pl.BlockSpec((None, 1, past, hps * A_VD), cache),
                  pl.BlockSpec((None, 1, past, hps * A_VD), cache),
                  pl.BlockSpec((4, A_HD), lambda b, h: (0, 0)),
                  pl.BlockSpec((1, A_VD), lambda b, h: (0, 0))],
        out_specs=pl.BlockSpec((ln, hps * A_VD), row),
        compiler_params=_params(("parallel", "parallel")),
        name=name,
    )(q, k, v, g, kc, vc, lamp, sub_gain.reshape(1, A_VD))


RET_HEADS_PER_STEP = 2


def _retention_kernel(q_ref, k_ref, v_ref, g_ref, s0_ref, dec_ref, cd_ref, kd_ref, gl_ref,
                      o_ref, sout_ref, st_ref, *, lc, hps):
    c = pl.program_id(2)

    @pl.when(c == 0)
    def _():
        st_ref[...] = s0_ref[0]

    hd = range(hps)
    qs = [q_ref[:, h * R_DK:(h + 1) * R_DK] for h in hd]
    ks = [k_ref[:, h * R_DK:(h + 1) * R_DK] for h in hd]
    vs = [v_ref[:, h * R_DV:(h + 1) * R_DV] for h in hd]
    sts = [st_ref[h] for h in hd]
    nt = (((1,), (1,)), ((), ()))
    inner = [lax.dot_general(qs[h], ks[h], nt, preferred_element_type=F32) for h in hd]
    cross = [jnp.dot(qs[h], sts[h].astype(BF16), preferred_element_type=F32) for h in hd]
    kdec = []
    for h in hd:
        kd = ks[h].astype(F32) * jnp.concatenate([kd_ref[h]] * (R_DK // LANES), axis=1)
        if lc < LANES:
            kd = jnp.concatenate([kd, jnp.zeros((LANES - lc, R_DK), F32)], axis=0)
        kdec.append(kd.T.astype(BF16))
    inner = [(inner[h] * dec_ref[h]).astype(BF16) for h in hd]
    o = [jnp.dot(inner[h], vs[h], preferred_element_type=F32) for h in hd]
    upd = []
    for h in hd:
        vv = vs[h]
        if lc < LANES:
            vv = jnp.concatenate([vv, jnp.zeros((LANES - lc, R_DV), BF16)], axis=0)
        upd.append(jnp.dot(kdec[h], vv, preferred_element_type=F32))
    for h in hd:
        cd = jnp.concatenate([cd_ref[h]] * (R_DV // LANES), axis=1)
        oh = o[h] + cross[h] * cd
        ms = jnp.mean(oh * oh, axis=-1, keepdims=True)
        g = g_ref[:, h * R_DV:(h + 1) * R_DV].astype(F32)
        o_ref[:, h * R_DV:(h + 1) * R_DV] = (oh * lax.rsqrt(ms + EPS) * (g * _sigmoid(g))).astype(o_ref.dtype)
    st_new = []
    for h in hd:
        gl = jnp.concatenate([gl_ref[h]] * (R_DV // LANES), axis=1)
        st_new.append(sts[h] * gl + upd[h])
        st_ref[h] = st_new[h]

    @pl.when(c == pl.num_programs(2) - 1)
    def _():
        for h in hd:
            sout_ref[0, h] = st_new[h]


def _retention(q, k, vg, s0, lc, *, name):
    bsz, heads = s0.shape[:2]
    t = q.shape[0] // bsz
    nc = t // lc
    lg = jnp.log1p(-(2.0 ** (-5.0 - jnp.arange(heads, dtype=F32))))
    idx = jnp.arange(lc, dtype=F32)
    diff = idx[:, None] - idx[None, :]
    dec = jnp.where(diff >= 0, jnp.exp(lg[:, None, None] * jnp.maximum(diff, 0.0)), 0.0)
    cd = jnp.broadcast_to(jnp.exp(lg[:, None] * (idx[None, :] + 1.0))[:, :, None], (heads, lc, LANES))
    kd = jnp.broadcast_to(jnp.exp(lg[:, None] * (lc - 1.0 - idx[None, :]))[:, :, None], (heads, lc, LANES))
    gl = jnp.broadcast_to(jnp.exp(lg * lc)[:, None, None], (heads, 1, LANES))
    hps = RET_HEADS_PER_STEP
    assert heads % hps == 0
    rows = lambda b, h, c: (b * nc + c, h)
    gate = lambda b, h, c: (b * nc + c, heads // hps + h)
    tab = lambda b, h, c: (h, 0, 0)
    state = lambda b, h, c: (b, h, 0, 0)
    return pl.pallas_call(
        functools.partial(_retention_kernel, lc=lc, hps=hps),
        out_shape=(jax.ShapeDtypeStruct((vg.shape[0], heads * R_DV), BF16),
                   jax.ShapeDtypeStruct(s0.shape, F32)),
        grid=(bsz, heads // hps, nc),
        in_specs=[pl.BlockSpec((lc, hps * R_DK), rows),
                  pl.BlockSpec((lc, hps * R_DK), rows),
                  pl.BlockSpec((lc, hps * R_DV), rows),
                  pl.BlockSpec((lc, hps * R_DV), gate),
                  pl.BlockSpec((1, hps, R_DK, R_DV), state),
                  pl.BlockSpec((hps, lc, lc), tab),
                  pl.BlockSpec((hps, lc, LANES), tab),
                  pl.BlockSpec((hps, lc, LANES), tab),
                  pl.BlockSpec((hps, 1, LANES), tab)],
        out_specs=(pl.BlockSpec((lc, hps * R_DV), rows),
                   pl.BlockSpec((1, hps, R_DK, R_DV), state)),
        scratch_shapes=[pltpu.VMEM((hps, R_DK, R_DV), F32)],
        compiler_params=_params(("parallel", "parallel", "arbitrary")),
        name=name,
    )(q, k, vg, vg, s0, dec, cd, kd, gl)


def _cmlp_kernel(u_ref, g_ref, v_ref, vg_ref, w_ref, b_ref, *out_refs, emit_v):
    if emit_v:
        a_ref, vn_ref = out_refs
    else:
        (a_ref,) = out_refs
    va = _gelu(v_ref[...])
    ms = jnp.mean(va * va, axis=-1, keepdims=True)
    vn = va * lax.rsqrt(ms + EPS) * vg_ref[...]
    if emit_v:
        vn_ref[...] = vn
    vb = vn.astype(BF16)
    gd = vn.shape[1] // M_GROUPS
    for grp in range(M_GROUPS):
        sl = slice(grp * gd, (grp + 1) * gd)
        bias = jnp.concatenate([b_ref[grp]] * (gd // LANES), axis=1)
        mix = jnp.dot(w_ref[grp], vb[:, sl], preferred_element_type=F32) + bias
        g = g_ref[:, sl].astype(F32)
        a_ref[:, sl] = (_gelu(u_ref[:, sl].astype(F32)) * mix * (g * _sigmoid(g))).astype(a_ref.dtype)


def _cmlp(ug, v, v_gain, wmix, bmix, *, emit_v, name):
    m, w = v.shape
    t = wmix.shape[1]
    out_shape = [jax.ShapeDtypeStruct((m, w), BF16)]
    out_specs = [pl.BlockSpec((t, w), lambda i: (i, 0))]
    if emit_v:
        out_shape.append(jax.ShapeDtypeStruct((m, w), F32))
        out_specs.append(pl.BlockSpec((t, w), lambda i: (i, 0)))
    res = pl.pallas_call(
        functools.partial(_cmlp_kernel, emit_v=emit_v),
        out_shape=tuple(out_shape),
        grid=(m // t,),
        in_specs=[pl.BlockSpec((t, w), lambda i: (i, 0)),
                  pl.BlockSpec((t, w), lambda i: (i, 1)),
                  pl.BlockSpec((t, w), lambda i: (i, 0)),
                  pl.BlockSpec((1, w), lambda i: (0, 0)),
                  pl.BlockSpec((M_GROUPS, t, t), lambda i: (0, 0, 0)),
                  pl.BlockSpec((M_GROUPS, t, LANES), lambda i: (0, 0, 0))],
        out_specs=tuple(out_specs),
        compiler_params=_params(("parallel",)),
        name=name,
    )(ug, ug, v, v_gain.reshape(1, w), wmix, bmix)
    return res if emit_v else res[0]


def _rope_tables(pos, d, signed):
    inv = ROPE_THETA ** (-jnp.arange(0, d, 2, dtype=F32) / d)
    ang = pos.astype(F32)[:, None] * inv[None, :]
    cos, sin = jnp.cos(ang), jnp.sin(ang)
    reps = LANES // (d // 2)
    if signed:
        return (jnp.tile(cos, (1, reps)),
                jnp.tile(jnp.concatenate([-sin, sin], axis=1), (1, reps // 2)))
    return jnp.tile(cos, (1, reps)), jnp.tile(sin, (1, reps))


def _diff_attn_layer(h, pos, nw, w_in, w_out, q_gain, k_gain, lamp, sub_gain, lam_init,
                     layers, layer, kbuf, vbuf, cache, tag):
    w = w_out.shape[0]
    cos, sin = _rope_tables(pos, A_HD, signed=True)
    wq, wk, wv, wg = (w_in[:, i * w:(i + 1) * w].astype(BF16) for i in range(4))
    (q,), (k, kb), (v, vb), (g,) = _proj(h, nw, [
        _seg(wq, (BF16,), "qk", (A_HD ** -0.5) * LOG2E, q_gain),
        _seg(wk, (F32, BF16), "qk", gain=k_gain, stack=(layers, layer, kbuf)),
        _seg(wv, (F32, BF16), stack=(layers, layer, vbuf)),
        _seg(wg, (BF16,))], cos=cos, sin=sin, name=f"{tag}_proj")
    if cache is None:
        o = _attn_prompt(q, kb, vb, g, lamp, sub_gain, lam_init, name=f"{tag}_attn")
    else:
        kc, vc = cache
        o = _attn_sample(q, kb, vb, g, kc, vc, layer, lamp, sub_gain, lam_init, name=f"{tag}_attn")
    return _outproj(o, w_out.astype(BF16), h, name=f"{tag}_out"), k, v


def _retention_layer(h, pos, nw, w_in, w_out, s0, lc, tag):
    heads = s0.shape[1]
    qk_w, v_w = heads * R_DK, heads * R_DV
    cos, sin = _rope_tables(pos, R_DK, signed=False)
    wq = w_in[:, :qk_w].astype(BF16)
    wk = w_in[:, qk_w:2 * qk_w].astype(BF16)
    assert w_in.shape[1] == 2 * qk_w + 2 * v_w
    wvg = w_in[:, 2 * qk_w:].astype(BF16)
    (q,), (k,), (vg,) = _proj(h, nw, [
        _seg(wq, (BF16,), "rope256"),
        _seg(wk, (BF16,), "rope256", R_DK ** -0.5),
        _seg(wvg, (BF16,))], cos=cos, sin=sin, name=f"{tag}_proj")
    o, s_new = _retention(q, k, vg, s0, lc, name=f"{tag}_ret")
    return _outproj(o, w_out.astype(BF16), h, name=f"{tag}_out"), s_new


def _cmlp_layer(h, nw, w_in, w_out, v_gain, wmix, bmix, emit_v, tag):
    w = w_out.shape[0]
    wu, wv, wg = (w_in[:, i * w:(i + 1) * w].astype(BF16) for i in range(3))
    (ug,), (v,) = _proj(h, nw, [_seg(jnp.concatenate([wu, wg], axis=1), (BF16,)), _seg(wv, (F32,))],
                        name=f"{tag}_proj")
    res = _cmlp(ug, v, v_gain, wmix, bmix, emit_v=emit_v, name=f"{tag}_mix")
    a, vn = res if emit_v else (res, None)
    return _outproj(a, w_out.astype(BF16), h, name=f"{tag}_out"), vn


def _mix_tables(w_s, b_s, chunk_len):
    groups = w_s.shape[0]
    wl = jnp.tril(w_s[:, :chunk_len, :chunk_len])
    reps = M_CHUNK // chunk_len
    eye = jnp.eye(reps, dtype=F32)
    wt = jnp.einsum("ab,gij->gaibj", eye, wl).reshape(groups, M_CHUNK, M_CHUNK)
    bt = jnp.tile(b_s[:, :chunk_len], (1, reps))
    return wt.astype(BF16), jnp.broadcast_to(bt[:, :, None], (groups, M_CHUNK, LANES))


def kernel(x_prompt, x_sample, cache_k_attn, cache_v_attn, state_ret, norm_w, a_w_in, a_w_out, a_q_gain, a_k_gain, a_lam_q1, a_lam_k1, a_lam_q2, a_lam_k2, a_sub_gain, r_w_in, r_w_out, c_w_in, c_w_out, c_v_gain, c_w_s, c_b_s):
    batch, s_len, d = x_prompt.shape
    dec_b, dec_len, _ = x_sample.shape
    past = cache_k_attn.shape[2]
    depth = norm_w.shape[0]
    assert batch == 1 and M_CHUNK % dec_len == 0 and s_len % M_CHUNK == 0

    hp = x_prompt.reshape(s_len, d)
    hs = x_sample.reshape(dec_b * dec_len, d)
    pos_p = jnp.arange(s_len, dtype=jnp.int32)
    pos_s = jnp.tile(past + jnp.arange(dec_len, dtype=jnp.int32), dec_b)

    n_a = a_w_in.shape[0]
    aw = a_w_out.shape[1]
    cache = (cache_k_attn.reshape(n_a, dec_b, past, aw), cache_v_attn.reshape(n_a, dec_b, past, aw))
    kp = vp = kn = vn = None
    sp_l, ss_l, vm_l = [], [], []
    for i in range(depth):
        kind, j = i % N_MIXERS, i // N_MIXERS
        if kind == 0:
            lam_init = 0.8 - 0.6 * math.exp(-0.3 * i)
            lamp = jnp.stack([a_lam_q1[j], a_lam_k1[j], a_lam_q2[j], a_lam_k2[j]])
            args = (norm_w[i], a_w_in[j], a_w_out[j], a_q_gain[j], a_k_gain[j], lamp, a_sub_gain[j], lam_init)
            hp, kp, vp = _diff_attn_layer(hp, pos_p, *args, n_a, j, kp, vp, None, f"l{i}p")
            hp, hs = lax.optimization_barrier((hp, hs))
            hs, kn, vn = _diff_attn_layer(hs, pos_s, *args, n_a, j, kn, vn, cache, f"l{i}s")
        elif kind == 1:
            heads = state_ret.shape[2]
            s0 = jnp.zeros((batch, heads, R_DK, R_DV), F32)
            hp, st_p = _retention_layer(hp, pos_p, norm_w[i], r_w_in[j], r_w_out[j], s0,
                                        _tile(s_len, 256), f"l{i}p")
            hs, st_s = _retention_layer(hs, pos_s, norm_w[i], r_w_in[j], r_w_out[j],
                                        state_ret[j].astype(F32), dec_len, f"l{i}s")
            sp_l.append(st_p)
            ss_l.append(st_s)
        else:
            wp, bp = _mix_tables(c_w_s[j], c_b_s[j], M_CHUNK)
            ws, bs = _mix_tables(c_w_s[j], c_b_s[j], dec_len)
            hp, _ = _cmlp_layer(hp, norm_w[i], c_w_in[j], c_w_out[j], c_v_gain[j], wp, bp, False, f"l{i}p")
            hs, v_s = _cmlp_layer(hs, norm_w[i], c_w_in[j], c_w_out[j], c_v_gain[j], ws, bs, True, f"l{i}s")
            vm_l.append(v_s.reshape(dec_b, dec_len, -1))

    return (hp.reshape(batch, s_len, d), hs.reshape(dec_b, dec_len, d),
            kp.reshape(n_a, batch, s_len, aw // A_HD, A_HD), vp.reshape(n_a, batch, s_len, aw // A_VD, A_VD),
            kn.reshape(n_a, dec_b, dec_len, aw // A_HD, A_HD), vn.reshape(n_a, dec_b, dec_len, aw // A_VD, A_VD),
            jnp.stack(sp_l), jnp.stack(ss_l), jnp.stack(vm_l))
```

```python
import functools
import math

import jax
import jax.numpy as jnp
from jax import lax
from jax.experimental import pallas as pl
from jax.experimental.pallas import tpu as pltpu

F32 = jnp.float32
BF16 = jnp.bfloat16

EPS = 1e-6
CHUNK = 64
ROPE_THETA = 10000.0
N_MIXERS = 3
A_HD = 64
A_VD = 2 * A_HD
R_DK = 256
R_DV = 2 * R_DK
M_GROUPS = 8
M_CHUNK = 128
LOG2E = 1.4426950408889634

LANES = 128
MXU_COLS = 256
VMEM_LIMIT = 56 * 1024 * 1024
NEG_BIG = -1e30


def _tile(n, pref):
    if n <= pref:
        return n
    t = pref
    while t >= 8:
        if n % t == 0:
            return t
        t -= 8
    return n


def _params(sem):
    return pltpu.CompilerParams(dimension_semantics=sem, vmem_limit_bytes=VMEM_LIMIT)


def _gelu(x):
    return 0.5 * x * (1.0 + jnp.tanh(0.7978845608028654 * (x + 0.044715 * (x * x * x))))


def _sigmoid(x):
    return 1.0 / (1.0 + jnp.exp(-x))


def _proj_kernel(*refs, plan):
    xn_ref = refs[-1]
    it = iter(refs)
    x_ref, nw_ref = next(it), next(it)
    epis = [p[0] for p in plan]
    cos_ref, sin_ref = (next(it), next(it)) if any(e != "none" for e in epis) else (None, None)
    gsum_ref = next(it) if "qk" in epis else None
    w_refs, gain_refs = [], []
    for e in epis:
        w_refs.append(next(it))
        gain_refs.append(next(it) if e == "qk" else None)
    outs = list(refs[len(refs) - 1 - sum(p[2] for p in plan):-1])
    o_refs = []
    for p in plan:
        o_refs.append(outs[:p[2]])
        outs = outs[p[2]:]

    def store(seg, cols, val):
        scale = plan[seg][1]
        if scale != 1.0:
            val = val * scale
        for o_ref in o_refs[seg]:
            o_ref[:, cols] = val.astype(o_ref.dtype)

    @pl.when(pl.program_id(1) == 0)
    def _():
        x = x_ref[...]
        ms = jnp.mean(x * x, axis=-1, keepdims=True)
        xn_ref[...] = (x * lax.rsqrt(ms + EPS) * nw_ref[...]).astype(BF16)

    qk = [s for s, e in enumerate(epis) if e == "qk"]
    rest = [s for s, e in enumerate(epis) if e == "rope256"] + [s for s, e in enumerate(epis) if e == "none"]
    slabs = {s: [slice(c * MXU_COLS, (c + 1) * MXU_COLS) for c in range(w_refs[s].shape[1] // MXU_COLS)]
             for s in qk}
    zq = {(s, c): jnp.dot(xn_ref[...], w_refs[s][:, sl], preferred_element_type=F32)
          for s in qk for c, sl in enumerate(slabs[s])}
    sq = {key: jnp.dot((z * z).astype(BF16), gsum_ref[...], preferred_element_type=F32) for key, z in zq.items()}
    zr = {s: jnp.dot(xn_ref[...], w_refs[s][...], preferred_element_type=F32) for s in rest}

    if qk:
        cos = jnp.concatenate([cos_ref[...]] * 2, axis=1)
        sin = jnp.concatenate([sin_ref[...]] * 2, axis=1)
        lane = lax.broadcasted_iota(jnp.int32, (xn_ref.shape[0], MXU_COLS), 1)
        first_half = (lane % A_HD) < (A_HD // 2)
        for (s, c), z in zq.items():
            gain = jnp.concatenate([gain_refs[s][...]] * 2, axis=1)
            zn = z * lax.rsqrt(sq[(s, c)] * (1.0 / A_HD) + EPS) * gain
            partner = jnp.where(first_half,
                                pltpu.roll(zn, MXU_COLS - A_HD // 2, 1),
                                pltpu.roll(zn, A_HD // 2, 1))
            store(s, slabs[s][c], zn * cos + partner * sin)
    for s in rest:
        z = zr[s]
        if epis[s] == "rope256":
            cos = cos_ref[...]
            sin = sin_ref[...]
            for c in range(z.shape[1] // R_DK):
                lo, hi = slice(c * R_DK, c * R_DK + LANES), slice(c * R_DK + LANES, (c + 1) * R_DK)
                x1, x2 = z[:, lo], z[:, hi]
                store(s, lo, x1 * cos - x2 * sin)
                store(s, hi, x2 * cos + x1 * sin)
        else:
            store(s, slice(None), z)


def _seg(w, out_dtypes, epi="none", out_scale=1.0, gain=None, stack=None):
    return dict(w=w, out_dtypes=out_dtypes, epi=epi, out_scale=out_scale, gain=gain, stack=stack)


PROJ_VMEM_BUDGET = 48 * 1024 * 1024


def _proj(x, nw, segs, *, cos=None, sin=None, name):
    m, d = x.shape
    tm = _tile(m, 1024)
    epis = [s["epi"] for s in segs]
    assert not ("qk" in epis and "rope256" in epis)
    unit = {"qk": MXU_COLS, "rope256": R_DK, "none": LANES}

    def vmem_bytes(nj):
        total = 2 * tm * d * 4 + tm * d * 2
        for s in segs:
            tn = s["w"].shape[1] // nj
            total += 2 * d * tn * 2 + tm * tn * 4
            total += sum(2 * tm * tn * jnp.dtype(dt).itemsize for dt in s["out_dtypes"])
        return total

    nj = next(c for c in (1, 2, 4, 8, 16, 32)
              if all(s["w"].shape[1] % (c * unit[s["epi"]]) == 0 for s in segs) and vmem_bytes(c) <= PROJ_VMEM_BUDGET)
    in_specs = [pl.BlockSpec((tm, d), lambda i, j: (i, 0)),
                pl.BlockSpec((1, d), lambda i, j: (0, 0))]
    args = [x, nw.reshape(1, d)]
    if any(e != "none" for e in epis):
        in_specs += [pl.BlockSpec((tm, LANES), lambda i, j: (i, 0))] * 2
        args += [cos, sin]
    if "qk" in epis:
        gidx = jnp.arange(MXU_COLS) // A_HD
        in_specs.append(pl.BlockSpec((MXU_COLS, MXU_COLS), lambda i, j: (0, 0)))
        args.append((gidx[:, None] == gidx[None, :]).astype(BF16))
    out_shape, out_specs, bufs = [], [], []
    for s in segs:
        n = s["w"].shape[1]
        tn = n // nj
        in_specs.append(pl.BlockSpec((d, tn), lambda i, j: (0, j)))
        args.append(s["w"])
        if s["epi"] == "qk":
            in_specs.append(pl.BlockSpec((1, LANES), lambda i, j: (0, 0)))
            args.append(jnp.tile(s["gain"].reshape(1, A_HD), (1, LANES // A_HD)))
        for o, dt in enumerate(s["out_dtypes"]):
            if o == 0 and s["stack"] is not None:
                layers, layer, buf = s["stack"]
                out_specs.append(pl.BlockSpec((None, tm, tn), lambda i, j, layer=layer: (layer, i, j)))
                out_shape.append(jax.ShapeDtypeStruct((layers, m, n), dt))
                if buf is not None:
                    bufs.append((len(out_shape) - 1, buf))
            else:
                out_specs.append(pl.BlockSpec((tm, tn), lambda i, j: (i, j)))
                out_shape.append(jax.ShapeDtypeStruct((m, n), dt))
    aliases = {}
    for out_idx, buf in bufs:
        aliases[len(args)] = out_idx
        in_specs.append(pl.BlockSpec(memory_space=pl.ANY))
        args.append(buf)
    res = list(pl.pallas_call(
        functools.partial(_proj_kernel, plan=tuple((s["epi"], s["out_scale"], len(s["out_dtypes"])) for s in segs)),
        out_shape=tuple(out_shape),
        grid=(m // tm, nj),
        in_specs=in_specs,
        out_specs=tuple(out_specs),
        scratch_shapes=[pltpu.VMEM((tm, d), BF16)],
        input_output_aliases=aliases,
        compiler_params=_params(("parallel", "arbitrary")),
        name=name,
    )(*args))
    out = []
    for s in segs:
        out.append(res[:len(s["out_dtypes"])])
        res = res[len(s["out_dtypes"]):]
    return out


def _outproj_kernel(a_ref, w_ref, h_ref, o_ref):
    o_ref[...] = h_ref[...] + jnp.dot(a_ref[...], w_ref[...], preferred_element_type=F32)


def _outproj(a, w, h, *, name):
    m, k = a.shape
    n = w.shape[1]
    tm = _tile(m, 1024)
    tn = _tile(n, 1024)
    return pl.pallas_call(
        _outproj_kernel,
        out_shape=jax.ShapeDtypeStruct((m, n), F32),
        grid=(m // tm, n // tn),
        in_specs=[pl.BlockSpec((tm, k), lambda i, j: (i, 0)),
                  pl.BlockSpec((k, tn), lambda i, j: (0, j)),
                  pl.BlockSpec((tm, tn), lambda i, j: (i, j))],
        out_specs=pl.BlockSpec((tm, tn), lambda i, j: (i, j)),
        compiler_params=_params(("parallel", "parallel")),
        name=name,
    )(a, w, h)


def _lambda(lamp_ref, lam_init):
    lp = lamp_ref[...]
    s1 = jnp.sum(lp[0:1, :] * lp[1:2, :], axis=-1, keepdims=True)
    s2 = jnp.sum(lp[2:3, :] * lp[3:4, :], axis=-1, keepdims=True)
    return jnp.exp(s1) - jnp.exp(s2) + lam_init


V_AUG = A_VD + 16


LOOKAHEAD = 5


def _attn_prompt_kernel(q_ref, qn_ref, k_ref, v_ref, g_ref, lamp_ref, sg_ref, o_ref,
                        qpad_ref, vt_ref, m_ref, acc_ref, s_ref, mx_ref, *, tq, tk, lam_init):
    i = pl.program_id(1)
    slot = i % 2

    sw = s_ref.shape[2]
    n_strips = 2 * tq // sw
    la = min(LOOKAHEAD, n_strips - 1)
    order = sorted(range(n_strips), key=lambda c: -((c * sw) % tq))

    def load_queries(src_ref, dst):
        qt = src_ref[...].astype(F32).T
        row = lax.broadcasted_iota(jnp.int32, qt.shape, 0)
        qpad_ref[dst, :, :tq] = jnp.where(row < A_HD, qt, 0.0).astype(BF16)
        qpad_ref[dst, :, tq:] = jnp.where(row >= A_HD, qt, 0.0).astype(BF16)

    def scores(j, c, rows=tk, src=slot):
        kj = k_ref[pl.ds(pl.multiple_of(j * tk, tk), rows), :]
        s = jnp.dot(kj, qpad_ref[src, :, c * sw:(c + 1) * sw], preferred_element_type=F32)
        s_ref[c, :rows] = s
        mx_ref[c] = jnp.max(s, axis=0, keepdims=True)

    @pl.when(i == 0)
    def _():
        aug = lax.broadcasted_iota(jnp.int32, (V_AUG - A_VD, tk), 0)
        ones_row = jnp.where(aug == 0, 1.0, 0.0).astype(BF16)

        def fill(t, carry):
            vj = v_ref[pl.ds(pl.multiple_of(t * tk, tk), tk), :].astype(F32)
            vt_ref[t, :A_VD, :] = vj.T.astype(BF16)
            vt_ref[t, A_VD:, :] = ones_row
            return carry

        lax.fori_loop(0, vt_ref.shape[0], fill, 0)
        load_queries(q_ref, 0)
        for c in order[:la]:
            scores(0, c, src=0)

    m_ref[...] = jnp.full(m_ref.shape, NEG_BIG, F32)
    acc_ref[...] = jnp.zeros(acc_ref.shape, F32)

    def update(j, c, mask, rows=tk):
        cs = slice(c * sw, (c + 1) * sw)
        s = s_ref[c, :rows]
        if mask is None:
            mx = mx_ref[c]
        else:
            s = jnp.where(mask[:rows], s, NEG_BIG)
            mx = jnp.max(s, axis=0, keepdims=True)
        m_old = m_ref[:, cs]
        m_new = jnp.maximum(m_old, mx)
        alpha = jnp.exp2(m_old - m_new)
        p = jnp.exp2(s - m_new).astype(BF16)
        pv = jnp.dot(vt_ref[j][:, :rows], p, preferred_element_type=F32)
        acc_ref[:, cs] = alpha * acc_ref[:, cs] + pv
        m_ref[:, cs] = m_new

    n_sub = tq // tk
    n_full = i * n_sub

    def full_tiles(j0, count):
        for j in range(j0, j0 + count) if isinstance(j0, int) else [j0 + t for t in range(count)]:
            for n, c in enumerate(order):
                update(j, c, None)
                if n + la < n_strips:
                    scores(j, order[n + la])
                else:
                    scores(j + 1, order[n + la - n_strips])

    odd = i % 2

    @pl.when(odd == 1)
    def _():
        full_tiles(0, n_sub)

    def body(jj, carry):
        full_tiles(odd * n_sub + jj * (2 * n_sub), 2 * n_sub)
        return carry

    lax.fori_loop(0, i // 2, body, 0)

    r = lax.broadcasted_iota(jnp.int32, (tk, sw), 0)
    cc = lax.broadcasted_iota(jnp.int32, (tk, sw), 1)
    units = []
    for d in range(n_sub):
        for c in order:
            q_lo, k_lo = (c * sw) % tq, d * tk
            if q_lo + sw <= k_lo:
                continue
            full = q_lo >= k_lo + tk
            rows = min(tk, q_lo + sw - k_lo)
            units.append((d, c, None if full else (k_lo + r) // CHUNK <= (q_lo + cc) // CHUNK, rows))
    assert [u[:2] + u[3:] for u in units[:la]] == [(0, c, tk) for c in order[:la]]
    load_queries(qn_ref, 1 - slot)
    for n, (d, c, mask, rows) in enumerate(units):
        update(n_full + d, c, mask, rows)
        if n + la < len(units):
            nd, nc, _, nrows = units[n + la]
            assert all(u[1] != nc for u in units[n + 1:n + la])
            scores(n_full + nd, nc, nrows)
    for c in order[:la]:
        scores(0, c, src=1 - slot)

    lam = _lambda(lamp_ref, lam_init)
    acc = acc_ref[...]
    inv_l = 1.0 / acc[A_VD:A_VD + 1, :]
    ot = acc[:A_VD, :tq] * inv_l[:, :tq] - lam * (acc[:A_VD, tq:] * inv_l[:, tq:])
    ms = jnp.mean(ot * ot, axis=0, keepdims=True)
    ot = ot * lax.rsqrt(ms + EPS) * (sg_ref[...] * (1.0 - lam_init))
    g = g_ref[...].astype(F32)
    o_ref[...] = (ot.T * (g * _sigmoid(g))).astype(o_ref.dtype)


def _attn_prompt(q, k, v, g, lamp, sub_gain, lam_init, *, name):
    s, w = q.shape
    heads = w // A_VD
    tq = _tile(s, 1024)
    tk = _tile(tq, 512)
    sw = min(MXU_COLS, tq)
    nq = s // tq
    return pl.pallas_call(
        functools.partial(_attn_prompt_kernel, tq=tq, tk=tk, lam_init=lam_init),
        out_shape=jax.ShapeDtypeStruct((s, w), BF16),
        grid=(heads, nq),
        in_specs=[pl.BlockSpec((tq, A_VD), lambda h, i: (i, h)),
                  pl.BlockSpec((tq, A_VD), lambda h, i: (jnp.minimum(i + 1, nq - 1), h)),
                  pl.BlockSpec((s, A_VD), lambda h, i: (0, h)),
                  pl.BlockSpec((s, A_VD), lambda h, i: (0, h)),
                  pl.BlockSpec((tq, A_VD), lambda h, i: (i, h)),
                  pl.BlockSpec((4, A_HD), lambda h, i: (0, 0)),
                  pl.BlockSpec((A_VD, 1), lambda h, i: (0, 0))],
        out_specs=pl.BlockSpec((tq, A_VD), lambda h, i: (i, h)),
        scratch_shapes=[pltpu.VMEM((2, A_VD, 2 * tq), BF16),
                        pltpu.VMEM((s // tk, V_AUG, tk), BF16),
                        pltpu.VMEM((1, 2 * tq), F32),
                        pltpu.VMEM((V_AUG, 2 * tq), F32),
                        pltpu.VMEM((2 * tq // sw, tk, sw), F32),
                        pltpu.VMEM((2 * tq // sw, 1, sw), F32)],
        compiler_params=_params(("parallel", "arbitrary")),
        name=name,
    )(q, q, k, v, g, lamp, sub_gain.reshape(A_VD, 1))


SAMPLE_HEADS_PER_STEP = 4


def _attn_sample_kernel(q_ref, kn_ref, vn_ref, g_ref, kc_ref, vc_ref, lamp_ref, sg_ref, o_ref, *, lam_init, hps):
    lam = _lambda(lamp_ref, lam_init)
    nt = (((1,), (1,)), ((), ()))
    lane = lax.broadcasted_iota(jnp.int32, (q_ref.shape[0], A_VD), 1)
    heads = [slice(h * A_VD, (h + 1) * A_VD) for h in range(hps)]
    scores = []
    for hs in heads:
        q = q_ref[:, hs]
        kc = kc_ref[0, :, hs].astype(BF16)
        kn = kn_ref[:, hs].astype(BF16)
        for half in range(2):
            qh = jnp.where((lane >= A_HD) == bool(half), q, jnp.zeros_like(q))
            scores.append((lax.dot_general(qh, kc, nt, preferred_element_type=F32),
                           lax.dot_general(qh, kn, nt, preferred_element_type=F32)))
    probs = []
    for sc, sn in scores:
        m = jnp.maximum(jnp.max(sc, axis=-1, keepdims=True), jnp.max(sn, axis=-1, keepdims=True))
        pc = jnp.exp2(sc - m)
        pn = jnp.exp2(sn - m)
        inv = 1.0 / (jnp.sum(pc, axis=-1, keepdims=True) + jnp.sum(pn, axis=-1, keepdims=True))
        probs.append((pc * inv, pn * inv))
    outs = []
    for h, hs in enumerate(heads):
        ac = (probs[2 * h][0] - lam * probs[2 * h + 1][0]).astype(BF16)
        an = (probs[2 * h][1] - lam * probs[2 * h + 1][1]).astype(BF16)
        outs.append(jnp.dot(ac, vc_ref[0, :, hs].astype(BF16), preferred_element_type=F32)
                    + jnp.dot(an, vn_ref[:, hs].astype(BF16), preferred_element_type=F32))
    for hs, o in zip(heads, outs):
        ms = jnp.mean(o * o, axis=-1, keepdims=True)
        g = g_ref[:, hs].astype(F32)
        o = o * lax.rsqrt(ms + EPS) * (sg_ref[...] * (1.0 - lam_init))
        o_ref[:, hs] = (o * (g * _sigmoid(g))).astype(o_ref.dtype)


def _attn_sample(q, k, v, g, kc, vc, layer, lamp, sub_gain, lam_init, *, name):
    _, bsz, past, w = kc.shape
    heads = w // A_VD
    ln = q.shape[0] // bsz
    assert past % CHUNK == 0 and ln <= CHUNK
    hps = SAMPLE_HEADS_PER_STEP
    assert heads % hps == 0
    row = lambda b, h: (b, h)
    cache = lambda b, h: (layer, b, 0, h)
    return pl.pallas_call(
        functools.partial(_attn_sample_kernel, lam_init=lam_init, hps=hps),
        out_shape=jax.ShapeDtypeStruct(q.shape, BF16),
        grid=(bsz, heads // hps),
        in_specs=[pl.BlockSpec((ln, hps * A_VD), row),
                  pl.BlockSpec((ln, hps * A_VD), row),
                  pl.BlockSpec((ln, hps * A_VD), row),
                  pl.BlockSpec((ln, hps * A_VD), row),
                  pl.BlockSpec((None, 1, past, hps * A_VD), cache),
                  pl.BlockSpec((None, 1, past, hps * A_VD), cache),
                  pl.BlockSpec((4, A_HD), lambda b, h: (0, 0)),
                  pl.BlockSpec((1, A_VD), lambda b, h: (0, 0))],
        out_specs=pl.BlockSpec((ln, hps * A_VD), row),
        compiler_params=_params(("parallel", "parallel")),
        name=name,
    )(q, k, v, g, kc, vc, lamp, sub_gain.reshape(1, A_VD))


RET_HEADS_PER_STEP = 8


def _retention_kernel(q_ref, k_ref, v_ref, g_ref, s0_ref, dec_ref, cd_ref, kd_ref, gl_ref,
                      o_ref, sout_ref, st_ref, *, lc, hps):
    c = pl.program_id(2)

    @pl.when(c == 0)
    def _():
        st_ref[...] = s0_ref[0]

    hd = range(hps)
    qs = [q_ref[:, h * R_DK:(h + 1) * R_DK] for h in hd]
    ks = [k_ref[:, h * R_DK:(h + 1) * R_DK] for h in hd]
    vs = [v_ref[:, h * R_DV:(h + 1) * R_DV] for h in hd]
    sts = [st_ref[h] for h in hd]
    nt = (((1,), (1,)), ((), ()))
    inner = [lax.dot_general(qs[h], ks[h], nt, preferred_element_type=F32) for h in hd]
    cross = [jnp.dot(qs[h], sts[h].astype(BF16), preferred_element_type=F32) for h in hd]
    kdec = []
    for h in hd:
        kd = ks[h].astype(F32) * jnp.concatenate([kd_ref[h]] * (R_DK // LANES), axis=1)
        if lc < LANES:
            kd = jnp.concatenate([kd, jnp.zeros((LANES - lc, R_DK), F32)], axis=0)
        kdec.append(kd.T.astype(BF16))
    inner = [(inner[h] * dec_ref[h]).astype(BF16) for h in hd]
    o = [jnp.dot(inner[h], vs[h], preferred_element_type=F32) for h in hd]
    upd = []
    for h in hd:
        vv = vs[h]
        if lc < LANES:
            vv = jnp.concatenate([vv, jnp.zeros((LANES - lc, R_DV), BF16)], axis=0)
        upd.append(jnp.dot(kdec[h], vv, preferred_element_type=F32))
    for h in hd:
        cd = jnp.concatenate([cd_ref[h]] * (R_DV // LANES), axis=1)
        oh = o[h] + cross[h] * cd
        ms = jnp.mean(oh * oh, axis=-1, keepdims=True)
        g = g_ref[:, h * R_DV:(h + 1) * R_DV].astype(F32)
        o_ref[:, h * R_DV:(h + 1) * R_DV] = (oh * lax.rsqrt(ms + EPS) * (g * _sigmoid(g))).astype(o_ref.dtype)
    st_new = []
    for h in hd:
        gl = jnp.concatenate([gl_ref[h]] * (R_DV // LANES), axis=1)
        st_new.append(sts[h] * gl + upd[h])
        st_ref[h] = st_new[h]

    @pl.when(c == pl.num_programs(2) - 1)
    def _():
        for h in hd:
            sout_ref[0, h] = st_new[h]


def _retention(q, k, vg, s0, lc, *, name):
    bsz, heads = s0.shape[:2]
    t = q.shape[0] // bsz
    nc = t // lc
    lg = jnp.log1p(-(2.0 ** (-5.0 - jnp.arange(heads, dtype=F32))))
    idx = jnp.arange(lc, dtype=F32)
    diff = idx[:, None] - idx[None, :]
    dec = jnp.where(diff >= 0, jnp.exp(lg[:, None, None] * jnp.maximum(diff, 0.0)), 0.0)
    cd = jnp.broadcast_to(jnp.exp(lg[:, None] * (idx[None, :] + 1.0))[:, :, None], (heads, lc, LANES))
    kd = jnp.broadcast_to(jnp.exp(lg[:, None] * (lc - 1.0 - idx[None, :]))[:, :, None], (heads, lc, LANES))
    gl = jnp.broadcast_to(jnp.exp(lg * lc)[:, None, None], (heads, 1, LANES))
    hps = RET_HEADS_PER_STEP
    assert heads % hps == 0
    rows = lambda b, h, c: (b * nc + c, h)
    gate = lambda b, h, c: (b * nc + c, heads // hps + h)
    tab = lambda b, h, c: (h, 0, 0)
    state = lambda b, h, c: (b, h, 0, 0)
    return pl.pallas_call(
        functools.partial(_retention_kernel, lc=lc, hps=hps),
        out_shape=(jax.ShapeDtypeStruct((vg.shape[0], heads * R_DV), BF16),
                   jax.ShapeDtypeStruct(s0.shape, F32)),
        grid=(bsz, heads // hps, nc),
        in_specs=[pl.BlockSpec((lc, hps * R_DK), rows),
                  pl.BlockSpec((lc, hps * R_DK), rows),
                  pl.BlockSpec((lc, hps * R_DV), rows),
                  pl.BlockSpec((lc, hps * R_DV), gate),
                  pl.BlockSpec((1, hps, R_DK, R_DV), state),
                  pl.BlockSpec((hps, lc, lc), tab),
                  pl.BlockSpec((hps, lc, LANES), tab),
                  pl.BlockSpec((hps, lc, LANES), tab),
                  pl.BlockSpec((hps, 1, LANES), tab)],
        out_specs=(pl.BlockSpec((lc, hps * R_DV), rows),
                   pl.BlockSpec((1, hps, R_DK, R_DV), state)),
        scratch_shapes=[pltpu.VMEM((hps, R_DK, R_DV), F32)],
        compiler_params=_params(("parallel", "parallel", "arbitrary")),
        name=name,
    )(q, k, vg, vg, s0, dec, cd, kd, gl)


def _cmlp_kernel(u_ref, g_ref, v_ref, vg_ref, w_ref, b_ref, *out_refs, emit_v):
    if emit_v:
        a_ref, vn_ref = out_refs
    else:
        (a_ref,) = out_refs
    va = _gelu(v_ref[...])
    ms = jnp.mean(va * va, axis=-1, keepdims=True)
    vn = va * lax.rsqrt(ms + EPS) * vg_ref[...]
    if emit_v:
        vn_ref[...] = vn
    vb = vn.astype(BF16)
    gd = vn.shape[1] // M_GROUPS
    for grp in range(M_GROUPS):
        sl = slice(grp * gd, (grp + 1) * gd)
        bias = jnp.concatenate([b_ref[grp]] * (gd // LANES), axis=1)
        mix = jnp.dot(w_ref[grp], vb[:, sl], preferred_element_type=F32) + bias
        g = g_ref[:, sl].astype(F32)
        a_ref[:, sl] = (_gelu(u_ref[:, sl].astype(F32)) * mix * (g * _sigmoid(g))).astype(a_ref.dtype)


def _cmlp(ug, v, v_gain, wmix, bmix, *, emit_v, name):
    m, w = v.shape
    t = wmix.shape[1]
    out_shape = [jax.ShapeDtypeStruct((m, w), BF16)]
    out_specs = [pl.BlockSpec((t, w), lambda i: (i, 0))]
    if emit_v:
        out_shape.append(jax.ShapeDtypeStruct((m, w), F32))
        out_specs.append(pl.BlockSpec((t, w), lambda i: (i, 0)))
    res = pl.pallas_call(
        functools.partial(_cmlp_kernel, emit_v=emit_v),
        out_shape=tuple(out_shape),
        grid=(m // t,),
        in_specs=[pl.BlockSpec((t, w), lambda i: (i, 0)),
                  pl.BlockSpec((t, w), lambda i: (i, 1)),
                  pl.BlockSpec((t, w), lambda i: (i, 0)),
                  pl.BlockSpec((1, w), lambda i: (0, 0)),
                  pl.BlockSpec((M_GROUPS, t, t), lambda i: (0, 0, 0)),
                  pl.BlockSpec((M_GROUPS, t, LANES), lambda i: (0, 0, 0))],
        out_specs=tuple(out_specs),
        compiler_params=_params(("parallel",)),
        name=name,
    )(ug, ug, v, v_gain.reshape(1, w), wmix, bmix)
    return res if emit_v else res[0]


def _rope_tables(pos, d, signed):
    inv = ROPE_THETA ** (-jnp.arange(0, d, 2, dtype=F32) / d)
    ang = pos.astype(F32)[:, None] * inv[None, :]
    cos, sin = jnp.cos(ang), jnp.sin(ang)
    reps = LANES // (d // 2)
    if signed:
        return (jnp.tile(cos, (1, reps)),
                jnp.tile(jnp.concatenate([-sin, sin], axis=1), (1, reps // 2)))
    return jnp.tile(cos, (1, reps)), jnp.tile(sin, (1, reps))


def _diff_attn_layer(h, pos, nw, w_in, w_out, q_gain, k_gain, lamp, sub_gain, lam_init,
                     layers, layer, kbuf, vbuf, cache, tag, after_proj=None):
    w = w_out.shape[0]
    cos, sin = _rope_tables(pos, A_HD, signed=True)
    wq, wk, wv, wg = (w_in[:, i * w:(i + 1) * w].astype(BF16) for i in range(4))
    (q,), (k, kb), (v, vb), (g,) = _proj(h, nw, [
        _seg(wq, (BF16,), "qk", (A_HD ** -0.5) * LOG2E, q_gain),
        _seg(wk, (F32, BF16), "qk", gain=k_gain, stack=(layers, layer, kbuf)),
        _seg(wv, (F32, BF16), stack=(layers, layer, vbuf)),
        _seg(wg, (BF16,))], cos=cos, sin=sin, name=f"{tag}_proj")
    if after_proj is not None:
        q, after_proj = lax.optimization_barrier((q, after_proj))
    if cache is None:
        o = _attn_prompt(q, kb, vb, g, lamp, sub_gain, lam_init, name=f"{tag}_attn")
    else:
        kc, vc = cache
        o = _attn_sample(q, kb, vb, g, kc, vc, layer, lamp, sub_gain, lam_init, name=f"{tag}_attn")
    return _outproj(o, w_out.astype(BF16), h, name=f"{tag}_out"), k, v, after_proj


def _retention_layer(h, pos, nw, w_in, w_out, s0, lc, tag):
    heads = s0.shape[1]
    qk_w, v_w = heads * R_DK, heads * R_DV
    cos, sin = _rope_tables(pos, R_DK, signed=False)
    wq = w_in[:, :qk_w].astype(BF16)
    wk = w_in[:, qk_w:2 * qk_w].astype(BF16)
    assert w_in.shape[1] == 2 * qk_w + 2 * v_w
    wvg = w_in[:, 2 * qk_w:].astype(BF16)
    (q,), (k,), (vg,) = _proj(h, nw, [
        _seg(wq, (BF16,), "rope256"),
        _seg(wk, (BF16,), "rope256", R_DK ** -0.5),
        _seg(wvg, (BF16,))], cos=cos, sin=sin, name=f"{tag}_proj")
    o, s_new = _retention(q, k, vg, s0, lc, name=f"{tag}_ret")
    return _outproj(o, w_out.astype(BF16), h, name=f"{tag}_out"), s_new


def _cmlp_layer(h, nw, w_in, w_out, v_gain, wmix, bmix, emit_v, tag):
    w = w_out.shape[0]
    wu, wv, wg = (w_in[:, i * w:(i + 1) * w].astype(BF16) for i in range(3))
    (ug,), (v,) = _proj(h, nw, [_seg(jnp.concatenate([wu, wg], axis=1), (BF16,)), _seg(wv, (F32,))],
                        name=f"{tag}_proj")
    res = _cmlp(ug, v, v_gain, wmix, bmix, emit_v=emit_v, name=f"{tag}_mix")
    a, vn = res if emit_v else (res, None)
    return _outproj(a, w_out.astype(BF16), h, name=f"{tag}_out"), vn


def _mix_tables(w_s, b_s, chunk_len):
    groups = w_s.shape[0]
    wl = jnp.tril(w_s[:, :chunk_len, :chunk_len])
    reps = M_CHUNK // chunk_len
    eye = jnp.eye(reps, dtype=F32)
    wt = jnp.einsum("ab,gij->gaibj", eye, wl).reshape(groups, M_CHUNK, M_CHUNK)
    bt = jnp.tile(b_s[:, :chunk_len], (1, reps))
    return wt.astype(BF16), jnp.broadcast_to(bt[:, :, None], (groups, M_CHUNK, LANES))


def kernel(x_prompt, x_sample, cache_k_attn, cache_v_attn, state_ret, norm_w, a_w_in, a_w_out, a_q_gain, a_k_gain, a_lam_q1, a_lam_k1, a_lam_q2, a_lam_k2, a_sub_gain, r_w_in, r_w_out, c_w_in, c_w_out, c_v_gain, c_w_s, c_b_s):
    batch, s_len, d = x_prompt.shape
    dec_b, dec_len, _ = x_sample.shape
    past = cache_k_attn.shape[2]
    depth = norm_w.shape[0]
    assert batch == 1 and M_CHUNK % dec_len == 0 and s_len % M_CHUNK == 0

    hp = x_prompt.reshape(s_len, d)
    hs = x_sample.reshape(dec_b * dec_len, d)
    pos_p = jnp.arange(s_len, dtype=jnp.int32)
    pos_s = jnp.tile(past + jnp.arange(dec_len, dtype=jnp.int32), dec_b)

    n_a = a_w_in.shape[0]
    aw = a_w_out.shape[1]
    cache_k = cache_k_attn.reshape(n_a, dec_b, past, aw)
    cache_v = cache_v_attn
    kp = vp = kn = vn = None
    sp_l, ss_l, vm_l = [], [], []
    for i in range(depth):
        kind, j = i % N_MIXERS, i // N_MIXERS
        if kind == 0:
            lam_init = 0.8 - 0.6 * math.exp(-0.3 * i)
            lamp = jnp.stack([a_lam_q1[j], a_lam_k1[j], a_lam_q2[j], a_lam_k2[j]])
            args = (norm_w[i], a_w_in[j], a_w_out[j], a_q_gain[j], a_k_gain[j], lamp, a_sub_gain[j], lam_init)
            hp, kp, vp, tied = _diff_attn_layer(hp, pos_p, *args, n_a, j, kp, vp, None, f"l{i}p",
                                                after_proj=cache_v if j == 0 else None)
            if j == 0:
                cache_v = tied.reshape(n_a, dec_b, past, aw)
            hp, hs = lax.optimization_barrier((hp, hs))
            hs, kn, vn, _ = _diff_attn_layer(hs, pos_s, *args, n_a, j, kn, vn, (cache_k, cache_v), f"l{i}s")
        elif kind == 1:
            heads = state_ret.shape[2]
            s0 = jnp.zeros((batch, heads, R_DK, R_DV), F32)
            hp, st_p = _retention_layer(hp, pos_p, norm_w[i], r_w_in[j], r_w_out[j], s0,
                                        _tile(s_len, 256), f"l{i}p")
            hs, st_s = _retention_layer(hs, pos_s, norm_w[i], r_w_in[j], r_w_out[j],
                                        state_ret[j].astype(F32), dec_len, f"l{i}s")
            sp_l.append(st_p)
            ss_l.append(st_s)
        else:
            wp, bp = _mix_tables(c_w_s[j], c_b_s[j], M_CHUNK)
            ws, bs = _mix_tables(c_w_s[j], c_b_s[j], dec_len)
            hp, _ = _cmlp_layer(hp, norm_w[i], c_w_in[j], c_w_out[j], c_v_gain[j], wp, bp, False, f"l{i}p")
            hs, v_s = _cmlp_layer(hs, norm_w[i], c_w_in[j], c_w_out[j], c_v_gain[j], ws, bs, True, f"l{i}s")
            vm_l.append(v_s.reshape(dec_b, dec_len, -1))

    hp, hs, kp = lax.optimization_barrier((hp, hs, kp))
    return (hp.reshape(batch, s_len, d), hs.reshape(dec_b, dec_len, d),
            kp.reshape(n_a, batch, s_len, aw // A_HD, A_HD), vp.reshape(n_a, batch, s_len, aw // A_VD, A_VD),
            kn.reshape(n_a, dec_b, dec_len, aw // A_HD, A_HD), vn.reshape(n_a, dec_b, dec_len, aw // A_VD, A_VD),
            jnp.stack(sp_l), jnp.stack(ss_l), jnp.stack(vm_l))
```

```python
import functools
import math

import jax
import jax.numpy as jnp
from jax import lax
from jax.experimental import pallas as pl
from jax.experimental.pallas import tpu as pltpu

F32 = jnp.float32
BF16 = jnp.bfloat16

EPS = 1e-6
CHUNK = 64
ROPE_THETA = 10000.0
N_MIXERS = 3
A_HD = 64
A_VD = 2 * A_HD
R_DK = 256
R_DV = 2 * R_DK
M_GROUPS = 8
M_CHUNK = 128
LOG2E = 1.4426950408889634

LANES = 128
MXU_COLS = 256
VMEM_LIMIT = 56 * 1024 * 1024
NEG_BIG = -1e30


def _tile(n, pref):
    if n <= pref:
        return n
    t = pref
    while t >= 8:
        if n % t == 0:
            return t
        t -= 8
    return n


def _params(sem):
    return pltpu.CompilerParams(dimension_semantics=sem, vmem_limit_bytes=VMEM_LIMIT)


def _gelu(x):
    return 0.5 * x * (1.0 + jnp.tanh(0.7978845608028654 * (x + 0.044715 * (x * x * x))))


def _sigmoid(x):
    return 1.0 / (1.0 + jnp.exp(-x))


def _proj_kernel(*refs, plan):
    xn_ref = refs[-1]
    it = iter(refs)
    x_ref, nw_ref = next(it), next(it)
    epis = [p[0] for p in plan]
    cos_ref, sin_ref = (next(it), next(it)) if any(e != "none" for e in epis) else (None, None)
    gsum_ref = next(it) if "qk" in epis else None
    w_refs, gain_refs = [], []
    for e in epis:
        w_refs.append(next(it))
        gain_refs.append(next(it) if e == "qk" else None)
    outs = list(refs[len(refs) - 1 - sum(p[2] for p in plan):-1])
    o_refs = []
    for p in plan:
        o_refs.append(outs[:p[2]])
        outs = outs[p[2]:]

    def store(seg, cols, val):
        scale = plan[seg][1]
        if scale != 1.0:
            val = val * scale
        for o_ref in o_refs[seg]:
            o_ref[:, cols] = val.astype(o_ref.dtype)

    @pl.when(pl.program_id(1) == 0)
    def _():
        x = x_ref[...]
        ms = jnp.mean(x * x, axis=-1, keepdims=True)
        xn_ref[...] = (x * lax.rsqrt(ms + EPS) * nw_ref[...]).astype(BF16)

    qk = [s for s, e in enumerate(epis) if e == "qk"]
    rest = [s for s, e in enumerate(epis) if e == "rope256"] + [s for s, e in enumerate(epis) if e == "none"]
    slabs = {s: [slice(c * MXU_COLS, (c + 1) * MXU_COLS) for c in range(w_refs[s].shape[1] // MXU_COLS)]
             for s in qk}
    zq = {(s, c): jnp.dot(xn_ref[...], w_refs[s][:, sl], preferred_element_type=F32)
          for s in qk for c, sl in enumerate(slabs[s])}
    sq = {key: jnp.dot((z * z).astype(BF16), gsum_ref[...], preferred_element_type=F32) for key, z in zq.items()}
    zr = {s: jnp.dot(xn_ref[...], w_refs[s][...], preferred_element_type=F32) for s in rest}

    if qk:
        cos = jnp.concatenate([cos_ref[...]] * 2, axis=1)
        sin = jnp.concatenate([sin_ref[...]] * 2, axis=1)
        lane = lax.broadcasted_iota(jnp.int32, (xn_ref.shape[0], MXU_COLS), 1)
        first_half = (lane % A_HD) < (A_HD // 2)
        for (s, c), z in zq.items():
            gain = jnp.concatenate([gain_refs[s][...]] * 2, axis=1)
            zn = z * lax.rsqrt(sq[(s, c)] * (1.0 / A_HD) + EPS) * gain
            partner = jnp.where(first_half,
                                pltpu.roll(zn, MXU_COLS - A_HD // 2, 1),
                                pltpu.roll(zn, A_HD // 2, 1))
            store(s, slabs[s][c], zn * cos + partner * sin)
    for s in rest:
        z = zr[s]
        if epis[s] == "rope256":
            cos = cos_ref[...]
            sin = sin_ref[...]
            for c in range(z.shape[1] // R_DK):
                lo, hi = slice(c * R_DK, c * R_DK + LANES), slice(c * R_DK + LANES, (c + 1) * R_DK)
                x1, x2 = z[:, lo], z[:, hi]
                store(s, lo, x1 * cos - x2 * sin)
                store(s, hi, x2 * cos + x1 * sin)
        else:
            store(s, slice(None), z)


def _seg(w, out_dtypes, epi="none", out_scale=1.0, gain=None, stack=None):
    return dict(w=w, out_dtypes=out_dtypes, epi=epi, out_scale=out_scale, gain=gain, stack=stack)


PROJ_VMEM_BUDGET = 48 * 1024 * 1024


def _proj(x, nw, segs, *, cos=None, sin=None, name):
    m, d = x.shape
    tm = _tile(m, 1024)
    epis = [s["epi"] for s in segs]
    assert not ("qk" in epis and "rope256" in epis)
    unit = {"qk": MXU_COLS, "rope256": R_DK, "none": LANES}

    def vmem_bytes(nj):
        total = 2 * tm * d * 4 + tm * d * 2
        for s in segs:
            tn = s["w"].shape[1] // nj
            total += 2 * d * tn * 2 + tm * tn * 4
            total += sum(2 * tm * tn * jnp.dtype(dt).itemsize for dt in s["out_dtypes"])
        return total

    nj = next(c for c in (1, 2, 4, 8, 16, 32)
              if all(s["w"].shape[1] % (c * unit[s["epi"]]) == 0 for s in segs) and vmem_bytes(c) <= PROJ_VMEM_BUDGET)
    in_specs = [pl.BlockSpec((tm, d), lambda i, j: (i, 0)),
                pl.BlockSpec((1, d), lambda i, j: (0, 0))]
    args = [x, nw.reshape(1, d)]
    if any(e != "none" for e in epis):
        in_specs += [pl.BlockSpec((tm, LANES), lambda i, j: (i, 0))] * 2
        args += [cos, sin]
    if "qk" in epis:
        gidx = jnp.arange(MXU_COLS) // A_HD
        in_specs.append(pl.BlockSpec((MXU_COLS, MXU_COLS), lambda i, j: (0, 0)))
        args.append((gidx[:, None] == gidx[None, :]).astype(BF16))
    out_shape, out_specs, bufs = [], [], []
    for s in segs:
        n = s["w"].shape[1]
        tn = n // nj
        in_specs.append(pl.BlockSpec((d, tn), lambda i, j: (0, j)))
        args.append(s["w"])
        if s["epi"] == "qk":
            in_specs.append(pl.BlockSpec((1, LANES), lambda i, j: (0, 0)))
            args.append(jnp.tile(s["gain"].reshape(1, A_HD), (1, LANES // A_HD)))
        for o, dt in enumerate(s["out_dtypes"]):
            if o == 0 and s["stack"] is not None:
                layers, layer, buf = s["stack"]
                out_specs.append(pl.BlockSpec((None, tm, tn), lambda i, j, layer=layer: (layer, i, j)))
                out_shape.append(jax.ShapeDtypeStruct((layers, m, n), dt))
                if buf is not None:
                    bufs.append((len(out_shape) - 1, buf))
            else:
                out_specs.append(pl.BlockSpec((tm, tn), lambda i, j: (i, j)))
                out_shape.append(jax.ShapeDtypeStruct((m, n), dt))
    aliases = {}
    for out_idx, buf in bufs:
        aliases[len(args)] = out_idx
        in_specs.append(pl.BlockSpec(memory_space=pl.ANY))
        args.append(buf)
    res = list(pl.pallas_call(
        functools.partial(_proj_kernel, plan=tuple((s["epi"], s["out_scale"], len(s["out_dtypes"])) for s in segs)),
        out_shape=tuple(out_shape),
        grid=(m // tm, nj),
        in_specs=in_specs,
        out_specs=tuple(out_specs),
        scratch_shapes=[pltpu.VMEM((tm, d), BF16)],
        input_output_aliases=aliases,
        compiler_params=_params(("parallel", "arbitrary")),
        name=name,
    )(*args))
    out = []
    for s in segs:
        out.append(res[:len(s["out_dtypes"])])
        res = res[len(s["out_dtypes"]):]
    return out


def _outproj_kernel(a_ref, w_ref, h_ref, o_ref):
    o_ref[...] = h_ref[...] + jnp.dot(a_ref[...], w_ref[...], preferred_element_type=F32)


def _outproj(a, w, h, *, name):
    m, k = a.shape
    n = w.shape[1]
    tm = _tile(m, 1024)
    tn = _tile(n, 1024)
    return pl.pallas_call(
        _outproj_kernel,
        out_shape=jax.ShapeDtypeStruct((m, n), F32),
        grid=(m // tm, n // tn),
        in_specs=[pl.BlockSpec((tm, k), lambda i, j: (i, 0)),
                  pl.BlockSpec((k, tn), lambda i, j: (0, j)),
                  pl.BlockSpec((tm, tn), lambda i, j: (i, j))],
        out_specs=pl.BlockSpec((tm, tn), lambda i, j: (i, j)),
        compiler_params=_params(("parallel", "parallel")),
        name=name,
    )(a, w, h)


def _lambda(lamp_ref, lam_init):
    lp = lamp_ref[...]
    s1 = jnp.sum(lp[0:1, :] * lp[1:2, :], axis=-1, keepdims=True)
    s2 = jnp.sum(lp[2:3, :] * lp[3:4, :], axis=-1, keepdims=True)
    return jnp.exp(s1) - jnp.exp(s2) + lam_init


V_AUG = A_VD + 16


LOOKAHEAD = 5


def _attn_prompt_kernel(q_ref, qn_ref, k_ref, v_ref, g_ref, lamp_ref, sg_ref, o_ref,
                        qpad_ref, vt_ref, m_ref, acc_ref, s_ref, mx_ref, *, tq, tk, lam_init):
    i = pl.program_id(1)
    slot = i % 2

    sw = s_ref.shape[2]
    n_strips = 2 * tq // sw
    la = min(LOOKAHEAD, n_strips - 1)
    order = sorted(range(n_strips), key=lambda c: -((c * sw) % tq))

    def load_queries(src_ref, dst):
        qt = src_ref[...].astype(F32).T
        row = lax.broadcasted_iota(jnp.int32, qt.shape, 0)
        qpad_ref[dst, :, :tq] = jnp.where(row < A_HD, qt, 0.0).astype(BF16)
        qpad_ref[dst, :, tq:] = jnp.where(row >= A_HD, qt, 0.0).astype(BF16)

    def scores(j, c, rows=tk, src=slot):
        kj = k_ref[pl.ds(pl.multiple_of(j * tk, tk), rows), :]
        s = jnp.dot(kj, qpad_ref[src, :, c * sw:(c + 1) * sw], preferred_element_type=F32)
        s_ref[c, :rows] = s
        mx_ref[c] = jnp.max(s, axis=0, keepdims=True)

    @pl.when(i == 0)
    def _():
        aug = lax.broadcasted_iota(jnp.int32, (V_AUG - A_VD, tk), 0)
        ones_row = jnp.where(aug == 0, 1.0, 0.0).astype(BF16)

        def fill(t, carry):
            vj = v_ref[pl.ds(pl.multiple_of(t * tk, tk), tk), :].astype(F32)
            vt_ref[t, :A_VD, :] = vj.T.astype(BF16)
            vt_ref[t, A_VD:, :] = ones_row
            return carry

        lax.fori_loop(0, vt_ref.shape[0], fill, 0)
        load_queries(q_ref, 0)
        for c in order[:la]:
            scores(0, c, src=0)

    m_ref[...] = jnp.full(m_ref.shape, NEG_BIG, F32)
    acc_ref[...] = jnp.zeros(acc_ref.shape, F32)

    def update(j, c, mask, rows=tk):
        cs = slice(c * sw, (c + 1) * sw)
        s = s_ref[c, :rows]
        if mask is None:
            mx = mx_ref[c]
        else:
            s = jnp.where(mask[:rows], s, NEG_BIG)
            mx = jnp.max(s, axis=0, keepdims=True)
        m_old = m_ref[:, cs]
        m_new = jnp.maximum(m_old, mx)
        alpha = jnp.exp2(m_old - m_new)
        p = jnp.exp2(s - m_new).astype(BF16)
        pv = jnp.dot(vt_ref[j][:, :rows], p, preferred_element_type=F32)
        acc_ref[:, cs] = alpha * acc_ref[:, cs] + pv
        m_ref[:, cs] = m_new

    n_sub = tq // tk
    n_full = i * n_sub

    def full_tiles(j0, count):
        for j in range(j0, j0 + count) if isinstance(j0, int) else [j0 + t for t in range(count)]:
            for n, c in enumerate(order):
                update(j, c, None)
                if n + la < n_strips:
                    scores(j, order[n + la])
                else:
                    scores(j + 1, order[n + la - n_strips])

    odd = i % 2

    @pl.when(odd == 1)
    def _():
        full_tiles(0, n_sub)

    def body(jj, carry):
        full_tiles(odd * n_sub + jj * (2 * n_sub), 2 * n_sub)
        return carry

    lax.fori_loop(0, i // 2, body, 0)

    r = lax.broadcasted_iota(jnp.int32, (tk, sw), 0)
    cc = lax.broadcasted_iota(jnp.int32, (tk, sw), 1)
    units = []
    for d in range(n_sub):
        for c in order:
            q_lo, k_lo = (c * sw) % tq, d * tk
            if q_lo + sw <= k_lo:
                continue
            full = q_lo >= k_lo + tk
            rows = min(tk, q_lo + sw - k_lo)
            units.append((d, c, None if full else (k_lo + r) // CHUNK <= (q_lo + cc) // CHUNK, rows))
    assert [u[:2] + u[3:] for u in units[:la]] == [(0, c, tk) for c in order[:la]]
    load_queries(qn_ref, 1 - slot)
    for n, (d, c, mask, rows) in enumerate(units):
        update(n_full + d, c, mask, rows)
        if n + la < len(units):
            nd, nc, _, nrows = units[n + la]
            assert all(u[1] != nc for u in units[n + 1:n + la])
            scores(n_full + nd, nc, nrows)
    for c in order[:la]:
        scores(0, c, src=1 - slot)

    lam = _lambda(lamp_ref, lam_init)
    acc = acc_ref[...]
    inv_l = 1.0 / acc[A_VD:A_VD + 1, :]
    ot = acc[:A_VD, :tq] * inv_l[:, :tq] - lam * (acc[:A_VD, tq:] * inv_l[:, tq:])
    ms = jnp.mean(ot * ot, axis=0, keepdims=True)
    ot = ot * lax.rsqrt(ms + EPS) * (sg_ref[...] * (1.0 - lam_init))
    g = g_ref[...].astype(F32)
    o_ref[...] = (ot.T * (g * _sigmoid(g))).astype(o_ref.dtype)


def _attn_prompt(q, k, v, g, lamp, sub_gain, lam_init, *, name):
    s, w = q.shape
    heads = w // A_VD
    tq = _tile(s, 1024)
    tk = _tile(tq, 512)
    sw = min(MXU_COLS, tq)
    nq = s // tq
    return pl.pallas_call(
        functools.partial(_attn_prompt_kernel, tq=tq, tk=tk, lam_init=lam_init),
        out_shape=jax.ShapeDtypeStruct((s, w), BF16),
        grid=(heads, nq),
        in_specs=[pl.BlockSpec((tq, A_VD), lambda h, i: (i, h)),
                  pl.BlockSpec((tq, A_VD), lambda h, i: (jnp.minimum(i + 1, nq - 1), h)),
                  pl.BlockSpec((s, A_VD), lambda h, i: (0, h)),
                  pl.BlockSpec((s, A_VD), lambda h, i: (0, h)),
                  pl.BlockSpec((tq, A_VD), lambda h, i: (i, h)),
                  pl.BlockSpec((4, A_HD), lambda h, i: (0, 0)),
                  pl.BlockSpec((A_VD, 1), lambda h, i: (0, 0))],
        out_specs=pl.BlockSpec((tq, A_VD), lambda h, i: (i, h)),
        scratch_shapes=[pltpu.VMEM((2, A_VD, 2 * tq), BF16),
                        pltpu.VMEM((s // tk, V_AUG, tk), BF16),
                        pltpu.VMEM((1, 2 * tq), F32),
                        pltpu.VMEM((V_AUG, 2 * tq), F32),
                        pltpu.VMEM((2 * tq // sw, tk, sw), F32),
                        pltpu.VMEM((2 * tq // sw, 1, sw), F32)],
        compiler_params=_params(("parallel", "arbitrary")),
        name=name,
    )(q, q, k, v, g, lamp, sub_gain.reshape(A_VD, 1))


SAMPLE_HEADS_PER_STEP = 4


def _attn_sample_kernel(q_ref, kn_ref, vn_ref, g_ref, kc_ref, vc_ref, lamp_ref, sg_ref, o_ref, *, lam_init, hps):
    lam = _lambda(lamp_ref, lam_init)
    nt = (((1,), (1,)), ((), ()))
    lane = lax.broadcasted_iota(jnp.int32, (q_ref.shape[0], A_VD), 1)
    heads = [slice(h * A_VD, (h + 1) * A_VD) for h in range(hps)]
    scores = []
    for hs in heads:
        q = q_ref[:, hs]
        kc = kc_ref[0, :, hs].astype(BF16)
        kn = kn_ref[:, hs].astype(BF16)
        for half in range(2):
            qh = jnp.where((lane >= A_HD) == bool(half), q, jnp.zeros_like(q))
            scores.append((lax.dot_general(qh, kc, nt, preferred_element_type=F32),
                           lax.dot_general(qh, kn, nt, preferred_element_type=F32)))
    probs = []
    for sc, sn in scores:
        m = jnp.maximum(jnp.max(sc, axis=-1, keepdims=True), jnp.max(sn, axis=-1, keepdims=True))
        pc = jnp.exp2(sc - m)
        pn = jnp.exp2(sn - m)
        inv = 1.0 / (jnp.sum(pc, axis=-1, keepdims=True) + jnp.sum(pn, axis=-1, keepdims=True))
        probs.append((pc * inv, pn * inv))
    outs = []
    for h, hs in enumerate(heads):
        ac = (probs[2 * h][0] - lam * probs[2 * h + 1][0]).astype(BF16)
        an = (probs[2 * h][1] - lam * probs[2 * h + 1][1]).astype(BF16)
        outs.append(jnp.dot(ac, vc_ref[0, :, hs].astype(BF16), preferred_element_type=F32)
                    + jnp.dot(an, vn_ref[:, hs].astype(BF16), preferred_element_type=F32))
    for hs, o in zip(heads, outs):
        ms = jnp.mean(o * o, axis=-1, keepdims=True)
        g = g_ref[:, hs].astype(F32)
        o = o * lax.rsqrt(ms + EPS) * (sg_ref[...] * (1.0 - lam_init))
        o_ref[:, hs] = (o * (g * _sigmoid(g))).astype(o_ref.dtype)


def _attn_sample(q, k, v, g, kc, vc, layer, lamp, sub_gain, lam_init, *, name):
    _, bsz, past, w = kc.shape
    heads = w // A_VD
    ln = q.shape[0] // bsz
    assert past % CHUNK == 0 and ln <= CHUNK
    hps = SAMPLE_HEADS_PER_STEP
    assert heads % hps == 0
    row = lambda b, h: (b, h)
    cache = lambda b, h: (layer, b, 0, h)
    return pl.pallas_call(
        functools.partial(_attn_sample_kernel, lam_init=lam_init, hps=hps),
        out_shape=jax.ShapeDtypeStruct(q.shape, BF16),
        grid=(bsz, heads // hps),
        in_specs=[pl.BlockSpec((ln, hps * A_VD), row),
                  pl.BlockSpec((ln, hps * A_VD), row),
                  pl.BlockSpec((ln, hps * A_VD), row),
                  pl.BlockSpec((ln, hps * A_VD), row),
                  pl.BlockSpec((None, 1, past, hps * A_VD), cache),
                  pl.BlockSpec((None, 1, past, hps * A_VD), cache),
                  pl.BlockSpec((4, A_HD), lambda b, h: (0, 0)),
                  pl.BlockSpec((1, A_VD), lambda b, h: (0, 0))],
        out_specs=pl.BlockSpec((ln, hps * A_VD), row),
        compiler_params=_params(("parallel", "parallel")),
        name=name,
    )(q, k, v, g, kc, vc, lamp, sub_gain.reshape(1, A_VD))


RET_HEADS_PER_STEP = 8


def _retention_kernel(q_ref, k_ref, v_ref, g_ref, s0_ref, dec_ref, cd_ref, kd_ref, gl_ref,
                      o_ref, sout_ref, st_ref, *, lc, hps):
    c = pl.program_id(2)

    @pl.when(c == 0)
    def _():
        st_ref[...] = s0_ref[0]

    hd = range(hps)
    qs = [q_ref[:, h * R_DK:(h + 1) * R_DK] for h in hd]
    ks = [k_ref[:, h * R_DK:(h + 1) * R_DK] for h in hd]
    vs = [v_ref[:, h * R_DV:(h + 1) * R_DV] for h in hd]
    sts = [st_ref[h] for h in hd]
    nt = (((1,), (1,)), ((), ()))
    inner = [lax.dot_general(qs[h], ks[h], nt, preferred_element_type=F32) for h in hd]
    cross = [jnp.dot(qs[h], sts[h].astype(BF16), preferred_element_type=F32) for h in hd]
    kdec = []
    for h in hd:
        kd = ks[h].astype(F32) * jnp.concatenate([kd_ref[h]] * (R_DK // LANES), axis=1)
        if lc < LANES:
            kd = jnp.concatenate([kd, jnp.zeros((LANES - lc, R_DK), F32)], axis=0)
        kdec.append(kd.T.astype(BF16))
    inner = [(inner[h] * dec_ref[h]).astype(BF16) for h in hd]
    o = [jnp.dot(inner[h], vs[h], preferred_element_type=F32) for h in hd]
    upd = []
    for h in hd:
        vv = vs[h]
        if lc < LANES:
            vv = jnp.concatenate([vv, jnp.zeros((LANES - lc, R_DV), BF16)], axis=0)
        upd.append(jnp.dot(kdec[h], vv, preferred_element_type=F32))
    for h in hd:
        cd = jnp.concatenate([cd_ref[h]] * (R_DV // LANES), axis=1)
        oh = o[h] + cross[h] * cd
        ms = jnp.mean(oh * oh, axis=-1, keepdims=True)
        g = g_ref[:, h * R_DV:(h + 1) * R_DV].astype(F32)
        o_ref[:, h * R_DV:(h + 1) * R_DV] = (oh * lax.rsqrt(ms + EPS) * (g * _sigmoid(g))).astype(o_ref.dtype)
    st_new = []
    for h in hd:
        gl = jnp.concatenate([gl_ref[h]] * (R_DV // LANES), axis=1)
        st_new.append(sts[h] * gl + upd[h])
        st_ref[h] = st_new[h]

    @pl.when(c == pl.num_programs(2) - 1)
    def _():
        for h in hd:
            sout_ref[0, h] = st_new[h]


def _retention(q, k, vg, s0, lc, *, name):
    bsz, heads = s0.shape[:2]
    t = q.shape[0] // bsz
    nc = t // lc
    lg = jnp.log1p(-(2.0 ** (-5.0 - jnp.arange(heads, dtype=F32))))
    idx = jnp.arange(lc, dtype=F32)
    diff = idx[:, None] - idx[None, :]
    dec = jnp.where(diff >= 0, jnp.exp(lg[:, None, None] * jnp.maximum(diff, 0.0)), 0.0)
    cd = jnp.broadcast_to(jnp.exp(lg[:, None] * (idx[None, :] + 1.0))[:, :, None], (heads, lc, LANES))
    kd = jnp.broadcast_to(jnp.exp(lg[:, None] * (lc - 1.0 - idx[None, :]))[:, :, None], (heads, lc, LANES))
    gl = jnp.broadcast_to(jnp.exp(lg * lc)[:, None, None], (heads, 1, LANES))
    hps = RET_HEADS_PER_STEP
    assert heads % hps == 0
    rows = lambda b, h, c: (b * nc + c, h)
    gate = lambda b, h, c: (b * nc + c, heads // hps + h)
    tab = lambda b, h, c: (h, 0, 0)
    state = lambda b, h, c: (b, h, 0, 0)
    return pl.pallas_call(
        functools.partial(_retention_kernel, lc=lc, hps=hps),
        out_shape=(jax.ShapeDtypeStruct((vg.shape[0], heads * R_DV), BF16),
                   jax.ShapeDtypeStruct(s0.shape, F32)),
        grid=(bsz, heads // hps, nc),
        in_specs=[pl.BlockSpec((lc, hps * R_DK), rows),
                  pl.BlockSpec((lc, hps * R_DK), rows),
                  pl.BlockSpec((lc, hps * R_DV), rows),
                  pl.BlockSpec((lc, hps * R_DV), gate),
                  pl.BlockSpec((1, hps, R_DK, R_DV), state),
                  pl.BlockSpec((hps, lc, lc), tab),
                  pl.BlockSpec((hps, lc, LANES), tab),
                  pl.BlockSpec((hps, lc, LANES), tab),
                  pl.BlockSpec((hps, 1, LANES), tab)],
        out_specs=(pl.BlockSpec((lc, hps * R_DV), rows),
                   pl.BlockSpec((1, hps, R_DK, R_DV), state)),
        scratch_shapes=[pltpu.VMEM((hps, R_DK, R_DV), F32)],
        compiler_params=_params(("parallel", "parallel", "arbitrary")),
        name=name,
    )(q, k, vg, vg, s0, dec, cd, kd, gl)


def _cmlp_kernel(u_ref, g_ref, v_ref, vg_ref, w_ref, b_ref, *out_refs, emit_v):
    if emit_v:
        a_ref, vn_ref = out_refs
    else:
        (a_ref,) = out_refs
    va = _gelu(v_ref[...])
    ms = jnp.mean(va * va, axis=-1, keepdims=True)
    vn = va * lax.rsqrt(ms + EPS) * vg_ref[...]
    if emit_v:
        vn_ref[...] = vn
    vb = vn.astype(BF16)
    gd = vn.shape[1] // M_GROUPS
    for grp in range(M_GROUPS):
        sl = slice(grp * gd, (grp + 1) * gd)
        bias = jnp.concatenate([b_ref[grp]] * (gd // LANES), axis=1)
        mix = jnp.dot(w_ref[grp], vb[:, sl], preferred_element_type=F32) + bias
        g = g_ref[:, sl].astype(F32)
        a_ref[:, sl] = (_gelu(u_ref[:, sl].astype(F32)) * mix * (g * _sigmoid(g))).astype(a_ref.dtype)


def _cmlp(ug, v, v_gain, wmix, bmix, *, emit_v, name):
    m, w = v.shape
    t = wmix.shape[1]
    out_shape = [jax.ShapeDtypeStruct((m, w), BF16)]
    out_specs = [pl.BlockSpec((t, w), lambda i: (i, 0))]
    if emit_v:
        out_shape.append(jax.ShapeDtypeStruct((m, w), F32))
        out_specs.append(pl.BlockSpec((t, w), lambda i: (i, 0)))
    res = pl.pallas_call(
        functools.partial(_cmlp_kernel, emit_v=emit_v),
        out_shape=tuple(out_shape),
        grid=(m // t,),
        in_specs=[pl.BlockSpec((t, w), lambda i: (i, 0)),
                  pl.BlockSpec((t, w), lambda i: (i, 1)),
                  pl.BlockSpec((t, w), lambda i: (i, 0)),
                  pl.BlockSpec((1, w), lambda i: (0, 0)),
                  pl.BlockSpec((M_GROUPS, t, t), lambda i: (0, 0, 0)),
                  pl.BlockSpec((M_GROUPS, t, LANES), lambda i: (0, 0, 0))],
        out_specs=tuple(out_specs),
        compiler_params=_params(("parallel",)),
        name=name,
    )(ug, ug, v, v_gain.reshape(1, w), wmix, bmix)
    return res if emit_v else res[0]


def _rope_tables(pos, d, signed):
    inv = ROPE_THETA ** (-jnp.arange(0, d, 2, dtype=F32) / d)
    ang = pos.astype(F32)[:, None] * inv[None, :]
    cos, sin = jnp.cos(ang), jnp.sin(ang)
    reps = LANES // (d // 2)
    if signed:
        return (jnp.tile(cos, (1, reps)),
                jnp.tile(jnp.concatenate([-sin, sin], axis=1), (1, reps // 2)))
    return jnp.tile(cos, (1, reps)), jnp.tile(sin, (1, reps))


def _diff_attn_layer(h, pos, nw, w_in, w_out, q_gain, k_gain, lamp, sub_gain, lam_init,
                     layers, layer, kbuf, vbuf, cache, tag):
    w = w_out.shape[0]
    cos, sin = _rope_tables(pos, A_HD, signed=True)
    wq, wk, wv, wg = (w_in[:, i * w:(i + 1) * w].astype(BF16) for i in range(4))
    (q,), (k, kb), (v, vb), (g,) = _proj(h, nw, [
        _seg(wq, (BF16,), "qk", (A_HD ** -0.5) * LOG2E, q_gain),
        _seg(wk, (F32, BF16), "qk", gain=k_gain, stack=(layers, layer, kbuf)),
        _seg(wv, (F32, BF16), stack=(layers, layer, vbuf)),
        _seg(wg, (BF16,))], cos=cos, sin=sin, name=f"{tag}_proj")
    if cache is None:
        o = _attn_prompt(q, kb, vb, g, lamp, sub_gain, lam_init, name=f"{tag}_attn")
        o, k = lax.optimization_barrier((o, k))
    else:
        kc, vc = cache
        o = _attn_sample(q, kb, vb, g, kc, vc, layer, lamp, sub_gain, lam_init, name=f"{tag}_attn")
    return _outproj(o, w_out.astype(BF16), h, name=f"{tag}_out"), k, v


def _retention_layer(h, pos, nw, w_in, w_out, s0, lc, tag):
    heads = s0.shape[1]
    qk_w, v_w = heads * R_DK, heads * R_DV
    cos, sin = _rope_tables(pos, R_DK, signed=False)
    wq = w_in[:, :qk_w].astype(BF16)
    wk = w_in[:, qk_w:2 * qk_w].astype(BF16)
    assert w_in.shape[1] == 2 * qk_w + 2 * v_w
    wvg = w_in[:, 2 * qk_w:].astype(BF16)
    (q,), (k,), (vg,) = _proj(h, nw, [
        _seg(wq, (BF16,), "rope256"),
        _seg(wk, (BF16,), "rope256", R_DK ** -0.5),
        _seg(wvg, (BF16,))], cos=cos, sin=sin, name=f"{tag}_proj")
    o, s_new = _retention(q, k, vg, s0, lc, name=f"{tag}_ret")
    return _outproj(o, w_out.astype(BF16), h, name=f"{tag}_out"), s_new


def _cmlp_layer(h, nw, w_in, w_out, v_gain, wmix, bmix, emit_v, tag):
    w = w_out.shape[0]
    wu, wv, wg = (w_in[:, i * w:(i + 1) * w].astype(BF16) for i in range(3))
    (ug,), (v,) = _proj(h, nw, [_seg(jnp.concatenate([wu, wg], axis=1), (BF16,)), _seg(wv, (F32,))],
                        name=f"{tag}_proj")
    res = _cmlp(ug, v, v_gain, wmix, bmix, emit_v=emit_v, name=f"{tag}_mix")
    a, vn = res if emit_v else (res, None)
    return _outproj(a, w_out.astype(BF16), h, name=f"{tag}_out"), vn


def _mix_tables(w_s, b_s, chunk_len):
    groups = w_s.shape[0]
    wl = jnp.tril(w_s[:, :chunk_len, :chunk_len])
    reps = M_CHUNK // chunk_len
    eye = jnp.eye(reps, dtype=F32)
    wt = jnp.einsum("ab,gij->gaibj", eye, wl).reshape(groups, M_CHUNK, M_CHUNK)
    bt = jnp.tile(b_s[:, :chunk_len], (1, reps))
    return wt.astype(BF16), jnp.broadcast_to(bt[:, :, None], (groups, M_CHUNK, LANES))


def kernel(x_prompt, x_sample, cache_k_attn, cache_v_attn, state_ret, norm_w, a_w_in, a_w_out, a_q_gain, a_k_gain, a_lam_q1, a_lam_k1, a_lam_q2, a_lam_k2, a_sub_gain, r_w_in, r_w_out, c_w_in, c_w_out, c_v_gain, c_w_s, c_b_s):
    batch, s_len, d = x_prompt.shape
    dec_b, dec_len, _ = x_sample.shape
    past = cache_k_attn.shape[2]
    depth = norm_w.shape[0]
    assert batch == 1 and M_CHUNK % dec_len == 0 and s_len % M_CHUNK == 0

    hp = x_prompt.reshape(s_len, d)
    hs = x_sample.reshape(dec_b * dec_len, d)
    pos_p = jnp.arange(s_len, dtype=jnp.int32)
    pos_s = jnp.tile(past + jnp.arange(dec_len, dtype=jnp.int32), dec_b)

    n_a = a_w_in.shape[0]
    aw = a_w_out.shape[1]
    cache = (cache_k_attn.reshape(n_a, dec_b, past, aw), cache_v_attn.reshape(n_a, dec_b, past, aw))
    kp = vp = kn = vn = None
    sp_l, ss_l, vm_l = [], [], []
    for i in range(depth):
        kind, j = i % N_MIXERS, i // N_MIXERS
        if kind == 0:
            lam_init = 0.8 - 0.6 * math.exp(-0.3 * i)
            lamp = jnp.stack([a_lam_q1[j], a_lam_k1[j], a_lam_q2[j], a_lam_k2[j]])
            args = (norm_w[i], a_w_in[j], a_w_out[j], a_q_gain[j], a_k_gain[j], lamp, a_sub_gain[j], lam_init)
            hp, kp, vp = _diff_attn_layer(hp, pos_p, *args, n_a, j, kp, vp, None, f"l{i}p")
            hp, hs = lax.optimization_barrier((hp, hs))
            hs, kn, vn = _diff_attn_layer(hs, pos_s, *args, n_a, j, kn, vn, cache, f"l{i}s")
        elif kind == 1:
            heads = state_ret.shape[2]
            s0 = jnp.zeros((batch, heads, R_DK, R_DV), F32)
            hp, st_p = _retention_layer(hp, pos_p, norm_w[i], r_w_in[j], r_w_out[j], s0,
                                        _tile(s_len, 256), f"l{i}p")
            hs, st_s = _retention_layer(hs, pos_s, norm_w[i], r_w_in[j], r_w_out[j],
                                        state_ret[j].astype(F32), dec_len, f"l{i}s")
            sp_l.append(st_p)
            ss_l.append(st_s)
        else:
            wp, bp = _mix_tables(c_w_s[j], c_b_s[j], M_CHUNK)
            ws, bs = _mix_tables(c_w_s[j], c_b_s[j], dec_len)
            hp, _ = _cmlp_layer(hp, norm_w[i], c_w_in[j], c_w_out[j], c_v_gain[j], wp, bp, False, f"l{i}p")
            hs, v_s = _cmlp_layer(hs, norm_w[i], c_w_in[j], c_w_out[j], c_v_gain[j], ws, bs, True, f"l{i}s")
            vm_l.append(v_s.reshape(dec_b, dec_len, -1))

    return (hp.reshape(batch, s_len, d), hs.reshape(dec_b, dec_len, d),
            kp.reshape(n_a, batch, s_len, aw // A_HD, A_HD), vp.reshape(n_a, batch, s_len, aw // A_VD, A_VD),
            kn.reshape(n_a, dec_b, dec_len, aw // A_HD, A_HD), vn.reshape(n_a, dec_b, dec_len, aw // A_VD, A_VD),
            jnp.stack(sp_l), jnp.stack(ss_l), jnp.stack(vm_l))
```

```python
import functools
import math

import jax
import jax.numpy as jnp
from jax import lax
from jax.experimental import pallas as pl
from jax.experimental.pallas import tpu as pltpu

F32 = jnp.float32
BF16 = jnp.bfloat16

EPS = 1e-6
CHUNK = 64
ROPE_THETA = 10000.0
N_MIXERS = 3
A_HD = 64
A_VD = 2 * A_HD
R_DK = 256
R_DV = 2 * R_DK
M_GROUPS = 8
M_CHUNK = 128
LOG2E = 1.4426950408889634

LANES = 128
MXU_COLS = 256
V7X_VMEM_BYTES = 64 * 1024 * 1024
VMEM_LIMIT = V7X_VMEM_BYTES * 7 // 8
NEG_BIG = -1e30


def _tile(n, pref):
    if n <= pref:
        return n
    t = pref
    while t >= 8:
        if n % t == 0:
            return t
        t -= 8
    return n


def _params(sem):
    return pltpu.CompilerParams(dimension_semantics=sem, vmem_limit_bytes=VMEM_LIMIT)


def _gelu(x):
    return 0.5 * x * (1.0 + jnp.tanh(0.7978845608028654 * (x + 0.044715 * (x * x * x))))


def _sigmoid(x):
    return 1.0 / (1.0 + jnp.exp(-x))


def _proj_kernel(*refs, plan):
    xn_ref = refs[-1]
    it = iter(refs)
    x_ref, nw_ref = next(it), next(it)
    epis = [p[0] for p in plan]
    cos_ref, sin_ref = (next(it), next(it)) if any(e != "none" for e in epis) else (None, None)
    gsum_ref = next(it) if "qk" in epis else None
    w_refs, gain_refs = [], []
    for e in epis:
        w_refs.append(next(it))
        gain_refs.append(next(it) if e == "qk" else None)
    outs = list(refs[len(refs) - 1 - sum(p[2] for p in plan):-1])
    o_refs = []
    for p in plan:
        o_refs.append(outs[:p[2]])
        outs = outs[p[2]:]

    def store(seg, cols, val):
        scale = plan[seg][1]
        if scale != 1.0:
            val = val * scale
        for o_ref in o_refs[seg]:
            o_ref[:, cols] = val.astype(o_ref.dtype)

    @pl.when(pl.program_id(1) == 0)
    def _():
        x = x_ref[...]
        ms = jnp.mean(x * x, axis=-1, keepdims=True)
        xn_ref[...] = (x * lax.rsqrt(ms + EPS) * nw_ref[...]).astype(BF16)

    qk = [s for s, e in enumerate(epis) if e == "qk"]
    rest = [s for s, e in enumerate(epis) if e == "rope256"] + [s for s, e in enumerate(epis) if e == "none"]
    slabs = {s: [slice(c * MXU_COLS, (c + 1) * MXU_COLS) for c in range(w_refs[s].shape[1] // MXU_COLS)]
             for s in qk}
    zq = {(s, c): jnp.dot(xn_ref[...], w_refs[s][:, sl], preferred_element_type=F32)
          for s in qk for c, sl in enumerate(slabs[s])}
    sq = {key: jnp.dot((z * z).astype(BF16), gsum_ref[...], preferred_element_type=F32) for key, z in zq.items()}
    zr = {s: jnp.dot(xn_ref[...], w_refs[s][...], preferred_element_type=F32) for s in rest}

    if qk:
        cos = jnp.concatenate([cos_ref[...]] * 2, axis=1)
        sin = jnp.concatenate([sin_ref[...]] * 2, axis=1)
        lane = lax.broadcasted_iota(jnp.int32, (xn_ref.shape[0], MXU_COLS), 1)
        first_half = (lane % A_HD) < (A_HD // 2)
        for (s, c), z in zq.items():
            gain = jnp.concatenate([gain_refs[s][...]] * 2, axis=1)
            zn = z * lax.rsqrt(sq[(s, c)] * (1.0 / A_HD) + EPS) * gain
            partner = jnp.where(first_half,
                                pltpu.roll(zn, MXU_COLS - A_HD // 2, 1),
                                pltpu.roll(zn, A_HD // 2, 1))
            store(s, slabs[s][c], zn * cos + partner * sin)
    for s in rest:
        z = zr[s]
        if epis[s] == "rope256":
            cos = cos_ref[...]
            sin = sin_ref[...]
            for c in range(z.shape[1] // R_DK):
                lo, hi = slice(c * R_DK, c * R_DK + LANES), slice(c * R_DK + LANES, (c + 1) * R_DK)
                x1, x2 = z[:, lo], z[:, hi]
                store(s, lo, x1 * cos - x2 * sin)
                store(s, hi, x2 * cos + x1 * sin)
        else:
            store(s, slice(None), z)


def _seg(w, out_dtypes, epi="none", out_scale=1.0, gain=None, stack=None):
    return dict(w=w, out_dtypes=out_dtypes, epi=epi, out_scale=out_scale, gain=gain, stack=stack)


PROJ_VMEM_BUDGET = V7X_VMEM_BYTES * 3 // 4


def _proj(x, nw, segs, *, cos=None, sin=None, name):
    m, d = x.shape
    tm = _tile(m, 1024)
    epis = [s["epi"] for s in segs]
    assert not ("qk" in epis and "rope256" in epis)
    unit = {"qk": MXU_COLS, "rope256": R_DK, "none": LANES}

    def vmem_bytes(nj):
        total = 2 * tm * d * 4 + tm * d * 2
        for s in segs:
            tn = s["w"].shape[1] // nj
            total += 2 * d * tn * 2 + tm * tn * 4
            total += sum(2 * tm * tn * jnp.dtype(dt).itemsize for dt in s["out_dtypes"])
        return total

    nj = next(c for c in (1, 2, 4, 8, 16, 32)
              if all(s["w"].shape[1] % (c * unit[s["epi"]]) == 0 for s in segs) and vmem_bytes(c) <= PROJ_VMEM_BUDGET)
    in_specs = [pl.BlockSpec((tm, d), lambda i, j: (i, 0)),
                pl.BlockSpec((1, d), lambda i, j: (0, 0))]
    args = [x, nw.reshape(1, d)]
    if any(e != "none" for e in epis):
        in_specs += [pl.BlockSpec((tm, LANES), lambda i, j: (i, 0))] * 2
        args += [cos, sin]
    if "qk" in epis:
        gidx = jnp.arange(MXU_COLS) // A_HD
        in_specs.append(pl.BlockSpec((MXU_COLS, MXU_COLS), lambda i, j: (0, 0)))
        args.append((gidx[:, None] == gidx[None, :]).astype(BF16))
    out_shape, out_specs, bufs = [], [], []
    for s in segs:
        n = s["w"].shape[1]
        tn = n // nj
        in_specs.append(pl.BlockSpec((d, tn), lambda i, j: (0, j)))
        args.append(s["w"])
        if s["epi"] == "qk":
            in_specs.append(pl.BlockSpec((1, LANES), lambda i, j: (0, 0)))
            args.append(jnp.tile(s["gain"].reshape(1, A_HD), (1, LANES // A_HD)))
        for o, dt in enumerate(s["out_dtypes"]):
            if o == 0 and s["stack"] is not None:
                layers, layer, buf = s["stack"]
                out_specs.append(pl.BlockSpec((None, tm, tn), lambda i, j, layer=layer: (layer, i, j)))
                out_shape.append(jax.ShapeDtypeStruct((layers, m, n), dt))
                if buf is not None:
                    bufs.append((len(out_shape) - 1, buf))
            else:
                out_specs.append(pl.BlockSpec((tm, tn), lambda i, j: (i, j)))
                out_shape.append(jax.ShapeDtypeStruct((m, n), dt))
    aliases = {}
    for out_idx, buf in bufs:
        aliases[len(args)] = out_idx
        in_specs.append(pl.BlockSpec(memory_space=pl.ANY))
        args.append(buf)
    res = list(pl.pallas_call(
        functools.partial(_proj_kernel, plan=tuple((s["epi"], s["out_scale"], len(s["out_dtypes"])) for s in segs)),
        out_shape=tuple(out_shape),
        grid=(m // tm, nj),
        in_specs=in_specs,
        out_specs=tuple(out_specs),
        scratch_shapes=[pltpu.VMEM((tm, d), BF16)],
        input_output_aliases=aliases,
        compiler_params=_params(("parallel", "arbitrary")),
        name=name,
    )(*args))
    out = []
    for s in segs:
        out.append(res[:len(s["out_dtypes"])])
        res = res[len(s["out_dtypes"]):]
    return out


def _outproj_kernel(a_ref, w_ref, h_ref, o_ref):
    o_ref[...] = h_ref[...] + jnp.dot(a_ref[...], w_ref[...], preferred_element_type=F32)


def _outproj(a, w, h, *, name):
    m, k = a.shape
    n = w.shape[1]
    tm = _tile(m, 1024)
    tn = _tile(n, 1024)
    return pl.pallas_call(
        _outproj_kernel,
        out_shape=jax.ShapeDtypeStruct((m, n), F32),
        grid=(m // tm, n // tn),
        in_specs=[pl.BlockSpec((tm, k), lambda i, j: (i, 0)),
                  pl.BlockSpec((k, tn), lambda i, j: (0, j)),
                  pl.BlockSpec((tm, tn), lambda i, j: (i, j))],
        out_specs=pl.BlockSpec((tm, tn), lambda i, j: (i, j)),
        compiler_params=_params(("parallel", "parallel")),
        name=name,
    )(a, w, h)


def _lambda(lamp_ref, lam_init):
    lp = lamp_ref[...]
    s1 = jnp.sum(lp[0:1, :] * lp[1:2, :], axis=-1, keepdims=True)
    s2 = jnp.sum(lp[2:3, :] * lp[3:4, :], axis=-1, keepdims=True)
    return jnp.exp(s1) - jnp.exp(s2) + lam_init


V_AUG = A_VD + 16


LOOKAHEAD = 5


def _attn_prompt_kernel(q_ref, qn_ref, k_ref, v_ref, g_ref, lamp_ref, sg_ref, o_ref,
                        qpad_ref, vt_ref, m_ref, acc_ref, s_ref, mx_ref, *, tq, tk, lam_init):
    i = pl.program_id(1)
    slot = i % 2

    sw = s_ref.shape[2]
    n_strips = 2 * tq // sw
    la = min(LOOKAHEAD, n_strips - 1)
    order = sorted(range(n_strips), key=lambda c: -((c * sw) % tq))

    def load_queries(src_ref, dst):
        qt = src_ref[...].astype(F32).T
        row = lax.broadcasted_iota(jnp.int32, qt.shape, 0)
        qpad_ref[dst, :, :tq] = jnp.where(row < A_HD, qt, 0.0).astype(BF16)
        qpad_ref[dst, :, tq:] = jnp.where(row >= A_HD, qt, 0.0).astype(BF16)

    def scores(j, c, rows=tk, src=slot):
        kj = k_ref[pl.ds(pl.multiple_of(j * tk, tk), rows), :]
        s = jnp.dot(kj, qpad_ref[src, :, c * sw:(c + 1) * sw], preferred_element_type=F32)
        s_ref[c, :rows] = s
        mx_ref[c] = jnp.max(s, axis=0, keepdims=True)

    @pl.when(i == 0)
    def _():
        aug = lax.broadcasted_iota(jnp.int32, (V_AUG - A_VD, tk), 0)
        ones_row = jnp.where(aug == 0, 1.0, 0.0).astype(BF16)

        def fill(t, carry):
            vj = v_ref[pl.ds(pl.multiple_of(t * tk, tk), tk), :].astype(F32)
            vt_ref[t, :A_VD, :] = vj.T.astype(BF16)
            vt_ref[t, A_VD:, :] = ones_row
            return carry

        lax.fori_loop(0, vt_ref.shape[0], fill, 0)
        load_queries(q_ref, 0)
        for c in order[:la]:
            scores(0, c, src=0)

    m_ref[...] = jnp.full(m_ref.shape, NEG_BIG, F32)
    acc_ref[...] = jnp.zeros(acc_ref.shape, F32)

    def update(j, c, mask, rows=tk):
        cs = slice(c * sw, (c + 1) * sw)
        s = s_ref[c, :rows]
        if mask is None:
            mx = mx_ref[c]
        else:
            s = jnp.where(mask[:rows], s, NEG_BIG)
            mx = jnp.max(s, axis=0, keepdims=True)
        m_old = m_ref[:, cs]
        m_new = jnp.maximum(m_old, mx)
        alpha = jnp.exp2(m_old - m_new)
        p = jnp.exp2(s - m_new).astype(BF16)
        pv = jnp.dot(vt_ref[j][:, :rows], p, preferred_element_type=F32)
        acc_ref[:, cs] = alpha * acc_ref[:, cs] + pv
        m_ref[:, cs] = m_new

    n_sub = tq // tk
    n_full = i * n_sub

    def full_tiles(j0, count):
        for j in range(j0, j0 + count) if isinstance(j0, int) else [j0 + t for t in range(count)]:
            for n, c in enumerate(order):
                update(j, c, None)
                if n + la < n_strips:
                    scores(j, order[n + la])
                else:
                    scores(j + 1, order[n + la - n_strips])

    odd = i % 2

    @pl.when(odd == 1)
    def _():
        full_tiles(0, n_sub)

    def body(jj, carry):
        full_tiles(odd * n_sub + jj * (2 * n_sub), 2 * n_sub)
        return carry

    lax.fori_loop(0, i // 2, body, 0)

    r = lax.broadcasted_iota(jnp.int32, (tk, sw), 0)
    cc = lax.broadcasted_iota(jnp.int32, (tk, sw), 1)
    units = []
    for d in range(n_sub):
        for c in order:
            q_lo, k_lo = (c * sw) % tq, d * tk
            if q_lo + sw <= k_lo:
                continue
            full = q_lo >= k_lo + tk
            rows = min(tk, q_lo + sw - k_lo)
            units.append((d, c, None if full else (k_lo + r) // CHUNK <= (q_lo + cc) // CHUNK, rows))
    assert [u[:2] + u[3:] for u in units[:la]] == [(0, c, tk) for c in order[:la]]
    load_queries(qn_ref, 1 - slot)
    for n, (d, c, mask, rows) in enumerate(units):
        update(n_full + d, c, mask, rows)
        if n + la < len(units):
            nd, nc, _, nrows = units[n + la]
            assert all(u[1] != nc for u in units[n + 1:n + la])
            scores(n_full + nd, nc, nrows)
    for c in order[:la]:
        scores(0, c, src=1 - slot)

    lam = _lambda(lamp_ref, lam_init)
    acc = acc_ref[...]
    inv_l = 1.0 / acc[A_VD:A_VD + 1, :]
    ot = acc[:A_VD, :tq] * inv_l[:, :tq] - lam * (acc[:A_VD, tq:] * inv_l[:, tq:])
    ms = jnp.mean(ot * ot, axis=0, keepdims=True)
    ot = ot * lax.rsqrt(ms + EPS) * (sg_ref[...] * (1.0 - lam_init))
    g = g_ref[...].astype(F32)
    o_ref[...] = (ot.T * (g * _sigmoid(g))).astype(o_ref.dtype)


def _attn_prompt(q, k, v, g, lamp, sub_gain, lam_init, *, name):
    s, w = q.shape
    heads = w // A_VD
    tq = _tile(s, 1024)
    tk = _tile(tq, 512)
    sw = min(MXU_COLS, tq)
    nq = s // tq
    return pl.pallas_call(
        functools.partial(_attn_prompt_kernel, tq=tq, tk=tk, lam_init=lam_init),
        out_shape=jax.ShapeDtypeStruct((s, w), BF16),
        grid=(heads, nq),
        in_specs=[pl.BlockSpec((tq, A_VD), lambda h, i: (i, h)),
                  pl.BlockSpec((tq, A_VD), lambda h, i: (jnp.minimum(i + 1, nq - 1), h)),
                  pl.BlockSpec((s, A_VD), lambda h, i: (0, h)),
                  pl.BlockSpec((s, A_VD), lambda h, i: (0, h)),
                  pl.BlockSpec((tq, A_VD), lambda h, i: (i, h)),
                  pl.BlockSpec((4, A_HD), lambda h, i: (0, 0)),
                  pl.BlockSpec((A_VD, 1), lambda h, i: (0, 0))],
        out_specs=pl.BlockSpec((tq, A_VD), lambda h, i: (i, h)),
        scratch_shapes=[pltpu.VMEM((2, A_VD, 2 * tq), BF16),
                        pltpu.VMEM((s // tk, V_AUG, tk), BF16),
                        pltpu.VMEM((1, 2 * tq), F32),
                        pltpu.VMEM((V_AUG, 2 * tq), F32),
                        pltpu.VMEM((2 * tq // sw, tk, sw), F32),
                        pltpu.VMEM((2 * tq // sw, 1, sw), F32)],
        compiler_params=_params(("parallel", "arbitrary")),
        name=name,
    )(q, q, k, v, g, lamp, sub_gain.reshape(A_VD, 1))


SAMPLE_HEADS_PER_STEP = 4


def _attn_sample_kernel(q_ref, kn_ref, vn_ref, g_ref, kc_ref, vc_ref, lamp_ref, sg_ref, o_ref, *, lam_init, hps):
    lam = _lambda(lamp_ref, lam_init)
    nt = (((1,), (1,)), ((), ()))
    lane = lax.broadcasted_iota(jnp.int32, (q_ref.shape[0], A_VD), 1)
    heads = [slice(h * A_VD, (h + 1) * A_VD) for h in range(hps)]
    scores = []
    for hs in heads:
        q = q_ref[:, hs]
        kc = kc_ref[0, :, hs].astype(BF16)
        kn = kn_ref[:, hs].astype(BF16)
        for half in range(2):
            qh = jnp.where((lane >= A_HD) == bool(half), q, jnp.zeros_like(q))
            scores.append((lax.dot_general(qh, kc, nt, preferred_element_type=F32),
                           lax.dot_general(qh, kn, nt, preferred_element_type=F32)))
    probs = []
    for sc, sn in scores:
        m = jnp.maximum(jnp.max(sc, axis=-1, keepdims=True), jnp.max(sn, axis=-1, keepdims=True))
        pc = jnp.exp2(sc - m)
        pn = jnp.exp2(sn - m)
        inv = 1.0 / (jnp.sum(pc, axis=-1, keepdims=True) + jnp.sum(pn, axis=-1, keepdims=True))
        probs.append((pc * inv, pn * inv))
    outs = []
    for h, hs in enumerate(heads):
        ac = (probs[2 * h][0] - lam * probs[2 * h + 1][0]).astype(BF16)
        an = (probs[2 * h][1] - lam * probs[2 * h + 1][1]).astype(BF16)
        outs.append(jnp.dot(ac, vc_ref[0, :, hs].astype(BF16), preferred_element_type=F32)
                    + jnp.dot(an, vn_ref[:, hs].astype(BF16), preferred_element_type=F32))
    for hs, o in zip(heads, outs):
        ms = jnp.mean(o * o, axis=-1, keepdims=True)
        g = g_ref[:, hs].astype(F32)
        o = o * lax.rsqrt(ms + EPS) * (sg_ref[...] * (1.0 - lam_init))
        o_ref[:, hs] = (o * (g * _sigmoid(g))).astype(o_ref.dtype)


def _attn_sample(q, k, v, g, kc, vc, layer, lamp, sub_gain, lam_init, *, name):
    _, bsz, past, w = kc.shape
    heads = w // A_VD
    ln = q.shape[0] // bsz
    assert past % CHUNK == 0 and ln <= CHUNK
    hps = SAMPLE_HEADS_PER_STEP
    assert heads % hps == 0
    row = lambda b, h: (b, h)
    cache = lambda b, h: (layer, b, 0, h)
    return pl.pallas_call(
        functools.partial(_attn_sample_kernel, lam_init=lam_init, hps=hps),
        out_shape=jax.ShapeDtypeStruct(q.shape, BF16),
        grid=(bsz, heads // hps),
        in_specs=[pl.BlockSpec((ln, hps * A_VD), row),
                  pl.BlockSpec((ln, hps * A_VD), row),
                  pl.BlockSpec((ln, hps * A_VD), row),
                  pl.BlockSpec((ln, hps * A_VD), row),
                  pl.BlockSpec((None, 1, past, hps * A_VD), cache),
                  pl.BlockSpec((None, 1, past, hps * A_VD), cache),
                  pl.BlockSpec((4, A_HD), lambda b, h: (0, 0)),
                  pl.BlockSpec((1, A_VD), lambda b, h: (0, 0))],
        out_specs=pl.BlockSpec((ln, hps * A_VD), row),
        compiler_params=_params(("parallel", "parallel")),
        name=name,
    )(q, k, v, g, kc, vc, lamp, sub_gain.reshape(1, A_VD))


RET_HEADS_PER_STEP = 8


def _retention_kernel(q_ref, k_ref, v_ref, g_ref, s0_ref, dec_ref, cd_ref, kd_ref, gl_ref,
                      o_ref, sout_ref, st_ref, *, lc, hps):
    c = pl.program_id(2)

    @pl.when(c == 0)
    def _():
        st_ref[...] = s0_ref[0]

    hd = range(hps)
    qs = [q_ref[:, h * R_DK:(h + 1) * R_DK] for h in hd]
    ks = [k_ref[:, h * R_DK:(h + 1) * R_DK] for h in hd]
    vs = [v_ref[:, h * R_DV:(h + 1) * R_DV] for h in hd]
    sts = [st_ref[h] for h in hd]
    nt = (((1,), (1,)), ((), ()))
    inner = [lax.dot_general(qs[h], ks[h], nt, preferred_element_type=F32) for h in hd]
    cross = [jnp.dot(qs[h], sts[h].astype(BF16), preferred_element_type=F32) for h in hd]
    kdec = []
    for h in hd:
        kd = ks[h].astype(F32) * jnp.concatenate([kd_ref[h]] * (R_DK // LANES), axis=1)
        if lc < LANES:
            kd = jnp.concatenate([kd, jnp.zeros((LANES - lc, R_DK), F32)], axis=0)
        kdec.append(kd.T.astype(BF16))
    inner = [(inner[h] * dec_ref[h]).astype(BF16) for h in hd]
    o = [jnp.dot(inner[h], vs[h], preferred_element_type=F32) for h in hd]
    upd = []
    for h in hd:
        vv = vs[h]
        if lc < LANES:
            vv = jnp.concatenate([vv, jnp.zeros((LANES - lc, R_DV), BF16)], axis=0)
        upd.append(jnp.dot(kdec[h], vv, preferred_element_type=F32))
    for h in hd:
        cd = jnp.concatenate([cd_ref[h]] * (R_DV // LANES), axis=1)
        oh = o[h] + cross[h] * cd
        ms = jnp.mean(oh * oh, axis=-1, keepdims=True)
        g = g_ref[:, h * R_DV:(h + 1) * R_DV].astype(F32)
        o_ref[:, h * R_DV:(h + 1) * R_DV] = (oh * lax.rsqrt(ms + EPS) * (g * _sigmoid(g))).astype(o_ref.dtype)
    st_new = []
    for h in hd:
        gl = jnp.concatenate([gl_ref[h]] * (R_DV // LANES), axis=1)
        st_new.append(sts[h] * gl + upd[h])
        st_ref[h] = st_new[h]

    @pl.when(c == pl.num_programs(2) - 1)
    def _():
        for h in hd:
            sout_ref[0, h] = st_new[h]


def _retention(q, k, vg, s0, lc, *, name):
    bsz, heads = s0.shape[:2]
    t = q.shape[0] // bsz
    nc = t // lc
    lg = jnp.log1p(-(2.0 ** (-5.0 - jnp.arange(heads, dtype=F32))))
    idx = jnp.arange(lc, dtype=F32)
    diff = idx[:, None] - idx[None, :]
    dec = jnp.where(diff >= 0, jnp.exp(lg[:, None, None] * jnp.maximum(diff, 0.0)), 0.0)
    cd = jnp.broadcast_to(jnp.exp(lg[:, None] * (idx[None, :] + 1.0))[:, :, None], (heads, lc, LANES))
    kd = jnp.broadcast_to(jnp.exp(lg[:, None] * (lc - 1.0 - idx[None, :]))[:, :, None], (heads, lc, LANES))
    gl = jnp.broadcast_to(jnp.exp(lg * lc)[:, None, None], (heads, 1, LANES))
    hps = RET_HEADS_PER_STEP
    assert heads % hps == 0
    rows = lambda b, h, c: (b * nc + c, h)
    gate = lambda b, h, c: (b * nc + c, heads // hps + h)
    tab = lambda b, h, c: (h, 0, 0)
    state = lambda b, h, c: (b, h, 0, 0)
    return pl.pallas_call(
        functools.partial(_retention_kernel, lc=lc, hps=hps),
        out_shape=(jax.ShapeDtypeStruct((vg.shape[0], heads * R_DV), BF16),
                   jax.ShapeDtypeStruct(s0.shape, F32)),
        grid=(bsz, heads // hps, nc),
        in_specs=[pl.BlockSpec((lc, hps * R_DK), rows),
                  pl.BlockSpec((lc, hps * R_DK), rows),
                  pl.BlockSpec((lc, hps * R_DV), rows),
                  pl.BlockSpec((lc, hps * R_DV), gate),
                  pl.BlockSpec((1, hps, R_DK, R_DV), state),
                  pl.BlockSpec((hps, lc, lc), tab),
                  pl.BlockSpec((hps, lc, LANES), tab),
                  pl.BlockSpec((hps, lc, LANES), tab),
                  pl.BlockSpec((hps, 1, LANES), tab)],
        out_specs=(pl.BlockSpec((lc, hps * R_DV), rows),
                   pl.BlockSpec((1, hps, R_DK, R_DV), state)),
        scratch_shapes=[pltpu.VMEM((hps, R_DK, R_DV), F32)],
        compiler_params=_params(("parallel", "parallel", "arbitrary")),
        name=name,
    )(q, k, vg, vg, s0, dec, cd, kd, gl)


def _cmlp_kernel(u_ref, g_ref, v_ref, vg_ref, w_ref, b_ref, *out_refs, emit_v):
    if emit_v:
        a_ref, vn_ref = out_refs
    else:
        (a_ref,) = out_refs
    va = _gelu(v_ref[...])
    ms = jnp.mean(va * va, axis=-1, keepdims=True)
    vn = va * lax.rsqrt(ms + EPS) * vg_ref[...]
    if emit_v:
        vn_ref[...] = vn
    vb = vn.astype(BF16)
    gd = vn.shape[1] // M_GROUPS
    for grp in range(M_GROUPS):
        sl = slice(grp * gd, (grp + 1) * gd)
        bias = jnp.concatenate([b_ref[grp]] * (gd // LANES), axis=1)
        mix = jnp.dot(w_ref[grp], vb[:, sl], preferred_element_type=F32) + bias
        g = g_ref[:, sl].astype(F32)
        a_ref[:, sl] = (_gelu(u_ref[:, sl].astype(F32)) * mix * (g * _sigmoid(g))).astype(a_ref.dtype)


def _cmlp(ug, v, v_gain, wmix, bmix, *, emit_v, name):
    m, w = v.shape
    t = wmix.shape[1]
    out_shape = [jax.ShapeDtypeStruct((m, w), BF16)]
    out_specs = [pl.BlockSpec((t, w), lambda i: (i, 0))]
    if emit_v:
        out_shape.append(jax.ShapeDtypeStruct((m, w), F32))
        out_specs.append(pl.BlockSpec((t, w), lambda i: (i, 0)))
    res = pl.pallas_call(
        functools.partial(_cmlp_kernel, emit_v=emit_v),
        out_shape=tuple(out_shape),
        grid=(m // t,),
        in_specs=[pl.BlockSpec((t, w), lambda i: (i, 0)),
                  pl.BlockSpec((t, w), lambda i: (i, 1)),
                  pl.BlockSpec((t, w), lambda i: (i, 0)),
                  pl.BlockSpec((1, w), lambda i: (0, 0)),
                  pl.BlockSpec((M_GROUPS, t, t), lambda i: (0, 0, 0)),
                  pl.BlockSpec((M_GROUPS, t, LANES), lambda i: (0, 0, 0))],
        out_specs=tuple(out_specs),
        compiler_params=_params(("parallel",)),
        name=name,
    )(ug, ug, v, v_gain.reshape(1, w), wmix, bmix)
    return res if emit_v else res[0]


def _rope_tables(pos, d, signed):
    inv = ROPE_THETA ** (-jnp.arange(0, d, 2, dtype=F32) / d)
    ang = pos.astype(F32)[:, None] * inv[None, :]
    cos, sin = jnp.cos(ang), jnp.sin(ang)
    reps = LANES // (d // 2)
    if signed:
        return (jnp.tile(cos, (1, reps)),
                jnp.tile(jnp.concatenate([-sin, sin], axis=1), (1, reps // 2)))
    return jnp.tile(cos, (1, reps)), jnp.tile(sin, (1, reps))


def _diff_attn_layer(h, pos, nw, w_in, w_out, q_gain, k_gain, lamp, sub_gain, lam_init,
                     layers, layer, kbuf, vbuf, cache, tag):
    w = w_out.shape[0]
    cos, sin = _rope_tables(pos, A_HD, signed=True)
    wq, wk, wv, wg = (w_in[:, i * w:(i + 1) * w].astype(BF16) for i in range(4))
    (q,), (k, kb), (v, vb), (g,) = _proj(h, nw, [
        _seg(wq, (BF16,), "qk", (A_HD ** -0.5) * LOG2E, q_gain),
        _seg(wk, (F32, BF16), "qk", gain=k_gain, stack=(layers, layer, kbuf)),
        _seg(wv, (F32, BF16), stack=(layers, layer, vbuf)),
        _seg(wg, (BF16,))], cos=cos, sin=sin, name=f"{tag}_proj")
    if cache is None:
        o = _attn_prompt(q, kb, vb, g, lamp, sub_gain, lam_init, name=f"{tag}_attn")
    else:
        kc, vc = cache
        o = _attn_sample(q, kb, vb, g, kc, vc, layer, lamp, sub_gain, lam_init, name=f"{tag}_attn")
    return _outproj(o, w_out.astype(BF16), h, name=f"{tag}_out"), k, v


def _retention_layer(h, pos, nw, w_in, w_out, s0, lc, tag):
    heads = s0.shape[1]
    qk_w, v_w = heads * R_DK, heads * R_DV
    cos, sin = _rope_tables(pos, R_DK, signed=False)
    wq = w_in[:, :qk_w].astype(BF16)
    wk = w_in[:, qk_w:2 * qk_w].astype(BF16)
    assert w_in.shape[1] == 2 * qk_w + 2 * v_w
    wvg = w_in[:, 2 * qk_w:].astype(BF16)
    (q,), (k,), (vg,) = _proj(h, nw, [
        _seg(wq, (BF16,), "rope256"),
        _seg(wk, (BF16,), "rope256", R_DK ** -0.5),
        _seg(wvg, (BF16,))], cos=cos, sin=sin, name=f"{tag}_proj")
    o, s_new = _retention(q, k, vg, s0, lc, name=f"{tag}_ret")
    return _outproj(o, w_out.astype(BF16), h, name=f"{tag}_out"), s_new


def _cmlp_layer(h, nw, w_in, w_out, v_gain, wmix, bmix, emit_v, tag):
    w = w_out.shape[0]
    wu, wv, wg = (w_in[:, i * w:(i + 1) * w].astype(BF16) for i in range(3))
    (ug,), (v,) = _proj(h, nw, [_seg(jnp.concatenate([wu, wg], axis=1), (BF16,)), _seg(wv, (F32,))],
                        name=f"{tag}_proj")
    res = _cmlp(ug, v, v_gain, wmix, bmix, emit_v=emit_v, name=f"{tag}_mix")
    a, vn = res if emit_v else (res, None)
    return _outproj(a, w_out.astype(BF16), h, name=f"{tag}_out"), vn


def _mix_tables(w_s, b_s, chunk_len):
    groups = w_s.shape[0]
    wl = jnp.tril(w_s[:, :chunk_len, :chunk_len])
    reps = M_CHUNK // chunk_len
    eye = jnp.eye(reps, dtype=F32)
    wt = jnp.einsum("ab,gij->gaibj", eye, wl).reshape(groups, M_CHUNK, M_CHUNK)
    bt = jnp.tile(b_s[:, :chunk_len], (1, reps))
    return wt.astype(BF16), jnp.broadcast_to(bt[:, :, None], (groups, M_CHUNK, LANES))


def kernel(x_prompt, x_sample, cache_k_attn, cache_v_attn, state_ret, norm_w, a_w_in, a_w_out, a_q_gain, a_k_gain, a_lam_q1, a_lam_k1, a_lam_q2, a_lam_k2, a_sub_gain, r_w_in, r_w_out, c_w_in, c_w_out, c_v_gain, c_w_s, c_b_s):
    batch, s_len, d = x_prompt.shape
    dec_b, dec_len, _ = x_sample.shape
    past = cache_k_attn.shape[2]
    depth = norm_w.shape[0]
    assert batch == 1 and M_CHUNK % dec_len == 0 and s_len % M_CHUNK == 0

    hp = x_prompt.reshape(s_len, d)
    hs = x_sample.reshape(dec_b * dec_len, d)
    pos_p = jnp.arange(s_len, dtype=jnp.int32)
    pos_s = jnp.tile(past + jnp.arange(dec_len, dtype=jnp.int32), dec_b)

    n_a = a_w_in.shape[0]
    aw = a_w_out.shape[1]
    cache = (cache_k_attn.reshape(n_a, dec_b, past, aw), cache_v_attn.reshape(n_a, dec_b, past, aw))
    kp = vp = kn = vn = None
    sp_l, ss_l, vm_l = [], [], []
    for i in range(depth):
        kind, j = i % N_MIXERS, i // N_MIXERS
        hp, hs = lax.optimization_barrier((hp, hs))
        if kind == 0:
            lam_init = 0.8 - 0.6 * math.exp(-0.3 * i)
            lamp = jnp.stack([a_lam_q1[j], a_lam_k1[j], a_lam_q2[j], a_lam_k2[j]])
            args = (norm_w[i], a_w_in[j], a_w_out[j], a_q_gain[j], a_k_gain[j], lamp, a_sub_gain[j], lam_init)
            hp, kp, vp = _diff_attn_layer(hp, pos_p, *args, n_a, j, kp, vp, None, f"l{i}p")
            hp, hs = lax.optimization_barrier((hp, hs))
            hs, kn, vn = _diff_attn_layer(hs, pos_s, *args, n_a, j, kn, vn, cache, f"l{i}s")
        elif kind == 1:
            heads = state_ret.shape[2]
            s0 = jnp.zeros((batch, heads, R_DK, R_DV), F32)
            hp, st_p = _retention_layer(hp, pos_p, norm_w[i], r_w_in[j], r_w_out[j], s0,
                                        _tile(s_len, 256), f"l{i}p")
            hs, st_s = _retention_layer(hs, pos_s, norm_w[i], r_w_in[j], r_w_out[j],
                                        state_ret[j].astype(F32), dec_len, f"l{i}s")
            sp_l.append(st_p)
            ss_l.append(st_s)
        else:
            wp, bp = _mix_tables(c_w_s[j], c_b_s[j], M_CHUNK)
            ws, bs = _mix_tables(c_w_s[j], c_b_s[j], dec_len)
            hp, _ = _cmlp_layer(hp, norm_w[i], c_w_in[j], c_w_out[j], c_v_gain[j], wp, bp, False, f"l{i}p")
            hs, v_s = _cmlp_layer(hs, norm_w[i], c_w_in[j], c_w_out[j], c_v_gain[j], ws, bs, True, f"l{i}s")
            vm_l.append(v_s.reshape(dec_b, dec_len, -1))

    return (hp.reshape(batch, s_len, d), hs.reshape(dec_b, dec_len, d),
            kp.reshape(n_a, batch, s_len, aw // A_HD, A_HD), vp.reshape(n_a, batch, s_len, aw // A_VD, A_VD),
            kn.reshape(n_a, dec_b, dec_len, aw // A_HD, A_HD), vn.reshape(n_a, dec_b, dec_len, aw // A_VD, A_VD),
            jnp.stack(sp_l), jnp.stack(ss_l), jnp.stack(vm_l))
```

```python
import functools
import math

import jax
import jax.numpy as jnp
from jax import lax
from jax.experimental import pallas as pl
from jax.experimental.pallas import tpu as pltpu

F32 = jnp.float32
BF16 = jnp.bfloat16

EPS = 1e-6
CHUNK = 64
ROPE_THETA = 10000.0
N_MIXERS = 3
A_HD = 64
A_VD = 2 * A_HD
R_DK = 256
R_DV = 2 * R_DK
M_GROUPS = 8
M_CHUNK = 128
LOG2E = 1.4426950408889634

LANES = 128
MXU_COLS = 256
V7X_VMEM_BYTES = 64 * 1024 * 1024
VMEM_LIMIT = V7X_VMEM_BYTES * 7 // 8
NEG_BIG = -1e30


def _tile(n, pref):
    if n <= pref:
        return n
    t = pref
    while t >= 8:
        if n % t == 0:
            return t
        t -= 8
    return n


def _params(sem):
    return pltpu.CompilerParams(dimension_semantics=sem, vmem_limit_bytes=VMEM_LIMIT)


def _nbytes(*arrays):
    return sum(a.size * jnp.dtype(a.dtype).itemsize for a in arrays)


def _cost(flops, nbytes, transcendentals=0):
    return pl.CostEstimate(flops=int(flops), transcendentals=int(transcendentals), bytes_accessed=int(nbytes))


def _gelu(x):
    return 0.5 * x * (1.0 + jnp.tanh(0.7978845608028654 * (x + 0.044715 * (x * x * x))))


def _sigmoid(x):
    return 1.0 / (1.0 + jnp.exp(-x))


def _proj_kernel(*refs, plan):
    xn_ref = refs[-1]
    it = iter(refs)
    x_ref, nw_ref = next(it), next(it)
    epis = [p[0] for p in plan]
    cos_ref, sin_ref = (next(it), next(it)) if any(e != "none" for e in epis) else (None, None)
    gsum_ref = next(it) if "qk" in epis else None
    w_refs, gain_refs = [], []
    for e in epis:
        w_refs.append(next(it))
        gain_refs.append(next(it) if e == "qk" else None)
    outs = list(refs[len(refs) - 1 - sum(p[2] for p in plan):-1])
    o_refs = []
    for p in plan:
        o_refs.append(outs[:p[2]])
        outs = outs[p[2]:]

    def store(seg, cols, val):
        scale = plan[seg][1]
        if scale != 1.0:
            val = val * scale
        for o_ref in o_refs[seg]:
            o_ref[:, cols] = val.astype(o_ref.dtype)

    @pl.when(pl.program_id(1) == 0)
    def _():
        x = x_ref[...]
        ms = jnp.mean(x * x, axis=-1, keepdims=True)
        xn_ref[...] = (x * lax.rsqrt(ms + EPS) * nw_ref[...]).astype(BF16)

    qk = [s for s, e in enumerate(epis) if e == "qk"]
    rest = [s for s, e in enumerate(epis) if e == "rope256"] + [s for s, e in enumerate(epis) if e == "none"]
    slabs = {s: [slice(c * MXU_COLS, (c + 1) * MXU_COLS) for c in range(w_refs[s].shape[1] // MXU_COLS)]
             for s in qk}
    zq = {(s, c): jnp.dot(xn_ref[...], w_refs[s][:, sl], preferred_element_type=F32)
          for s in qk for c, sl in enumerate(slabs[s])}
    sq = {key: jnp.dot((z * z).astype(BF16), gsum_ref[...], preferred_element_type=F32) for key, z in zq.items()}
    zr = {s: jnp.dot(xn_ref[...], w_refs[s][...], preferred_element_type=F32) for s in rest}

    if qk:
        cos = jnp.concatenate([cos_ref[...]] * 2, axis=1)
        sin = jnp.concatenate([sin_ref[...]] * 2, axis=1)
        lane = lax.broadcasted_iota(jnp.int32, (xn_ref.shape[0], MXU_COLS), 1)
        first_half = (lane % A_HD) < (A_HD // 2)
        for (s, c), z in zq.items():
            gain = jnp.concatenate([gain_refs[s][...]] * 2, axis=1)
            zn = z * lax.rsqrt(sq[(s, c)] * (1.0 / A_HD) + EPS) * gain
            partner = jnp.where(first_half,
                                pltpu.roll(zn, MXU_COLS - A_HD // 2, 1),
                                pltpu.roll(zn, A_HD // 2, 1))
            store(s, slabs[s][c], zn * cos + partner * sin)
    for s in rest:
        z = zr[s]
        if epis[s] == "rope256":
            cos = cos_ref[...]
            sin = sin_ref[...]
            for c in range(z.shape[1] // R_DK):
                lo, hi = slice(c * R_DK, c * R_DK + LANES), slice(c * R_DK + LANES, (c + 1) * R_DK)
                x1, x2 = z[:, lo], z[:, hi]
                store(s, lo, x1 * cos - x2 * sin)
                store(s, hi, x2 * cos + x1 * sin)
        else:
            store(s, slice(None), z)


def _seg(w, out_dtypes, epi="none", out_scale=1.0, gain=None, stack=None):
    return dict(w=w, out_dtypes=out_dtypes, epi=epi, out_scale=out_scale, gain=gain, stack=stack)


PROJ_VMEM_BUDGET = V7X_VMEM_BYTES * 3 // 4


def _proj(x, nw, segs, *, cos=None, sin=None, name):
    m, d = x.shape
    tm = _tile(m, 1024)
    epis = [s["epi"] for s in segs]
    assert not ("qk" in epis and "rope256" in epis)
    unit = {"qk": MXU_COLS, "rope256": R_DK, "none": LANES}

    def vmem_bytes(nj):
        total = 2 * tm * d * 4 + tm * d * 2
        for s in segs:
            tn = s["w"].shape[1] // nj
            total += 2 * d * tn * 2 + tm * tn * 4
            total += sum(2 * tm * tn * jnp.dtype(dt).itemsize for dt in s["out_dtypes"])
        return total

    nj = next(c for c in (1, 2, 4, 8, 16, 32)
              if all(s["w"].shape[1] % (c * unit[s["epi"]]) == 0 for s in segs) and vmem_bytes(c) <= PROJ_VMEM_BUDGET)
    in_specs = [pl.BlockSpec((tm, d), lambda i, j: (i, 0)),
                pl.BlockSpec((1, d), lambda i, j: (0, 0))]
    args = [x, nw.reshape(1, d)]
    if any(e != "none" for e in epis):
        in_specs += [pl.BlockSpec((tm, LANES), lambda i, j: (i, 0))] * 2
        args += [cos, sin]
    if "qk" in epis:
        gidx = jnp.arange(MXU_COLS) // A_HD
        in_specs.append(pl.BlockSpec((MXU_COLS, MXU_COLS), lambda i, j: (0, 0)))
        args.append((gidx[:, None] == gidx[None, :]).astype(BF16))
    out_shape, out_specs, bufs = [], [], []
    for s in segs:
        n = s["w"].shape[1]
        tn = n // nj
        in_specs.append(pl.BlockSpec((d, tn), lambda i, j: (0, j)))
        args.append(s["w"])
        if s["epi"] == "qk":
            in_specs.append(pl.BlockSpec((1, LANES), lambda i, j: (0, 0)))
            args.append(jnp.tile(s["gain"].reshape(1, A_HD), (1, LANES // A_HD)))
        for o, dt in enumerate(s["out_dtypes"]):
            if o == 0 and s["stack"] is not None:
                layers, layer, buf = s["stack"]
                out_specs.append(pl.BlockSpec((None, tm, tn), lambda i, j, layer=layer: (layer, i, j)))
                out_shape.append(jax.ShapeDtypeStruct((layers, m, n), dt))
                if buf is not None:
                    bufs.append((len(out_shape) - 1, buf))
            else:
                out_specs.append(pl.BlockSpec((tm, tn), lambda i, j: (i, j)))
                out_shape.append(jax.ShapeDtypeStruct((m, n), dt))
    aliases = {}
    for out_idx, buf in bufs:
        aliases[len(args)] = out_idx
        in_specs.append(pl.BlockSpec(memory_space=pl.ANY))
        args.append(buf)
    res = list(pl.pallas_call(
        functools.partial(_proj_kernel, plan=tuple((s["epi"], s["out_scale"], len(s["out_dtypes"])) for s in segs)),
        out_shape=tuple(out_shape),
        grid=(m // tm, nj),
        in_specs=in_specs,
        out_specs=tuple(out_specs),
        scratch_shapes=[pltpu.VMEM((tm, d), BF16)],
        input_output_aliases=aliases,
        compiler_params=_params(("parallel", "arbitrary")),
        cost_estimate=_cost(2 * m * d * sum(s["w"].shape[1] for s in segs),
                            _nbytes(x) + (m // tm) * _nbytes(*(s["w"] for s in segs))
                            + sum(m * s["w"].shape[1] * jnp.dtype(dt).itemsize for s in segs for dt in s["out_dtypes"])),
        name=name,
    )(*args))
    out = []
    for s in segs:
        out.append(res[:len(s["out_dtypes"])])
        res = res[len(s["out_dtypes"]):]
    return out


def _outproj_kernel(a_ref, w_ref, h_ref, o_ref):
    o_ref[...] = h_ref[...] + jnp.dot(a_ref[...], w_ref[...], preferred_element_type=F32)


def _outproj(a, w, h, *, name):
    m, k = a.shape
    n = w.shape[1]
    tm = _tile(m, 1024)
    tn = _tile(n, 1024)
    return pl.pallas_call(
        _outproj_kernel,
        out_shape=jax.ShapeDtypeStruct((m, n), F32),
        grid=(m // tm, n // tn),
        in_specs=[pl.BlockSpec((tm, k), lambda i, j: (i, 0)),
                  pl.BlockSpec((k, tn), lambda i, j: (0, j)),
                  pl.BlockSpec((tm, tn), lambda i, j: (i, j))],
        out_specs=pl.BlockSpec((tm, tn), lambda i, j: (i, j)),
        compiler_params=_params(("parallel", "parallel")),
        cost_estimate=_cost(2 * m * k * n, _nbytes(a) + (m // tm) * _nbytes(w) + 2 * _nbytes(h)),
        name=name,
    )(a, w, h)


def _lambda(lamp_ref, lam_init):
    lp = lamp_ref[...]
    s1 = jnp.sum(lp[0:1, :] * lp[1:2, :], axis=-1, keepdims=True)
    s2 = jnp.sum(lp[2:3, :] * lp[3:4, :], axis=-1, keepdims=True)
    return jnp.exp(s1) - jnp.exp(s2) + lam_init


V_AUG = A_VD + 16


LOOKAHEAD = 5


def _attn_prompt_kernel(q_ref, qn_ref, k_ref, v_ref, g_ref, lamp_ref, sg_ref, o_ref,
                        qpad_ref, vt_ref, m_ref, acc_ref, s_ref, mx_ref, *, tq, tk, lam_init):
    i = pl.program_id(1)
    slot = i % 2

    sw = s_ref.shape[2]
    n_strips = 2 * tq // sw
    la = min(LOOKAHEAD, n_strips - 1)
    order = sorted(range(n_strips), key=lambda c: -((c * sw) % tq))

    def load_queries(src_ref, dst):
        qt = src_ref[...].astype(F32).T
        row = lax.broadcasted_iota(jnp.int32, qt.shape, 0)
        qpad_ref[dst, :, :tq] = jnp.where(row < A_HD, qt, 0.0).astype(BF16)
        qpad_ref[dst, :, tq:] = jnp.where(row >= A_HD, qt, 0.0).astype(BF16)

    def scores(j, c, rows=tk, src=slot):
        kj = k_ref[pl.ds(pl.multiple_of(j * tk, tk), rows), :]
        s = jnp.dot(kj, qpad_ref[src, :, c * sw:(c + 1) * sw], preferred_element_type=F32)
        s_ref[c, :rows] = s
        mx_ref[c] = jnp.max(s, axis=0, keepdims=True)

    @pl.when(i == 0)
    def _():
        aug = lax.broadcasted_iota(jnp.int32, (V_AUG - A_VD, tk), 0)
        ones_row = jnp.where(aug == 0, 1.0, 0.0).astype(BF16)

        def fill(t, carry):
            vj = v_ref[pl.ds(pl.multiple_of(t * tk, tk), tk), :].astype(F32)
            vt_ref[t, :A_VD, :] = vj.T.astype(BF16)
            vt_ref[t, A_VD:, :] = ones_row
            return carry

        lax.fori_loop(0, vt_ref.shape[0], fill, 0)
        load_queries(q_ref, 0)
        for c in order[:la]:
            scores(0, c, src=0)

    m_ref[...] = jnp.full(m_ref.shape, NEG_BIG, F32)
    acc_ref[...] = jnp.zeros(acc_ref.shape, F32)

    def update(j, c, mask, rows=tk):
        cs = slice(c * sw, (c + 1) * sw)
        s = s_ref[c, :rows]
        if mask is None:
            mx = mx_ref[c]
        else:
            s = jnp.where(mask[:rows], s, NEG_BIG)
            mx = jnp.max(s, axis=0, keepdims=True)
        m_old = m_ref[:, cs]
        m_new = jnp.maximum(m_old, mx)
        alpha = jnp.exp2(m_old - m_new)
        p = jnp.exp2(s - m_new).astype(BF16)
        pv = jnp.dot(vt_ref[j][:, :rows], p, preferred_element_type=F32)
        acc_ref[:, cs] = alpha * acc_ref[:, cs] + pv
        m_ref[:, cs] = m_new

    n_sub = tq // tk
    n_full = i * n_sub

    def full_tiles(j0, count):
        for j in range(j0, j0 + count) if isinstance(j0, int) else [j0 + t for t in range(count)]:
            for n, c in enumerate(order):
                update(j, c, None)
                if n + la < n_strips:
                    scores(j, order[n + la])
                else:
                    scores(j + 1, order[n + la - n_strips])

    odd = i % 2

    @pl.when(odd == 1)
    def _():
        full_tiles(0, n_sub)

    def body(jj, carry):
        full_tiles(odd * n_sub + jj * (2 * n_sub), 2 * n_sub)
        return carry

    lax.fori_loop(0, i // 2, body, 0)

    r = lax.broadcasted_iota(jnp.int32, (tk, sw), 0)
    cc = lax.broadcasted_iota(jnp.int32, (tk, sw), 1)
    units = []
    for d in range(n_sub):
        for c in order:
            q_lo, k_lo = (c * sw) % tq, d * tk
            if q_lo + sw <= k_lo:
                continue
            full = q_lo >= k_lo + tk
            rows = min(tk, q_lo + sw - k_lo)
            units.append((d, c, None if full else (k_lo + r) // CHUNK <= (q_lo + cc) // CHUNK, rows))
    assert [u[:2] + u[3:] for u in units[:la]] == [(0, c, tk) for c in order[:la]]
    load_queries(qn_ref, 1 - slot)
    for n, (d, c, mask, rows) in enumerate(units):
        update(n_full + d, c, mask, rows)
        if n + la < len(units):
            nd, nc, _, nrows = units[n + la]
            assert all(u[1] != nc for u in units[n + 1:n + la])
            scores(n_full + nd, nc, nrows)
    for c in order[:la]:
        scores(0, c, src=1 - slot)

    lam = _lambda(lamp_ref, lam_init)
    acc = acc_ref[...]
    inv_l = 1.0 / acc[A_VD:A_VD + 1, :]
    ot = acc[:A_VD, :tq] * inv_l[:, :tq] - lam * (acc[:A_VD, tq:] * inv_l[:, tq:])
    ms = jnp.mean(ot * ot, axis=0, keepdims=True)
    ot = ot * lax.rsqrt(ms + EPS) * (sg_ref[...] * (1.0 - lam_init))
    g = g_ref[...].astype(F32)
    o_ref[...] = (ot.T * (g * _sigmoid(g))).astype(o_ref.dtype)


def _attn_prompt(q, k, v, g, lamp, sub_gain, lam_init, *, name):
    s, w = q.shape
    heads = w // A_VD
    tq = _tile(s, 1024)
    tk = _tile(tq, 512)
    sw = min(MXU_COLS, tq)
    nq = s // tq
    return pl.pallas_call(
        functools.partial(_attn_prompt_kernel, tq=tq, tk=tk, lam_init=lam_init),
        out_shape=jax.ShapeDtypeStruct((s, w), BF16),
        grid=(heads, nq),
        in_specs=[pl.BlockSpec((tq, A_VD), lambda h, i: (i, h)),
                  pl.BlockSpec((tq, A_VD), lambda h, i: (jnp.minimum(i + 1, nq - 1), h)),
                  pl.BlockSpec((s, A_VD), lambda h, i: (0, h)),
                  pl.BlockSpec((s, A_VD), lambda h, i: (0, h)),
                  pl.BlockSpec((tq, A_VD), lambda h, i: (i, h)),
                  pl.BlockSpec((4, A_HD), lambda h, i: (0, 0)),
                  pl.BlockSpec((A_VD, 1), lambda h, i: (0, 0))],
        out_specs=pl.BlockSpec((tq, A_VD), lambda h, i: (i, h)),
        scratch_shapes=[pltpu.VMEM((2, A_VD, 2 * tq), BF16),
                        pltpu.VMEM((s // tk, V_AUG, tk), BF16),
                        pltpu.VMEM((1, 2 * tq), F32),
                        pltpu.VMEM((V_AUG, 2 * tq), F32),
                        pltpu.VMEM((2 * tq // sw, tk, sw), F32),
                        pltpu.VMEM((2 * tq // sw, 1, sw), F32)],
        compiler_params=_params(("parallel", "arbitrary")),
        cost_estimate=_cost(heads * (s * s // 2) * 2 * 2 * (A_VD + V_AUG), _nbytes(q, k, v, g, q),
                            heads * (s * s // 2) * 2),
        name=name,
    )(q, q, k, v, g, lamp, sub_gain.reshape(A_VD, 1))


SAMPLE_HEADS_PER_STEP = 4


def _attn_sample_kernel(q_ref, kn_ref, vn_ref, g_ref, kc_ref, vc_ref, lamp_ref, sg_ref, o_ref, *, lam_init, hps):
    lam = _lambda(lamp_ref, lam_init)
    nt = (((1,), (1,)), ((), ()))
    lane = lax.broadcasted_iota(jnp.int32, (q_ref.shape[0], A_VD), 1)
    heads = [slice(h * A_VD, (h + 1) * A_VD) for h in range(hps)]
    scores = []
    for hs in heads:
        q = q_ref[:, hs]
        kc = kc_ref[0, :, hs].astype(BF16)
        kn = kn_ref[:, hs].astype(BF16)
        for half in range(2):
            qh = jnp.where((lane >= A_HD) == bool(half), q, jnp.zeros_like(q))
            scores.append((lax.dot_general(qh, kc, nt, preferred_element_type=F32),
                           lax.dot_general(qh, kn, nt, preferred_element_type=F32)))
    probs = []
    for sc, sn in scores:
        m = jnp.maximum(jnp.max(sc, axis=-1, keepdims=True), jnp.max(sn, axis=-1, keepdims=True))
        pc = jnp.exp2(sc - m)
        pn = jnp.exp2(sn - m)
        inv = 1.0 / (jnp.sum(pc, axis=-1, keepdims=True) + jnp.sum(pn, axis=-1, keepdims=True))
        probs.append((pc * inv, pn * inv))
    outs = []
    for h, hs in enumerate(heads):
        ac = (probs[2 * h][0] - lam * probs[2 * h + 1][0]).astype(BF16)
        an = (probs[2 * h][1] - lam * probs[2 * h + 1][1]).astype(BF16)
        outs.append(jnp.dot(ac, vc_ref[0, :, hs].astype(BF16), preferred_element_type=F32)
                    + jnp.dot(an, vn_ref[:, hs].astype(BF16), preferred_element_type=F32))
    for hs, o in zip(heads, outs):
        ms = jnp.mean(o * o, axis=-1, keepdims=True)
        g = g_ref[:, hs].astype(F32)
        o = o * lax.rsqrt(ms + EPS) * (sg_ref[...] * (1.0 - lam_init))
        o_ref[:, hs] = (o * (g * _sigmoid(g))).astype(o_ref.dtype)


def _attn_sample(q, k, v, g, kc, vc, layer, lamp, sub_gain, lam_init, *, name):
    _, bsz, past, w = kc.shape
    heads = w // A_VD
    ln = q.shape[0] // bsz
    assert past % CHUNK == 0 and ln <= CHUNK
    hps = SAMPLE_HEADS_PER_STEP
    assert heads % hps == 0
    row = lambda b, h: (b, h)
    cache = lambda b, h: (layer, b, 0, h)
    return pl.pallas_call(
        functools.partial(_attn_sample_kernel, lam_init=lam_init, hps=hps),
        out_shape=jax.ShapeDtypeStruct(q.shape, BF16),
        grid=(bsz, heads // hps),
        in_specs=[pl.BlockSpec((ln, hps * A_VD), row),
                  pl.BlockSpec((ln, hps * A_VD), row),
                  pl.BlockSpec((ln, hps * A_VD), row),
                  pl.BlockSpec((ln, hps * A_VD), row),
                  pl.BlockSpec((None, 1, past, hps * A_VD), cache),
                  pl.BlockSpec((None, 1, past, hps * A_VD), cache),
                  pl.BlockSpec((4, A_HD), lambda b, h: (0, 0)),
                  pl.BlockSpec((1, A_VD), lambda b, h: (0, 0))],
        out_specs=pl.BlockSpec((ln, hps * A_VD), row),
        compiler_params=_params(("parallel", "parallel")),
        cost_estimate=_cost(bsz * heads * ln * (past + ln) * 2 * 2 * 2 * A_VD,
                            _nbytes(q, k, v, g, q) + 2 * bsz * past * w * jnp.dtype(kc.dtype).itemsize,
                            bsz * heads * ln * (past + ln) * 2),
        name=name,
    )(q, k, v, g, kc, vc, lamp, sub_gain.reshape(1, A_VD))


RET_HEADS_PER_STEP = 8


def _retention_kernel(q_ref, k_ref, v_ref, g_ref, s0_ref, dec_ref, cd_ref, kd_ref, gl_ref,
                      o_ref, sout_ref, st_ref, *, lc, hps):
    c = pl.program_id(2)

    @pl.when(c == 0)
    def _():
        st_ref[...] = s0_ref[0]

    hd = range(hps)
    qs = [q_ref[:, h * R_DK:(h + 1) * R_DK] for h in hd]
    ks = [k_ref[:, h * R_DK:(h + 1) * R_DK] for h in hd]
    vs = [v_ref[:, h * R_DV:(h + 1) * R_DV] for h in hd]
    sts = [st_ref[h] for h in hd]
    nt = (((1,), (1,)), ((), ()))
    inner = [lax.dot_general(qs[h], ks[h], nt, preferred_element_type=F32) for h in hd]
    cross = [jnp.dot(qs[h], sts[h].astype(BF16), preferred_element_type=F32) for h in hd]
    kdec = []
    for h in hd:
        kd = ks[h].astype(F32) * jnp.concatenate([kd_ref[h]] * (R_DK // LANES), axis=1)
        if lc < LANES:
            kd = jnp.concatenate([kd, jnp.zeros((LANES - lc, R_DK), F32)], axis=0)
        kdec.append(kd.T.astype(BF16))
    inner = [(inner[h] * dec_ref[h]).astype(BF16) for h in hd]
    o = [jnp.dot(inner[h], vs[h], preferred_element_type=F32) for h in hd]
    upd = []
    for h in hd:
        vv = vs[h]
        if lc < LANES:
            vv = jnp.concatenate([vv, jnp.zeros((LANES - lc, R_DV), BF16)], axis=0)
        upd.append(jnp.dot(kdec[h], vv, preferred_element_type=F32))
    for h in hd:
        cd = jnp.concatenate([cd_ref[h]] * (R_DV // LANES), axis=1)
        oh = o[h] + cross[h] * cd
        ms = jnp.mean(oh * oh, axis=-1, keepdims=True)
        g = g_ref[:, h * R_DV:(h + 1) * R_DV].astype(F32)
        o_ref[:, h * R_DV:(h + 1) * R_DV] = (oh * lax.rsqrt(ms + EPS) * (g * _sigmoid(g))).astype(o_ref.dtype)
    st_new = []
    for h in hd:
        gl = jnp.concatenate([gl_ref[h]] * (R_DV // LANES), axis=1)
        st_new.append(sts[h] * gl + upd[h])
        st_ref[h] = st_new[h]

    @pl.when(c == pl.num_programs(2) - 1)
    def _():
        for h in hd:
            sout_ref[0, h] = st_new[h]


def _retention(q, k, vg, s0, lc, *, name):
    bsz, heads = s0.shape[:2]
    t = q.shape[0] // bsz
    nc = t // lc
    lg = jnp.log1p(-(2.0 ** (-5.0 - jnp.arange(heads, dtype=F32))))
    idx = jnp.arange(lc, dtype=F32)
    diff = idx[:, None] - idx[None, :]
    dec = jnp.where(diff >= 0, jnp.exp(lg[:, None, None] * jnp.maximum(diff, 0.0)), 0.0)
    cd = jnp.broadcast_to(jnp.exp(lg[:, None] * (idx[None, :] + 1.0))[:, :, None], (heads, lc, LANES))
    kd = jnp.broadcast_to(jnp.exp(lg[:, None] * (lc - 1.0 - idx[None, :]))[:, :, None], (heads, lc, LANES))
    gl = jnp.broadcast_to(jnp.exp(lg * lc)[:, None, None], (heads, 1, LANES))
    hps = RET_HEADS_PER_STEP
    assert heads % hps == 0
    rows = lambda b, h, c: (b * nc + c, h)
    gate = lambda b, h, c: (b * nc + c, heads // hps + h)
    tab = lambda b, h, c: (h, 0, 0)
    state = lambda b, h, c: (b, h, 0, 0)
    return pl.pallas_call(
        functools.partial(_retention_kernel, lc=lc, hps=hps),
        out_shape=(jax.ShapeDtypeStruct((vg.shape[0], heads * R_DV), BF16),
                   jax.ShapeDtypeStruct(s0.shape, F32)),
        grid=(bsz, heads // hps, nc),
        in_specs=[pl.BlockSpec((lc, hps * R_DK), rows),
                  pl.BlockSpec((lc, hps * R_DK), rows),
                  pl.BlockSpec((lc, hps * R_DV), rows),
                  pl.BlockSpec((lc, hps * R_DV), gate),
                  pl.BlockSpec((1, hps, R_DK, R_DV), state),
                  pl.BlockSpec((hps, lc, lc), tab),
                  pl.BlockSpec((hps, lc, LANES), tab),
                  pl.BlockSpec((hps, lc, LANES), tab),
                  pl.BlockSpec((hps, 1, LANES), tab)],
        out_specs=(pl.BlockSpec((lc, hps * R_DV), rows),
                   pl.BlockSpec((1, hps, R_DK, R_DV), state)),
        scratch_shapes=[pltpu.VMEM((hps, R_DK, R_DV), F32)],
        compiler_params=_params(("parallel", "parallel", "arbitrary")),
        cost_estimate=_cost(2 * q.shape[0] * heads * (lc * (R_DK + R_DV) + 2 * R_DK * R_DV),
                            _nbytes(q, k, vg, s0, s0) + q.shape[0] * heads * R_DV * 2,
                            q.shape[0] * heads * R_DV),
        name=name,
    )(q, k, vg, vg, s0, dec, cd, kd, gl)


def _cmlp_kernel(u_ref, g_ref, v_ref, vg_ref, w_ref, b_ref, *out_refs, emit_v):
    if emit_v:
        a_ref, vn_ref = out_refs
    else:
        (a_ref,) = out_refs
    va = _gelu(v_ref[...])
    ms = jnp.mean(va * va, axis=-1, keepdims=True)
    vn = va * lax.rsqrt(ms + EPS) * vg_ref[...]
    if emit_v:
        vn_ref[...] = vn
    vb = vn.astype(BF16)
    gd = vn.shape[1] // M_GROUPS
    for grp in range(M_GROUPS):
        sl = slice(grp * gd, (grp + 1) * gd)
        bias = jnp.concatenate([b_ref[grp]] * (gd // LANES), axis=1)
        mix = jnp.dot(w_ref[grp], vb[:, sl], preferred_element_type=F32) + bias
        g = g_ref[:, sl].astype(F32)
        a_ref[:, sl] = (_gelu(u_ref[:, sl].astype(F32)) * mix * (g * _sigmoid(g))).astype(a_ref.dtype)


def _cmlp(ug, v, v_gain, wmix, bmix, *, emit_v, name):
    m, w = v.shape
    t = wmix.shape[1]
    out_shape = [jax.ShapeDtypeStruct((m, w), BF16)]
    out_specs = [pl.BlockSpec((t, w), lambda i: (i, 0))]
    if emit_v:
        out_shape.append(jax.ShapeDtypeStruct((m, w), F32))
        out_specs.append(pl.BlockSpec((t, w), lambda i: (i, 0)))
    res = pl.pallas_call(
        functools.partial(_cmlp_kernel, emit_v=emit_v),
        out_shape=tuple(out_shape),
        grid=(m // t,),
        in_specs=[pl.BlockSpec((t, w), lambda i: (i, 0)),
                  pl.BlockSpec((t, w), lambda i: (i, 1)),
                  pl.BlockSpec((t, w), lambda i: (i, 0)),
                  pl.BlockSpec((1, w), lambda i: (0, 0)),
                  pl.BlockSpec((M_GROUPS, t, t), lambda i: (0, 0, 0)),
                  pl.BlockSpec((M_GROUPS, t, LANES), lambda i: (0, 0, 0))],
        out_specs=tuple(out_specs),
        compiler_params=_params(("parallel",)),
        cost_estimate=_cost(2 * m * t * w, _nbytes(ug, v) + sum(m * w * jnp.dtype(o.dtype).itemsize for o in out_shape),
                            3 * m * w),
        name=name,
    )(ug, ug, v, v_gain.reshape(1, w), wmix, bmix)
    return res if emit_v else res[0]


def _rope_tables(pos, d, signed):
    inv = ROPE_THETA ** (-jnp.arange(0, d, 2, dtype=F32) / d)
    ang = pos.astype(F32)[:, None] * inv[None, :]
    cos, sin = jnp.cos(ang), jnp.sin(ang)
    reps = LANES // (d // 2)
    if signed:
        return (jnp.tile(cos, (1, reps)),
                jnp.tile(jnp.concatenate([-sin, sin], axis=1), (1, reps // 2)))
    return jnp.tile(cos, (1, reps)), jnp.tile(sin, (1, reps))


def _diff_attn_layer(h, pos, nw, w_in, w_out, q_gain, k_gain, lamp, sub_gain, lam_init,
                     layers, layer, kbuf, vbuf, cache, tag):
    w = w_out.shape[0]
    cos, sin = _rope_tables(pos, A_HD, signed=True)
    wq, wk, wv, wg = (w_in[:, i * w:(i + 1) * w].astype(BF16) for i in range(4))
    (q,), (k, kb), (v, vb), (g,) = _proj(h, nw, [
        _seg(wq, (BF16,), "qk", (A_HD ** -0.5) * LOG2E, q_gain),
        _seg(wk, (F32, BF16), "qk", gain=k_gain, stack=(layers, layer, kbuf)),
        _seg(wv, (F32, BF16), stack=(layers, layer, vbuf)),
        _seg(wg, (BF16,))], cos=cos, sin=sin, name=f"{tag}_proj")
    if cache is None:
        o = _attn_prompt(q, kb, vb, g, lamp, sub_gain, lam_init, name=f"{tag}_attn")
    else:
        kc, vc = cache
        o = _attn_sample(q, kb, vb, g, kc, vc, layer, lamp, sub_gain, lam_init, name=f"{tag}_attn")
    return _outproj(o, w_out.astype(BF16), h, name=f"{tag}_out"), k, v


def _retention_layer(h, pos, nw, w_in, w_out, s0, lc, tag):
    heads = s0.shape[1]
    qk_w, v_w = heads * R_DK, heads * R_DV
    cos, sin = _rope_tables(pos, R_DK, signed=False)
    wq = w_in[:, :qk_w].astype(BF16)
    wk = w_in[:, qk_w:2 * qk_w].astype(BF16)
    assert w_in.shape[1] == 2 * qk_w + 2 * v_w
    wvg = w_in[:, 2 * qk_w:].astype(BF16)
    (q,), (k,), (vg,) = _proj(h, nw, [
        _seg(wq, (BF16,), "rope256"),
        _seg(wk, (BF16,), "rope256", R_DK ** -0.5),
        _seg(wvg, (BF16,))], cos=cos, sin=sin, name=f"{tag}_proj")
    o, s_new = _retention(q, k, vg, s0, lc, name=f"{tag}_ret")
    return _outproj(o, w_out.astype(BF16), h, name=f"{tag}_out"), s_new


def _cmlp_layer(h, nw, w_in, w_out, v_gain, wmix, bmix, emit_v, tag):
    w = w_out.shape[0]
    wu, wv, wg = (w_in[:, i * w:(i + 1) * w].astype(BF16) for i in range(3))
    (ug,), (v,) = _proj(h, nw, [_seg(jnp.concatenate([wu, wg], axis=1), (BF16,)), _seg(wv, (F32,))],
                        name=f"{tag}_proj")
    res = _cmlp(ug, v, v_gain, wmix, bmix, emit_v=emit_v, name=f"{tag}_mix")
    a, vn = res if emit_v else (res, None)
    return _outproj(a, w_out.astype(BF16), h, name=f"{tag}_out"), vn


def _mix_tables(w_s, b_s, chunk_len):
    groups = w_s.shape[0]
    wl = jnp.tril(w_s[:, :chunk_len, :chunk_len])
    reps = M_CHUNK // chunk_len
    eye = jnp.eye(reps, dtype=F32)
    wt = jnp.einsum("ab,gij->gaibj", eye, wl).reshape(groups, M_CHUNK, M_CHUNK)
    bt = jnp.tile(b_s[:, :chunk_len], (1, reps))
    return wt.astype(BF16), jnp.broadcast_to(bt[:, :, None], (groups, M_CHUNK, LANES))


def kernel(x_prompt, x_sample, cache_k_attn, cache_v_attn, state_ret, norm_w, a_w_in, a_w_out, a_q_gain, a_k_gain, a_lam_q1, a_lam_k1, a_lam_q2, a_lam_k2, a_sub_gain, r_w_in, r_w_out, c_w_in, c_w_out, c_v_gain, c_w_s, c_b_s):
    batch, s_len, d = x_prompt.shape
    dec_b, dec_len, _ = x_sample.shape
    past = cache_k_attn.shape[2]
    depth = norm_w.shape[0]
    assert batch == 1 and M_CHUNK % dec_len == 0 and s_len % M_CHUNK == 0

    hp = x_prompt.reshape(s_len, d)
    hs = x_sample.reshape(dec_b * dec_len, d)
    pos_p = jnp.arange(s_len, dtype=jnp.int32)
    pos_s = jnp.tile(past + jnp.arange(dec_len, dtype=jnp.int32), dec_b)

    n_a = a_w_in.shape[0]
    aw = a_w_out.shape[1]
    cache = (cache_k_attn.reshape(n_a, dec_b, past, aw), cache_v_attn.reshape(n_a, dec_b, past, aw))
    kp = vp = kn = vn = None
    sp_l, ss_l, vm_l = [], [], []
    for i in range(depth):
        kind, j = i % N_MIXERS, i // N_MIXERS
        hp, hs = lax.optimization_barrier((hp, hs))
        if kind == 0:
            lam_init = 0.8 - 0.6 * math.exp(-0.3 * i)
            lamp = jnp.stack([a_lam_q1[j], a_lam_k1[j], a_lam_q2[j], a_lam_k2[j]])
            args = (norm_w[i], a_w_in[j], a_w_out[j], a_q_gain[j], a_k_gain[j], lamp, a_sub_gain[j], lam_init)
            hp, kp, vp = _diff_attn_layer(hp, pos_p, *args, n_a, j, kp, vp, None, f"l{i}p")
            hp, hs = lax.optimization_barrier((hp, hs))
            hs, kn, vn = _diff_attn_layer(hs, pos_s, *args, n_a, j, kn, vn, cache, f"l{i}s")
        elif kind == 1:
            heads = state_ret.shape[2]
            s0 = jnp.zeros((batch, heads, R_DK, R_DV), F32)
            hp, st_p = _retention_layer(hp, pos_p, norm_w[i], r_w_in[j], r_w_out[j], s0,
                                        _tile(s_len, 256), f"l{i}p")
            hs, st_s = _retention_layer(hs, pos_s, norm_w[i], r_w_in[j], r_w_out[j],
                                        state_ret[j].astype(F32), dec_len, f"l{i}s")
            sp_l.append(st_p)
            ss_l.append(st_s)
        else:
            wp, bp = _mix_tables(c_w_s[j], c_b_s[j], M_CHUNK)
            ws, bs = _mix_tables(c_w_s[j], c_b_s[j], dec_len)
            hp, _ = _cmlp_layer(hp, norm_w[i], c_w_in[j], c_w_out[j], c_v_gain[j], wp, bp, False, f"l{i}p")
            hs, v_s = _cmlp_layer(hs, norm_w[i], c_w_in[j], c_w_out[j], c_v_gain[j], ws, bs, True, f"l{i}s")
            vm_l.append(v_s.reshape(dec_b, dec_len, -1))

    return (hp.reshape(batch, s_len, d), hs.reshape(dec_b, dec_len, d),
            kp.reshape(n_a, batch, s_len, aw // A_HD, A_HD), vp.reshape(n_a, batch, s_len, aw // A_VD, A_VD),
            kn.reshape(n_a, dec_b, dec_len, aw // A_HD, A_HD), vn.reshape(n_a, dec_b, dec_len, aw // A_VD, A_VD),
            jnp.stack(sp_l), jnp.stack(ss_l), jnp.stack(vm_l))
```

```python
import functools
import math

import jax
import jax.numpy as jnp
from jax import lax
from jax.experimental import pallas as pl
from jax.experimental.pallas import tpu as pltpu

F32 = jnp.float32
BF16 = jnp.bfloat16

EPS = 1e-6
CHUNK = 64
ROPE_THETA = 10000.0
N_MIXERS = 3
A_HD = 64
A_VD = 2 * A_HD
R_DK = 256
R_DV = 2 * R_DK
M_GROUPS = 8
M_CHUNK = 128
LOG2E = 1.4426950408889634

LANES = 128
MXU_COLS = 256
V7X_VMEM_BYTES = 64 * 1024 * 1024
VMEM_LIMIT = V7X_VMEM_BYTES * 7 // 8
NEG_BIG = -1e30


def _tile(n, pref):
    if n <= pref:
        return n
    t = pref
    while t >= 8:
        if n % t == 0:
            return t
        t -= 8
    return n


def _params(sem):
    return pltpu.CompilerParams(dimension_semantics=sem, vmem_limit_bytes=VMEM_LIMIT)


def _nbytes(*arrays):
    return sum(a.size * jnp.dtype(a.dtype).itemsize for a in arrays)


def _cost(flops, nbytes, transcendentals=0):
    return pl.CostEstimate(flops=int(flops), transcendentals=int(transcendentals), bytes_accessed=int(nbytes))


def _gelu(x):
    return 0.5 * x * (1.0 + jnp.tanh(0.7978845608028654 * (x + 0.044715 * (x * x * x))))


def _sigmoid(x):
    return 1.0 / (1.0 + jnp.exp(-x))


def _proj_kernel(*refs, plan):
    xn_ref = refs[-1]
    it = iter(refs)
    x_ref, nw_ref = next(it), next(it)
    epis = [p[0] for p in plan]
    cos_ref, sin_ref = (next(it), next(it)) if any(e != "none" for e in epis) else (None, None)
    gsum_ref = next(it) if "qk" in epis else None
    w_refs, gain_refs = [], []
    for e in epis:
        w_refs.append(next(it))
        gain_refs.append(next(it) if e == "qk" else None)
    outs = list(refs[len(refs) - 1 - sum(p[2] for p in plan):-1])
    o_refs = []
    for p in plan:
        o_refs.append(outs[:p[2]])
        outs = outs[p[2]:]

    def store(seg, cols, val):
        scale = plan[seg][1]
        if scale != 1.0:
            val = val * scale
        for o_ref in o_refs[seg]:
            o_ref[:, cols] = val.astype(o_ref.dtype)

    @pl.when(pl.program_id(1) == 0)
    def _():
        x = x_ref[...]
        ms = jnp.mean(x * x, axis=-1, keepdims=True)
        xn_ref[...] = (x * lax.rsqrt(ms + EPS) * nw_ref[...]).astype(BF16)

    qk = [s for s, e in enumerate(epis) if e == "qk"]
    rest = [s for s, e in enumerate(epis) if e == "rope256"] + [s for s, e in enumerate(epis) if e == "none"]
    slabs = {s: [slice(c * MXU_COLS, (c + 1) * MXU_COLS) for c in range(w_refs[s].shape[1] // MXU_COLS)]
             for s in qk}
    zq = {(s, c): jnp.dot(xn_ref[...], w_refs[s][:, sl], preferred_element_type=F32)
          for s in qk for c, sl in enumerate(slabs[s])}
    sq = {key: jnp.dot((z * z).astype(BF16), gsum_ref[...], preferred_element_type=F32) for key, z in zq.items()}
    zr = {s: jnp.dot(xn_ref[...], w_refs[s][...], preferred_element_type=F32) for s in rest}

    if qk:
        cos = jnp.concatenate([cos_ref[...]] * 2, axis=1)
        sin = jnp.concatenate([sin_ref[...]] * 2, axis=1)
        lane = lax.broadcasted_iota(jnp.int32, (xn_ref.shape[0], MXU_COLS), 1)
        first_half = (lane % A_HD) < (A_HD // 2)
        for (s, c), z in zq.items():
            gain = jnp.concatenate([gain_refs[s][...]] * 2, axis=1)
            zn = z * lax.rsqrt(sq[(s, c)] * (1.0 / A_HD) + EPS) * gain
            partner = jnp.where(first_half,
                                pltpu.roll(zn, MXU_COLS - A_HD // 2, 1),
                                pltpu.roll(zn, A_HD // 2, 1))
            store(s, slabs[s][c], zn * cos + partner * sin)
    for s in rest:
        z = zr[s]
        if epis[s] == "rope256":
            cos = cos_ref[...]
            sin = sin_ref[...]
            for c in range(z.shape[1] // R_DK):
                lo, hi = slice(c * R_DK, c * R_DK + LANES), slice(c * R_DK + LANES, (c + 1) * R_DK)
                x1, x2 = z[:, lo], z[:, hi]
                store(s, lo, x1 * cos - x2 * sin)
                store(s, hi, x2 * cos + x1 * sin)
        else:
            store(s, slice(None), z)


def _seg(w, out_dtypes, epi="none", out_scale=1.0, gain=None, stack=None):
    return dict(w=w, out_dtypes=out_dtypes, epi=epi, out_scale=out_scale, gain=gain, stack=stack)


PROJ_VMEM_BUDGET = V7X_VMEM_BYTES * 3 // 4


def _proj(x, nw, segs, *, cos=None, sin=None, name):
    m, d = x.shape
    tm = _tile(m, 1024)
    epis = [s["epi"] for s in segs]
    assert not ("qk" in epis and "rope256" in epis)
    unit = {"qk": MXU_COLS, "rope256": R_DK, "none": LANES}

    def vmem_bytes(nj):
        total = 2 * tm * d * 4 + tm * d * 2
        for s in segs:
            tn = s["w"].shape[1] // nj
            total += 2 * d * tn * 2 + tm * tn * 4
            total += sum(2 * tm * tn * jnp.dtype(dt).itemsize for dt in s["out_dtypes"])
        return total

    nj = next(c for c in (1, 2, 4, 8, 16, 32)
              if all(s["w"].shape[1] % (c * unit[s["epi"]]) == 0 for s in segs) and vmem_bytes(c) <= PROJ_VMEM_BUDGET)
    in_specs = [pl.BlockSpec((tm, d), lambda i, j: (i, 0)),
                pl.BlockSpec((1, d), lambda i, j: (0, 0))]
    args = [x, nw.reshape(1, d)]
    if any(e != "none" for e in epis):
        in_specs += [pl.BlockSpec((tm, LANES), lambda i, j: (i, 0))] * 2
        args += [cos, sin]
    if "qk" in epis:
        gidx = jnp.arange(MXU_COLS) // A_HD
        in_specs.append(pl.BlockSpec((MXU_COLS, MXU_COLS), lambda i, j: (0, 0)))
        args.append((gidx[:, None] == gidx[None, :]).astype(BF16))
    out_shape, out_specs, bufs = [], [], []
    for s in segs:
        n = s["w"].shape[1]
        tn = n // nj
        in_specs.append(pl.BlockSpec((d, tn), lambda i, j: (0, j)))
        args.append(s["w"])
        if s["epi"] == "qk":
            in_specs.append(pl.BlockSpec((1, LANES), lambda i, j: (0, 0)))
            args.append(jnp.tile(s["gain"].reshape(1, A_HD), (1, LANES // A_HD)))
        for o, dt in enumerate(s["out_dtypes"]):
            if o == 0 and s["stack"] is not None:
                layers, layer, buf = s["stack"]
                out_specs.append(pl.BlockSpec((None, tm, tn), lambda i, j, layer=layer: (layer, i, j)))
                out_shape.append(jax.ShapeDtypeStruct((layers, m, n), dt))
                if buf is not None:
                    bufs.append((len(out_shape) - 1, buf))
            else:
                out_specs.append(pl.BlockSpec((tm, tn), lambda i, j: (i, j)))
                out_shape.append(jax.ShapeDtypeStruct((m, n), dt))
    aliases = {}
    for out_idx, buf in bufs:
        aliases[len(args)] = out_idx
        in_specs.append(pl.BlockSpec(memory_space=pl.ANY))
        args.append(buf)
    res = list(pl.pallas_call(
        functools.partial(_proj_kernel, plan=tuple((s["epi"], s["out_scale"], len(s["out_dtypes"])) for s in segs)),
        out_shape=tuple(out_shape),
        grid=(m // tm, nj),
        in_specs=in_specs,
        out_specs=tuple(out_specs),
        scratch_shapes=[pltpu.VMEM((tm, d), BF16)],
        input_output_aliases=aliases,
        compiler_params=_params(("parallel", "arbitrary")),
        cost_estimate=_cost(2 * m * d * sum(s["w"].shape[1] for s in segs),
                            _nbytes(x) + (m // tm) * _nbytes(*(s["w"] for s in segs))
                            + sum(m * s["w"].shape[1] * jnp.dtype(dt).itemsize for s in segs for dt in s["out_dtypes"])),
        name=name,
    )(*args))
    out = []
    for s in segs:
        out.append(res[:len(s["out_dtypes"])])
        res = res[len(s["out_dtypes"]):]
    return out


def _outproj_kernel(a_ref, w_ref, h_ref, o_ref):
    o_ref[...] = h_ref[...] + jnp.dot(a_ref[...], w_ref[...], preferred_element_type=F32)


def _outproj(a, w, h, *, name):
    m, k = a.shape
    n = w.shape[1]
    tm = _tile(m, 1024)
    tn = _tile(n, 1024)
    return pl.pallas_call(
        _outproj_kernel,
        out_shape=jax.ShapeDtypeStruct((m, n), F32),
        grid=(m // tm, n // tn),
        in_specs=[pl.BlockSpec((tm, k), lambda i, j: (i, 0)),
                  pl.BlockSpec((k, tn), lambda i, j: (0, j)),
                  pl.BlockSpec((tm, tn), lambda i, j: (i, j))],
        out_specs=pl.BlockSpec((tm, tn), lambda i, j: (i, j)),
        compiler_params=_params(("parallel", "parallel")),
        cost_estimate=_cost(2 * m * k * n, _nbytes(a) + (m // tm) * _nbytes(w) + 2 * _nbytes(h)),
        name=name,
    )(a, w, h)


def _lambda(lamp_ref, lam_init):
    lp = lamp_ref[...]
    s1 = jnp.sum(lp[0:1, :] * lp[1:2, :], axis=-1, keepdims=True)
    s2 = jnp.sum(lp[2:3, :] * lp[3:4, :], axis=-1, keepdims=True)
    return jnp.exp(s1) - jnp.exp(s2) + lam_init


V_AUG = A_VD + 16


LOOKAHEAD = 5


def _attn_prompt_kernel(q_ref, qn_ref, k_ref, v_ref, g_ref, lamp_ref, sg_ref, o_ref,
                        qpad_ref, vt_ref, m_ref, acc_ref, s_ref, mx_ref, *, tq, tk, lam_init):
    i = pl.program_id(1)
    slot = i % 2

    sw = s_ref.shape[2]
    n_strips = 2 * tq // sw
    la = min(LOOKAHEAD, n_strips - 1)
    order = sorted(range(n_strips), key=lambda c: -((c * sw) % tq))

    def load_queries(src_ref, dst):
        qt = src_ref[...].astype(F32).T
        row = lax.broadcasted_iota(jnp.int32, qt.shape, 0)
        qpad_ref[dst, :, :tq] = jnp.where(row < A_HD, qt, 0.0).astype(BF16)
        qpad_ref[dst, :, tq:] = jnp.where(row >= A_HD, qt, 0.0).astype(BF16)

    def scores(j, c, rows=tk, src=slot):
        kj = k_ref[pl.ds(pl.multiple_of(j * tk, tk), rows), :]
        s = jnp.dot(kj, qpad_ref[src, :, c * sw:(c + 1) * sw], preferred_element_type=F32)
        s_ref[c, :rows] = s
        mx_ref[c] = jnp.max(s, axis=0, keepdims=True)

    @pl.when(i == 0)
    def _():
        aug = lax.broadcasted_iota(jnp.int32, (V_AUG - A_VD, tk), 0)
        ones_row = jnp.where(aug == 0, 1.0, 0.0).astype(BF16)

        def fill(t, carry):
            vj = v_ref[pl.ds(pl.multiple_of(t * tk, tk), tk), :].astype(F32)
            vt_ref[t, :A_VD, :] = vj.T.astype(BF16)
            vt_ref[t, A_VD:, :] = ones_row
            return carry

        lax.fori_loop(0, vt_ref.shape[0], fill, 0)
        load_queries(q_ref, 0)
        for c in order[:la]:
            scores(0, c, src=0)

    m_ref[...] = jnp.full(m_ref.shape, NEG_BIG, F32)
    acc_ref[...] = jnp.zeros(acc_ref.shape, F32)

    def update(j, c, mask, rows=tk):
        cs = slice(c * sw, (c + 1) * sw)
        s = s_ref[c, :rows]
        if mask is None:
            mx = mx_ref[c]
        else:
            s = jnp.where(mask[:rows], s, NEG_BIG)
            mx = jnp.max(s, axis=0, keepdims=True)
        m_old = m_ref[:, cs]
        m_new = jnp.maximum(m_old, mx)
        alpha = jnp.exp2(m_old - m_new)
        p = jnp.exp2(s - m_new).astype(BF16)
        pv = jnp.dot(vt_ref[j][:, :rows], p, preferred_element_type=F32)
        acc_ref[:, cs] = alpha * acc_ref[:, cs] + pv
        m_ref[:, cs] = m_new

    n_sub = tq // tk
    n_full = i * n_sub

    def full_tiles(j0, count):
        for j in range(j0, j0 + count) if isinstance(j0, int) else [j0 + t for t in range(count)]:
            for n, c in enumerate(order):
                update(j, c, None)
                if n + la < n_strips:
                    scores(j, order[n + la])
                else:
                    scores(j + 1, order[n + la - n_strips])

    odd = i % 2

    @pl.when(odd == 1)
    def _():
        full_tiles(0, n_sub)

    def body(jj, carry):
        full_tiles(odd * n_sub + jj * (2 * n_sub), 2 * n_sub)
        return carry

    lax.fori_loop(0, i // 2, body, 0)

    r = lax.broadcasted_iota(jnp.int32, (tk, sw), 0)
    cc = lax.broadcasted_iota(jnp.int32, (tk, sw), 1)
    units = []
    for d in range(n_sub):
        for c in order:
            q_lo, k_lo = (c * sw) % tq, d * tk
            if q_lo + sw <= k_lo:
                continue
            full = q_lo >= k_lo + tk
            rows = min(tk, q_lo + sw - k_lo)
            units.append((d, c, None if full else (k_lo + r) // CHUNK <= (q_lo + cc) // CHUNK, rows))
    assert [u[:2] + u[3:] for u in units[:la]] == [(0, c, tk) for c in order[:la]]
    load_queries(qn_ref, 1 - slot)
    for n, (d, c, mask, rows) in enumerate(units):
        update(n_full + d, c, mask, rows)
        if n + la < len(units):
            nd, nc, _, nrows = units[n + la]
            assert all(u[1] != nc for u in units[n + 1:n + la])
            scores(n_full + nd, nc, nrows)
    for c in order[:la]:
        scores(0, c, src=1 - slot)

    lam = _lambda(lamp_ref, lam_init)
    acc = acc_ref[...]
    inv_l = 1.0 / acc[A_VD:A_VD + 1, :]
    ot = acc[:A_VD, :tq] * inv_l[:, :tq] - lam * (acc[:A_VD, tq:] * inv_l[:, tq:])
    ms = jnp.mean(ot * ot, axis=0, keepdims=True)
    ot = ot * lax.rsqrt(ms + EPS) * (sg_ref[...] * (1.0 - lam_init))
    g = g_ref[...].astype(F32)
    o_ref[...] = (ot.T * (g * _sigmoid(g))).astype(o_ref.dtype)


def _attn_prompt(q, k, v, g, lamp, sub_gain, lam_init, *, name):
    s, w = q.shape
    heads = w // A_VD
    tq = _tile(s, 1024)
    tk = _tile(tq, 512)
    sw = min(MXU_COLS, tq)
    nq = s // tq
    return pl.pallas_call(
        functools.partial(_attn_prompt_kernel, tq=tq, tk=tk, lam_init=lam_init),
        out_shape=jax.ShapeDtypeStruct((s, w), BF16),
        grid=(heads, nq),
        in_specs=[pl.BlockSpec((tq, A_VD), lambda h, i: (i, h)),
                  pl.BlockSpec((tq, A_VD), lambda h, i: (jnp.minimum(i + 1, nq - 1), h)),
                  pl.BlockSpec((s, A_VD), lambda h, i: (0, h)),
                  pl.BlockSpec((s, A_VD), lambda h, i: (0, h)),
                  pl.BlockSpec((tq, A_VD), lambda h, i: (i, h)),
                  pl.BlockSpec((4, A_HD), lambda h, i: (0, 0)),
                  pl.BlockSpec((A_VD, 1), lambda h, i: (0, 0))],
        out_specs=pl.BlockSpec((tq, A_VD), lambda h, i: (i, h)),
        scratch_shapes=[pltpu.VMEM((2, A_VD, 2 * tq), BF16),
                        pltpu.VMEM((s // tk, V_AUG, tk), BF16),
                        pltpu.VMEM((1, 2 * tq), F32),
                        pltpu.VMEM((V_AUG, 2 * tq), F32),
                        pltpu.VMEM((2 * tq // sw, tk, sw), F32),
                        pltpu.VMEM((2 * tq // sw, 1, sw), F32)],
        compiler_params=_params(("parallel", "arbitrary")),
        cost_estimate=_cost(heads * (s * s // 2) * 2 * 2 * (A_VD + V_AUG), _nbytes(q, k, v, g, q),
                            heads * (s * s // 2) * 2),
        name=name,
    )(q, q, k, v, g, lamp, sub_gain.reshape(A_VD, 1))


SAMPLE_HEADS_PER_STEP = 4


def _attn_sample_kernel(q_ref, kn_ref, vn_ref, g_ref, kc_ref, vc_ref, lamp_ref, sg_ref, o_ref, *, lam_init, hps):
    lam = _lambda(lamp_ref, lam_init)
    nt = (((1,), (1,)), ((), ()))
    lane = lax.broadcasted_iota(jnp.int32, (q_ref.shape[0], A_VD), 1)
    heads = [slice(h * A_VD, (h + 1) * A_VD) for h in range(hps)]
    scores = []
    for hs in heads:
        q = q_ref[:, hs]
        kc = kc_ref[0, :, hs].astype(BF16)
        kn = kn_ref[:, hs].astype(BF16)
        for half in range(2):
            qh = jnp.where((lane >= A_HD) == bool(half), q, jnp.zeros_like(q))
            scores.append((lax.dot_general(qh, kc, nt, preferred_element_type=F32),
                           lax.dot_general(qh, kn, nt, preferred_element_type=F32)))
    probs = []
    for sc, sn in scores:
        m = jnp.maximum(jnp.max(sc, axis=-1, keepdims=True), jnp.max(sn, axis=-1, keepdims=True))
        pc = jnp.exp2(sc - m)
        pn = jnp.exp2(sn - m)
        inv = 1.0 / (jnp.sum(pc, axis=-1, keepdims=True) + jnp.sum(pn, axis=-1, keepdims=True))
        probs.append((pc * inv, pn * inv))
    outs = []
    for h, hs in enumerate(heads):
        ac = (probs[2 * h][0] - lam * probs[2 * h + 1][0]).astype(BF16)
        an = (probs[2 * h][1] - lam * probs[2 * h + 1][1]).astype(BF16)
        outs.append(jnp.dot(ac, vc_ref[0, :, hs].astype(BF16), preferred_element_type=F32)
                    + jnp.dot(an, vn_ref[:, hs].astype(BF16), preferred_element_type=F32))
    for hs, o in zip(heads, outs):
        ms = jnp.mean(o * o, axis=-1, keepdims=True)
        g = g_ref[:, hs].astype(F32)
        o = o * lax.rsqrt(ms + EPS) * (sg_ref[...] * (1.0 - lam_init))
        o_ref[:, hs] = (o * (g * _sigmoid(g))).astype(o_ref.dtype)


def _attn_sample(q, k, v, g, kc, vc, layer, lamp, sub_gain, lam_init, *, name):
    _, bsz, past, w = kc.shape
    heads = w // A_VD
    ln = q.shape[0] // bsz
    assert past % CHUNK == 0 and ln <= CHUNK
    hps = SAMPLE_HEADS_PER_STEP
    assert heads % hps == 0
    row = lambda b, h: (b, h)
    cache = lambda b, h: (layer, b, 0, h)
    return pl.pallas_call(
        functools.partial(_attn_sample_kernel, lam_init=lam_init, hps=hps),
        out_shape=jax.ShapeDtypeStruct(q.shape, BF16),
        grid=(bsz, heads // hps),
        in_specs=[pl.BlockSpec((ln, hps * A_VD), row),
                  pl.BlockSpec((ln, hps * A_VD), row),
                  pl.BlockSpec((ln, hps * A_VD), row),
                  pl.BlockSpec((ln, hps * A_VD), row),
                  pl.BlockSpec((None, 1, past, hps * A_VD), cache),
                  pl.BlockSpec((None, 1, past, hps * A_VD), cache),
                  pl.BlockSpec((4, A_HD), lambda b, h: (0, 0)),
                  pl.BlockSpec((1, A_VD), lambda b, h: (0, 0))],
        out_specs=pl.BlockSpec((ln, hps * A_VD), row),
        compiler_params=_params(("parallel", "parallel")),
        cost_estimate=_cost(bsz * heads * ln * (past + ln) * 2 * 2 * 2 * A_VD,
                            _nbytes(q, k, v, g, q) + 2 * bsz * past * w * jnp.dtype(kc.dtype).itemsize,
                            bsz * heads * ln * (past + ln) * 2),
        name=name,
    )(q, k, v, g, kc, vc, lamp, sub_gain.reshape(1, A_VD))


RET_HEADS_PER_STEP = 8


def _retention_kernel(q_ref, k_ref, v_ref, g_ref, s0_ref, dec_ref, cd_ref, kd_ref, gl_ref,
                      o_ref, sout_ref, st_ref, *, lc, hps):
    c = pl.program_id(2)

    @pl.when(c == 0)
    def _():
        st_ref[...] = s0_ref[0]

    hd = range(hps)
    qs = [q_ref[:, h * R_DK:(h + 1) * R_DK] for h in hd]
    ks = [k_ref[:, h * R_DK:(h + 1) * R_DK] for h in hd]
    vs = [v_ref[:, h * R_DV:(h + 1) * R_DV] for h in hd]
    sts = [st_ref[h] for h in hd]
    nt = (((1,), (1,)), ((), ()))
    inner = [lax.dot_general(qs[h], ks[h], nt, preferred_element_type=F32) for h in hd]
    cross = [jnp.dot(qs[h], sts[h].astype(BF16), preferred_element_type=F32) for h in hd]
    kdec = []
    for h in hd:
        kd = ks[h].astype(F32) * jnp.concatenate([kd_ref[h]] * (R_DK // LANES), axis=1)
        if lc < LANES:
            kd = jnp.concatenate([kd, jnp.zeros((LANES - lc, R_DK), F32)], axis=0)
        kdec.append(kd.T.astype(BF16))
    inner = [(inner[h] * dec_ref[h]).astype(BF16) for h in hd]
    o = [jnp.dot(inner[h], vs[h], preferred_element_type=F32) for h in hd]
    upd = []
    for h in hd:
        vv = vs[h]
        if lc < LANES:
            vv = jnp.concatenate([vv, jnp.zeros((LANES - lc, R_DV), BF16)], axis=0)
        upd.append(jnp.dot(kdec[h], vv, preferred_element_type=F32))
    for h in hd:
        cd = jnp.concatenate([cd_ref[h]] * (R_DV // LANES), axis=1)
        oh = o[h] + cross[h] * cd
        ms = jnp.mean(oh * oh, axis=-1, keepdims=True)
        g = g_ref[:, h * R_DV:(h + 1) * R_DV].astype(F32)
        o_ref[:, h * R_DV:(h + 1) * R_DV] = (oh * lax.rsqrt(ms + EPS) * (g * _sigmoid(g))).astype(o_ref.dtype)
    st_new = []
    for h in hd:
        gl = jnp.concatenate([gl_ref[h]] * (R_DV // LANES), axis=1)
        st_new.append(sts[h] * gl + upd[h])
        st_ref[h] = st_new[h]

    @pl.when(c == pl.num_programs(2) - 1)
    def _():
        for h in hd:
            sout_ref[0, h] = st_new[h]


def _retention(q, k, vg, s0, lc, *, name):
    bsz, heads = s0.shape[:2]
    t = q.shape[0] // bsz
    nc = t // lc
    lg = jnp.log1p(-(2.0 ** (-5.0 - jnp.arange(heads, dtype=F32))))
    idx = jnp.arange(lc, dtype=F32)
    diff = idx[:, None] - idx[None, :]
    dec = jnp.where(diff >= 0, jnp.exp(lg[:, None, None] * jnp.maximum(diff, 0.0)), 0.0)
    cd = jnp.broadcast_to(jnp.exp(lg[:, None] * (idx[None, :] + 1.0))[:, :, None], (heads, lc, LANES))
    kd = jnp.broadcast_to(jnp.exp(lg[:, None] * (lc - 1.0 - idx[None, :]))[:, :, None], (heads, lc, LANES))
    gl = jnp.broadcast_to(jnp.exp(lg * lc)[:, None, None], (heads, 1, LANES))
    hps = RET_HEADS_PER_STEP
    assert heads % hps == 0
    rows = lambda b, h, c: (b * nc + c, h)
    gate = lambda b, h, c: (b * nc + c, heads // hps + h)
    tab = lambda b, h, c: (h, 0, 0)
    state = lambda b, h, c: (b, h, 0, 0)
    return pl.pallas_call(
        functools.partial(_retention_kernel, lc=lc, hps=hps),
        out_shape=(jax.ShapeDtypeStruct((vg.shape[0], heads * R_DV), BF16),
                   jax.ShapeDtypeStruct(s0.shape, F32)),
        grid=(bsz, heads // hps, nc),
        in_specs=[pl.BlockSpec((lc, hps * R_DK), rows),
                  pl.BlockSpec((lc, hps * R_DK), rows),
                  pl.BlockSpec((lc, hps * R_DV), rows),
                  pl.BlockSpec((lc, hps * R_DV), gate),
                  pl.BlockSpec((1, hps, R_DK, R_DV), state),
                  pl.BlockSpec((hps, lc, lc), tab),
                  pl.BlockSpec((hps, lc, LANES), tab),
                  pl.BlockSpec((hps, lc, LANES), tab),
                  pl.BlockSpec((hps, 1, LANES), tab)],
        out_specs=(pl.BlockSpec((lc, hps * R_DV), rows),
                   pl.BlockSpec((1, hps, R_DK, R_DV), state)),
        scratch_shapes=[pltpu.VMEM((hps, R_DK, R_DV), F32)],
        compiler_params=_params(("parallel", "parallel", "arbitrary")),
        cost_estimate=_cost(2 * q.shape[0] * heads * (lc * (R_DK + R_DV) + 2 * R_DK * R_DV),
                            _nbytes(q, k, vg, s0, s0) + q.shape[0] * heads * R_DV * 2,
                            q.shape[0] * heads * R_DV),
        name=name,
    )(q, k, vg, vg, s0, dec, cd, kd, gl)


def _cmlp_kernel(u_ref, g_ref, v_ref, vg_ref, w_ref, b_ref, *out_refs, emit_v):
    if emit_v:
        a_ref, vn_ref = out_refs
    else:
        (a_ref,) = out_refs
    va = _gelu(v_ref[...])
    ms = jnp.mean(va * va, axis=-1, keepdims=True)
    vn = va * lax.rsqrt(ms + EPS) * vg_ref[...]
    if emit_v:
        vn_ref[...] = vn
    vb = vn.astype(BF16)
    gd = vn.shape[1] // M_GROUPS
    for grp in range(M_GROUPS):
        sl = slice(grp * gd, (grp + 1) * gd)
        bias = jnp.concatenate([b_ref[grp]] * (gd // LANES), axis=1)
        mix = jnp.dot(w_ref[grp], vb[:, sl], preferred_element_type=F32) + bias
        g = g_ref[:, sl].astype(F32)
        a_ref[:, sl] = (_gelu(u_ref[:, sl].astype(F32)) * mix * (g * _sigmoid(g))).astype(a_ref.dtype)


def _cmlp(ug, v, v_gain, wmix, bmix, *, emit_v, name):
    m, w = v.shape
    t = wmix.shape[1]
    out_shape = [jax.ShapeDtypeStruct((m, w), BF16)]
    out_specs = [pl.BlockSpec((t, w), lambda i: (i, 0))]
    if emit_v:
        out_shape.append(jax.ShapeDtypeStruct((m, w), F32))
        out_specs.append(pl.BlockSpec((t, w), lambda i: (i, 0)))
    res = pl.pallas_call(
        functools.partial(_cmlp_kernel, emit_v=emit_v),
        out_shape=tuple(out_shape),
        grid=(m // t,),
        in_specs=[pl.BlockSpec((t, w), lambda i: (i, 0)),
                  pl.BlockSpec((t, w), lambda i: (i, 1)),
                  pl.BlockSpec((t, w), lambda i: (i, 0)),
                  pl.BlockSpec((1, w), lambda i: (0, 0)),
                  pl.BlockSpec((M_GROUPS, t, t), lambda i: (0, 0, 0)),
                  pl.BlockSpec((M_GROUPS, t, LANES), lambda i: (0, 0, 0))],
        out_specs=tuple(out_specs),
        compiler_params=_params(("parallel",)),
        cost_estimate=_cost(2 * m * t * w, _nbytes(ug, v) + sum(m * w * jnp.dtype(o.dtype).itemsize for o in out_shape),
                            3 * m * w),
        name=name,
    )(ug, ug, v, v_gain.reshape(1, w), wmix, bmix)
    return res if emit_v else res[0]


def _rope_tables(pos, d, signed):
    inv = ROPE_THETA ** (-jnp.arange(0, d, 2, dtype=F32) / d)
    ang = pos.astype(F32)[:, None] * inv[None, :]
    cos, sin = jnp.cos(ang), jnp.sin(ang)
    reps = LANES // (d // 2)
    if signed:
        return (jnp.tile(cos, (1, reps)),
                jnp.tile(jnp.concatenate([-sin, sin], axis=1), (1, reps // 2)))
    return jnp.tile(cos, (1, reps)), jnp.tile(sin, (1, reps))


def _diff_attn_layer(h, pos, nw, w_in, w_out, q_gain, k_gain, lamp, sub_gain, lam_init,
                     layers, layer, kbuf, vbuf, cache, tag):
    w = w_out.shape[0]
    cos, sin = _rope_tables(pos, A_HD, signed=True)
    wq, wk, wv, wg = (w_in[:, i * w:(i + 1) * w].astype(BF16) for i in range(4))
    (q,), (k, kb), (v, vb), (g,) = _proj(h, nw, [
        _seg(wq, (BF16,), "qk", (A_HD ** -0.5) * LOG2E, q_gain),
        _seg(wk, (F32, BF16), "qk", gain=k_gain, stack=(layers, layer, kbuf)),
        _seg(wv, (F32, BF16), stack=(layers, layer, vbuf)),
        _seg(wg, (BF16,))], cos=cos, sin=sin, name=f"{tag}_proj")
    if cache is None:
        w_out, q = lax.optimization_barrier((w_out, q))
        wo, q = lax.optimization_barrier((w_out.astype(BF16), q))
        o = _attn_prompt(q, kb, vb, g, lamp, sub_gain, lam_init, name=f"{tag}_attn")
    else:
        kc, vc = cache
        wo = w_out.astype(BF16)
        o = _attn_sample(q, kb, vb, g, kc, vc, layer, lamp, sub_gain, lam_init, name=f"{tag}_attn")
    return _outproj(o, wo, h, name=f"{tag}_out"), k, v


def _retention_layer(h, pos, nw, w_in, w_out, s0, lc, tag):
    heads = s0.shape[1]
    qk_w, v_w = heads * R_DK, heads * R_DV
    cos, sin = _rope_tables(pos, R_DK, signed=False)
    wq = w_in[:, :qk_w].astype(BF16)
    wk = w_in[:, qk_w:2 * qk_w].astype(BF16)
    assert w_in.shape[1] == 2 * qk_w + 2 * v_w
    wvg = w_in[:, 2 * qk_w:].astype(BF16)
    (q,), (k,), (vg,) = _proj(h, nw, [
        _seg(wq, (BF16,), "rope256"),
        _seg(wk, (BF16,), "rope256", R_DK ** -0.5),
        _seg(wvg, (BF16,))], cos=cos, sin=sin, name=f"{tag}_proj")
    o, s_new = _retention(q, k, vg, s0, lc, name=f"{tag}_ret")
    return _outproj(o, w_out.astype(BF16), h, name=f"{tag}_out"), s_new


def _cmlp_layer(h, nw, w_in, w_out, v_gain, wmix, bmix, emit_v, tag):
    w = w_out.shape[0]
    wu, wv, wg = (w_in[:, i * w:(i + 1) * w].astype(BF16) for i in range(3))
    (ug,), (v,) = _proj(h, nw, [_seg(jnp.concatenate([wu, wg], axis=1), (BF16,)), _seg(wv, (F32,))],
                        name=f"{tag}_proj")
    res = _cmlp(ug, v, v_gain, wmix, bmix, emit_v=emit_v, name=f"{tag}_mix")
    a, vn = res if emit_v else (res, None)
    return _outproj(a, w_out.astype(BF16), h, name=f"{tag}_out"), vn


def _mix_tables(w_s, b_s, chunk_len):
    groups = w_s.shape[0]
    wl = jnp.tril(w_s[:, :chunk_len, :chunk_len])
    reps = M_CHUNK // chunk_len
    eye = jnp.eye(reps, dtype=F32)
    wt = jnp.einsum("ab,gij->gaibj", eye, wl).reshape(groups, M_CHUNK, M_CHUNK)
    bt = jnp.tile(b_s[:, :chunk_len], (1, reps))
    return wt.astype(BF16), jnp.broadcast_to(bt[:, :, None], (groups, M_CHUNK, LANES))


def kernel(x_prompt, x_sample, cache_k_attn, cache_v_attn, state_ret, norm_w, a_w_in, a_w_out, a_q_gain, a_k_gain, a_lam_q1, a_lam_k1, a_lam_q2, a_lam_k2, a_sub_gain, r_w_in, r_w_out, c_w_in, c_w_out, c_v_gain, c_w_s, c_b_s):
    batch, s_len, d = x_prompt.shape
    dec_b, dec_len, _ = x_sample.shape
    past = cache_k_attn.shape[2]
    depth = norm_w.shape[0]
    assert batch == 1 and M_CHUNK % dec_len == 0 and s_len % M_CHUNK == 0

    hp = x_prompt.reshape(s_len, d)
    hs = x_sample.reshape(dec_b * dec_len, d)
    pos_p = jnp.arange(s_len, dtype=jnp.int32)
    pos_s = jnp.tile(past + jnp.arange(dec_len, dtype=jnp.int32), dec_b)

    n_a = a_w_in.shape[0]
    aw = a_w_out.shape[1]
    cache = (cache_k_attn.reshape(n_a, dec_b, past, aw), cache_v_attn.reshape(n_a, dec_b, past, aw))
    kp = vp = kn = vn = None
    sp_l, ss_l, vm_l = [], [], []
    for i in range(depth):
        kind, j = i % N_MIXERS, i // N_MIXERS
        hp, hs = lax.optimization_barrier((hp, hs))
        if kind == 0:
            lam_init = 0.8 - 0.6 * math.exp(-0.3 * i)
            lamp = jnp.stack([a_lam_q1[j], a_lam_k1[j], a_lam_q2[j], a_lam_k2[j]])
            args = (norm_w[i], a_w_in[j], a_w_out[j], a_q_gain[j], a_k_gain[j], lamp, a_sub_gain[j], lam_init)
            hp, kp, vp = _diff_attn_layer(hp, pos_p, *args, n_a, j, kp, vp, None, f"l{i}p")
            hp, hs = lax.optimization_barrier((hp, hs))
            hs, kn, vn = _diff_attn_layer(hs, pos_s, *args, n_a, j, kn, vn, cache, f"l{i}s")
        elif kind == 1:
            heads = state_ret.shape[2]
            s0 = jnp.zeros((batch, heads, R_DK, R_DV), F32)
            hp, st_p = _retention_layer(hp, pos_p, norm_w[i], r_w_in[j], r_w_out[j], s0,
                                        _tile(s_len, 256), f"l{i}p")
            hs, st_s = _retention_layer(hs, pos_s, norm_w[i], r_w_in[j], r_w_out[j],
                                        state_ret[j].astype(F32), dec_len, f"l{i}s")
            sp_l.append(st_p)
            ss_l.append(st_s)
        else:
            wp, bp = _mix_tables(c_w_s[j], c_b_s[j], M_CHUNK)
            ws, bs = _mix_tables(c_w_s[j], c_b_s[j], dec_len)
            hp, _ = _cmlp_layer(hp, norm_w[i], c_w_in[j], c_w_out[j], c_v_gain[j], wp, bp, False, f"l{i}p")
            hs, v_s = _cmlp_layer(hs, norm_w[i], c_w_in[j], c_w_out[j], c_v_gain[j], ws, bs, True, f"l{i}s")
            vm_l.append(v_s.reshape(dec_b, dec_len, -1))

    return (hp.reshape(batch, s_len, d), hs.reshape(dec_b, dec_len, d),
            kp.reshape(n_a, batch, s_len, aw // A_HD, A_HD), vp.reshape(n_a, batch, s_len, aw // A_VD, A_VD),
            kn.reshape(n_a, dec_b, dec_len, aw // A_HD, A_HD), vn.reshape(n_a, dec_b, dec_len, aw // A_VD, A_VD),
            jnp.stack(sp_l), jnp.stack(ss_l), jnp.stack(vm_l))
```

```python
import functools
import math

import jax
import jax.numpy as jnp
from jax import lax
from jax.experimental import pallas as pl
from jax.experimental.pallas import tpu as pltpu

F32 = jnp.float32
BF16 = jnp.bfloat16

EPS = 1e-6
CHUNK = 64
ROPE_THETA = 10000.0
N_MIXERS = 3
A_HD = 64
A_VD = 2 * A_HD
R_DK = 256
R_DV = 2 * R_DK
M_GROUPS = 8
M_CHUNK = 128
LOG2E = 1.4426950408889634

LANES = 128
MXU_COLS = 256
V7X_VMEM_BYTES = 64 * 1024 * 1024
VMEM_LIMIT = V7X_VMEM_BYTES * 7 // 8
NEG_BIG = -1e30


def _tile(n, pref):
    if n <= pref:
        return n
    t = pref
    while t >= 8:
        if n % t == 0:
            return t
        t -= 8
    return n


def _params(sem):
    return pltpu.CompilerParams(dimension_semantics=sem, vmem_limit_bytes=VMEM_LIMIT)


def _nbytes(*arrays):
    return sum(a.size * jnp.dtype(a.dtype).itemsize for a in arrays)


def _cost(flops, nbytes, transcendentals=0):
    return pl.CostEstimate(flops=int(flops), transcendentals=int(transcendentals), bytes_accessed=int(nbytes))


def _gelu(x):
    return 0.5 * x * (1.0 + jnp.tanh(0.7978845608028654 * (x + 0.044715 * (x * x * x))))


def _sigmoid(x):
    return 1.0 / (1.0 + jnp.exp(-x))


def _proj_kernel(*refs, plan):
    xn_ref = refs[-1]
    it = iter(refs)
    x_ref, nw_ref = next(it), next(it)
    epis = [p[0] for p in plan]
    cos_ref, sin_ref = (next(it), next(it)) if any(e != "none" for e in epis) else (None, None)
    gsum_ref = next(it) if "qk" in epis else None
    w_refs, gain_refs = [], []
    for e in epis:
        w_refs.append(next(it))
        gain_refs.append(next(it) if e == "qk" else None)
    outs = list(refs[len(refs) - 1 - sum(p[2] for p in plan):-1])
    o_refs = []
    for p in plan:
        o_refs.append(outs[:p[2]])
        outs = outs[p[2]:]

    def store(seg, cols, val):
        scale = plan[seg][1]
        if scale != 1.0:
            val = val * scale
        for o_ref in o_refs[seg]:
            o_ref[:, cols] = val.astype(o_ref.dtype)

    @pl.when(pl.program_id(1) == 0)
    def _():
        x = x_ref[...]
        ms = jnp.mean(x * x, axis=-1, keepdims=True)
        xn_ref[...] = (x * lax.rsqrt(ms + EPS) * nw_ref[...]).astype(BF16)

    qk = [s for s, e in enumerate(epis) if e == "qk"]
    rest = [s for s, e in enumerate(epis) if e == "rope256"] + [s for s, e in enumerate(epis) if e == "none"]
    slabs = {s: [slice(c * MXU_COLS, (c + 1) * MXU_COLS) for c in range(w_refs[s].shape[1] // MXU_COLS)]
             for s in qk}
    zq = {(s, c): jnp.dot(xn_ref[...], w_refs[s][:, sl], preferred_element_type=F32)
          for s in qk for c, sl in enumerate(slabs[s])}
    sq = {key: jnp.dot((z * z).astype(BF16), gsum_ref[...], preferred_element_type=F32) for key, z in zq.items()}
    zr = {s: jnp.dot(xn_ref[...], w_refs[s][...], preferred_element_type=F32) for s in rest}

    if qk:
        cos = jnp.concatenate([cos_ref[...]] * 2, axis=1)
        sin = jnp.concatenate([sin_ref[...]] * 2, axis=1)
        lane = lax.broadcasted_iota(jnp.int32, (xn_ref.shape[0], MXU_COLS), 1)
        first_half = (lane % A_HD) < (A_HD // 2)
        for (s, c), z in zq.items():
            gain = jnp.concatenate([gain_refs[s][...]] * 2, axis=1)
            zn = z * lax.rsqrt(sq[(s, c)] * (1.0 / A_HD) + EPS) * gain
            partner = jnp.where(first_half,
                                pltpu.roll(zn, MXU_COLS - A_HD // 2, 1),
                                pltpu.roll(zn, A_HD // 2, 1))
            store(s, slabs[s][c], zn * cos + partner * sin)
    for s in rest:
        z = zr[s]
        if epis[s] == "rope256":
            cos = cos_ref[...]
            sin = sin_ref[...]
            for c in range(z.shape[1] // R_DK):
                lo, hi = slice(c * R_DK, c * R_DK + LANES), slice(c * R_DK + LANES, (c + 1) * R_DK)
                x1, x2 = z[:, lo], z[:, hi]
                store(s, lo, x1 * cos - x2 * sin)
                store(s, hi, x2 * cos + x1 * sin)
        else:
            store(s, slice(None), z)


def _seg(w, out_dtypes, epi="none", out_scale=1.0, gain=None, stack=None):
    return dict(w=w, out_dtypes=out_dtypes, epi=epi, out_scale=out_scale, gain=gain, stack=stack)


PROJ_VMEM_BUDGET = V7X_VMEM_BYTES * 3 // 4


def _proj(x, nw, segs, *, cos=None, sin=None, name):
    m, d = x.shape
    tm = _tile(m, 1024)
    epis = [s["epi"] for s in segs]
    assert not ("qk" in epis and "rope256" in epis)
    unit = {"qk": MXU_COLS, "rope256": R_DK, "none": LANES}

    def vmem_bytes(nj):
        total = 2 * tm * d * 4 + tm * d * 2
        for s in segs:
            tn = s["w"].shape[1] // nj
            total += 2 * d * tn * 2 + tm * tn * 4
            total += sum(2 * tm * tn * jnp.dtype(dt).itemsize for dt in s["out_dtypes"])
        return total

    nj = next(c for c in (1, 2, 4, 8, 16, 32)
              if all(s["w"].shape[1] % (c * unit[s["epi"]]) == 0 for s in segs) and vmem_bytes(c) <= PROJ_VMEM_BUDGET)
    in_specs = [pl.BlockSpec((tm, d), lambda i, j: (i, 0)),
                pl.BlockSpec((1, d), lambda i, j: (0, 0))]
    args = [x, nw.reshape(1, d)]
    if any(e != "none" for e in epis):
        in_specs += [pl.BlockSpec((tm, LANES), lambda i, j: (i, 0))] * 2
        args += [cos, sin]
    if "qk" in epis:
        gidx = jnp.arange(MXU_COLS) // A_HD
        in_specs.append(pl.BlockSpec((MXU_COLS, MXU_COLS), lambda i, j: (0, 0)))
        args.append((gidx[:, None] == gidx[None, :]).astype(BF16))
    out_shape, out_specs, bufs = [], [], []
    for s in segs:
        n = s["w"].shape[1]
        tn = n // nj
        in_specs.append(pl.BlockSpec((d, tn), lambda i, j: (0, j)))
        args.append(s["w"])
        if s["epi"] == "qk":
            in_specs.append(pl.BlockSpec((1, LANES), lambda i, j: (0, 0)))
            args.append(jnp.tile(s["gain"].reshape(1, A_HD), (1, LANES // A_HD)))
        for o, dt in enumerate(s["out_dtypes"]):
            if o == 0 and s["stack"] is not None:
                layers, layer, buf = s["stack"]
                out_specs.append(pl.BlockSpec((None, tm, tn), lambda i, j, layer=layer: (layer, i, j)))
                out_shape.append(jax.ShapeDtypeStruct((layers, m, n), dt))
                if buf is not None:
                    bufs.append((len(out_shape) - 1, buf))
            else:
                out_specs.append(pl.BlockSpec((tm, tn), lambda i, j: (i, j)))
                out_shape.append(jax.ShapeDtypeStruct((m, n), dt))
    aliases = {}
    for out_idx, buf in bufs:
        aliases[len(args)] = out_idx
        in_specs.append(pl.BlockSpec(memory_space=pl.ANY))
        args.append(buf)
    res = list(pl.pallas_call(
        functools.partial(_proj_kernel, plan=tuple((s["epi"], s["out_scale"], len(s["out_dtypes"])) for s in segs)),
        out_shape=tuple(out_shape),
        grid=(m // tm, nj),
        in_specs=in_specs,
        out_specs=tuple(out_specs),
        scratch_shapes=[pltpu.VMEM((tm, d), BF16)],
        input_output_aliases=aliases,
        compiler_params=_params(("parallel", "arbitrary")),
        cost_estimate=_cost(2 * m * d * sum(s["w"].shape[1] for s in segs),
                            _nbytes(x) + (m // tm) * _nbytes(*(s["w"] for s in segs))
                            + sum(m * s["w"].shape[1] * jnp.dtype(dt).itemsize for s in segs for dt in s["out_dtypes"])),
        name=name,
    )(*args))
    out = []
    for s in segs:
        out.append(res[:len(s["out_dtypes"])])
        res = res[len(s["out_dtypes"]):]
    return out


def _outproj_kernel(a_ref, w_ref, h_ref, o_ref):
    o_ref[...] = h_ref[...] + jnp.dot(a_ref[...], w_ref[...], preferred_element_type=F32)


def _outproj(a, w, h, *, name):
    m, k = a.shape
    n = w.shape[1]
    tm = _tile(m, 1024)
    tn = _tile(n, 1024)
    return pl.pallas_call(
        _outproj_kernel,
        out_shape=jax.ShapeDtypeStruct((m, n), F32),
        grid=(m // tm, n // tn),
        in_specs=[pl.BlockSpec((tm, k), lambda i, j: (i, 0)),
                  pl.BlockSpec((k, tn), lambda i, j: (0, j)),
                  pl.BlockSpec((tm, tn), lambda i, j: (i, j))],
        out_specs=pl.BlockSpec((tm, tn), lambda i, j: (i, j)),
        compiler_params=_params(("parallel", "parallel")),
        cost_estimate=_cost(2 * m * k * n, _nbytes(a) + (m // tm) * _nbytes(w) + 2 * _nbytes(h)),
        name=name,
    )(a, w, h)


def _lambda(lamp_ref, lam_init):
    lp = lamp_ref[...]
    s1 = jnp.sum(lp[0:1, :] * lp[1:2, :], axis=-1, keepdims=True)
    s2 = jnp.sum(lp[2:3, :] * lp[3:4, :], axis=-1, keepdims=True)
    return jnp.exp(s1) - jnp.exp(s2) + lam_init


V_AUG = A_VD + 16


LOOKAHEAD = 5


def _attn_prompt_kernel(q_ref, qn_ref, k_ref, v_ref, g_ref, lamp_ref, sg_ref, o_ref,
                        qpad_ref, vt_ref, m_ref, acc_ref, s_ref, mx_ref, *, tq, tk, lam_init):
    i = pl.program_id(1)
    slot = i % 2

    sw = s_ref.shape[2]
    n_strips = 2 * tq // sw
    la = min(LOOKAHEAD, n_strips - 1)
    order = sorted(range(n_strips), key=lambda c: -((c * sw) % tq))

    def load_queries(src_ref, dst):
        qt = src_ref[...].astype(F32).T
        row = lax.broadcasted_iota(jnp.int32, qt.shape, 0)
        qpad_ref[dst, :, :tq] = jnp.where(row < A_HD, qt, 0.0).astype(BF16)
        qpad_ref[dst, :, tq:] = jnp.where(row >= A_HD, qt, 0.0).astype(BF16)

    def scores(j, c, rows=tk, src=slot):
        kj = k_ref[pl.ds(pl.multiple_of(j * tk, tk), rows), :]
        s = jnp.dot(kj, qpad_ref[src, :, c * sw:(c + 1) * sw], preferred_element_type=F32)
        s_ref[c, :rows] = s
        mx_ref[c] = jnp.max(s, axis=0, keepdims=True)

    @pl.when(i == 0)
    def _():
        aug = lax.broadcasted_iota(jnp.int32, (V_AUG - A_VD, tk), 0)
        ones_row = jnp.where(aug == 0, 1.0, 0.0).astype(BF16)

        def fill(t, carry):
            vj = v_ref[pl.ds(pl.multiple_of(t * tk, tk), tk), :].astype(F32)
            vt_ref[t, :A_VD, :] = vj.T.astype(BF16)
            vt_ref[t, A_VD:, :] = ones_row
            return carry

        lax.fori_loop(0, vt_ref.shape[0], fill, 0)
        load_queries(q_ref, 0)
        for c in order[:la]:
            scores(0, c, src=0)

    m_ref[...] = jnp.full(m_ref.shape, NEG_BIG, F32)
    acc_ref[...] = jnp.zeros(acc_ref.shape, F32)

    def update(j, c, mask, rows=tk):
        cs = slice(c * sw, (c + 1) * sw)
        s = s_ref[c, :rows]
        if mask is None:
            mx = mx_ref[c]
        else:
            s = jnp.where(mask[:rows], s, NEG_BIG)
            mx = jnp.max(s, axis=0, keepdims=True)
        m_old = m_ref[:, cs]
        m_new = jnp.maximum(m_old, mx)
        alpha = jnp.exp2(m_old - m_new)
        p = jnp.exp2(s - m_new).astype(BF16)
        pv = jnp.dot(vt_ref[j][:, :rows], p, preferred_element_type=F32)
        acc_ref[:, cs] = alpha * acc_ref[:, cs] + pv
        m_ref[:, cs] = m_new

    n_sub = tq // tk
    n_full = i * n_sub

    def full_tiles(j0, count):
        for j in range(j0, j0 + count) if isinstance(j0, int) else [j0 + t for t in range(count)]:
            for n, c in enumerate(order):
                update(j, c, None)
                if n + la < n_strips:
                    scores(j, order[n + la])
                else:
                    scores(j + 1, order[n + la - n_strips])

    odd = i % 2

    @pl.when(odd == 1)
    def _():
        full_tiles(0, n_sub)

    def body(jj, carry):
        full_tiles(odd * n_sub + jj * (2 * n_sub), 2 * n_sub)
        return carry

    lax.fori_loop(0, i // 2, body, 0)

    r = lax.broadcasted_iota(jnp.int32, (tk, sw), 0)
    cc = lax.broadcasted_iota(jnp.int32, (tk, sw), 1)
    units = []
    for d in range(n_sub):
        for c in order:
            q_lo, k_lo = (c * sw) % tq, d * tk
            if q_lo + sw <= k_lo:
                continue
            full = q_lo >= k_lo + tk
            rows = min(tk, q_lo + sw - k_lo)
            units.append((d, c, None if full else (k_lo + r) // CHUNK <= (q_lo + cc) // CHUNK, rows))
    assert [u[:2] + u[3:] for u in units[:la]] == [(0, c, tk) for c in order[:la]]
    load_queries(qn_ref, 1 - slot)
    for n, (d, c, mask, rows) in enumerate(units):
        update(n_full + d, c, mask, rows)
        if n + la < len(units):
            nd, nc, _, nrows = units[n + la]
            assert all(u[1] != nc for u in units[n + 1:n + la])
            scores(n_full + nd, nc, nrows)
    for c in order[:la]:
        scores(0, c, src=1 - slot)

    lam = _lambda(lamp_ref, lam_init)
    acc = acc_ref[...]
    inv_l = 1.0 / acc[A_VD:A_VD + 1, :]
    ot = acc[:A_VD, :tq] * inv_l[:, :tq] - lam * (acc[:A_VD, tq:] * inv_l[:, tq:])
    ms = jnp.mean(ot * ot, axis=0, keepdims=True)
    ot = ot * lax.rsqrt(ms + EPS) * (sg_ref[...] * (1.0 - lam_init))
    g = g_ref[...].astype(F32)
    o_ref[...] = (ot.T * (g * _sigmoid(g))).astype(o_ref.dtype)


def _attn_prompt(q, k, v, g, lamp, sub_gain, lam_init, *, name):
    s, w = q.shape
    heads = w // A_VD
    tq = _tile(s, 1024)
    tk = _tile(tq, 512)
    sw = min(MXU_COLS, tq)
    nq = s // tq
    return pl.pallas_call(
        functools.partial(_attn_prompt_kernel, tq=tq, tk=tk, lam_init=lam_init),
        out_shape=jax.ShapeDtypeStruct((s, w), BF16),
        grid=(heads, nq),
        in_specs=[pl.BlockSpec((tq, A_VD), lambda h, i: (i, h)),
                  pl.BlockSpec((tq, A_VD), lambda h, i: (jnp.minimum(i + 1, nq - 1), h)),
                  pl.BlockSpec((s, A_VD), lambda h, i: (0, h)),
                  pl.BlockSpec((s, A_VD), lambda h, i: (0, h)),
                  pl.BlockSpec((tq, A_VD), lambda h, i: (i, h)),
                  pl.BlockSpec((4, A_HD), lambda h, i: (0, 0)),
                  pl.BlockSpec((A_VD, 1), lambda h, i: (0, 0))],
        out_specs=pl.BlockSpec((tq, A_VD), lambda h, i: (i, h)),
        scratch_shapes=[pltpu.VMEM((2, A_VD, 2 * tq), BF16),
                        pltpu.VMEM((s // tk, V_AUG, tk), BF16),
                        pltpu.VMEM((1, 2 * tq), F32),
                        pltpu.VMEM((V_AUG, 2 * tq), F32),
                        pltpu.VMEM((2 * tq // sw, tk, sw), F32),
                        pltpu.VMEM((2 * tq // sw, 1, sw), F32)],
        compiler_params=_params(("parallel", "arbitrary")),
        cost_estimate=_cost(heads * (s * s // 2) * 2 * 2 * (A_VD + V_AUG), _nbytes(q, k, v, g, q),
                            heads * (s * s // 2) * 2),
        name=name,
    )(q, q, k, v, g, lamp, sub_gain.reshape(A_VD, 1))


SAMPLE_HEADS_PER_STEP = 4


def _attn_sample_kernel(q_ref, kn_ref, vn_ref, g_ref, kc_ref, vc_ref, lamp_ref, sg_ref, o_ref, *, lam_init, hps):
    lam = _lambda(lamp_ref, lam_init)
    nt = (((1,), (1,)), ((), ()))
    lane = lax.broadcasted_iota(jnp.int32, (q_ref.shape[0], A_VD), 1)
    heads = [slice(h * A_VD, (h + 1) * A_VD) for h in range(hps)]
    scores = []
    for hs in heads:
        q = q_ref[:, hs]
        kc = kc_ref[0, :, hs].astype(BF16)
        kn = kn_ref[:, hs].astype(BF16)
        for half in range(2):
            qh = jnp.where((lane >= A_HD) == bool(half), q, jnp.zeros_like(q))
            scores.append((lax.dot_general(qh, kc, nt, preferred_element_type=F32),
                           lax.dot_general(qh, kn, nt, preferred_element_type=F32)))
    probs = []
    for sc, sn in scores:
        m = jnp.maximum(jnp.max(sc, axis=-1, keepdims=True), jnp.max(sn, axis=-1, keepdims=True))
        pc = jnp.exp2(sc - m)
        pn = jnp.exp2(sn - m)
        inv = 1.0 / (jnp.sum(pc, axis=-1, keepdims=True) + jnp.sum(pn, axis=-1, keepdims=True))
        probs.append((pc * inv, pn * inv))
    outs = []
    for h, hs in enumerate(heads):
        ac = (probs[2 * h][0] - lam * probs[2 * h + 1][0]).astype(BF16)
        an = (probs[2 * h][1] - lam * probs[2 * h + 1][1]).astype(BF16)
        outs.append(jnp.dot(ac, vc_ref[0, :, hs].astype(BF16), preferred_element_type=F32)
                    + jnp.dot(an, vn_ref[:, hs].astype(BF16), preferred_element_type=F32))
    for hs, o in zip(heads, outs):
        ms = jnp.mean(o * o, axis=-1, keepdims=True)
        g = g_ref[:, hs].astype(F32)
        o = o * lax.rsqrt(ms + EPS) * (sg_ref[...] * (1.0 - lam_init))
        o_ref[:, hs] = (o * (g * _sigmoid(g))).astype(o_ref.dtype)


def _attn_sample(q, k, v, g, kc, vc, layer, lamp, sub_gain, lam_init, *, name):
    _, bsz, past, w = kc.shape
    heads = w // A_VD
    ln = q.shape[0] // bsz
    assert past % CHUNK == 0 and ln <= CHUNK
    hps = SAMPLE_HEADS_PER_STEP
    assert heads % hps == 0
    row = lambda b, h: (b, h)
    cache = lambda b, h: (layer, b, 0, h)
    return pl.pallas_call(
        functools.partial(_attn_sample_kernel, lam_init=lam_init, hps=hps),
        out_shape=jax.ShapeDtypeStruct(q.shape, BF16),
        grid=(bsz, heads // hps),
        in_specs=[pl.BlockSpec((ln, hps * A_VD), row),
                  pl.BlockSpec((ln, hps * A_VD), row),
                  pl.BlockSpec((ln, hps * A_VD), row),
                  pl.BlockSpec((ln, hps * A_VD), row),
                  pl.BlockSpec((None, 1, past, hps * A_VD), cache),
                  pl.BlockSpec((None, 1, past, hps * A_VD), cache),
                  pl.BlockSpec((4, A_HD), lambda b, h: (0, 0)),
                  pl.BlockSpec((1, A_VD), lambda b, h: (0, 0))],
        out_specs=pl.BlockSpec((ln, hps * A_VD), row),
        compiler_params=_params(("parallel", "parallel")),
        cost_estimate=_cost(bsz * heads * ln * (past + ln) * 2 * 2 * 2 * A_VD,
                            _nbytes(q, k, v, g, q) + 2 * bsz * past * w * jnp.dtype(kc.dtype).itemsize,
                            bsz * heads * ln * (past + ln) * 2),
        name=name,
    )(q, k, v, g, kc, vc, lamp, sub_gain.reshape(1, A_VD))


RET_HEADS_PER_STEP = 8


def _retention_kernel(q_ref, k_ref, v_ref, g_ref, s0_ref, dec_ref, cd_ref, kd_ref, gl_ref,
                      o_ref, sout_ref, st_ref, *, lc, hps):
    c = pl.program_id(2)

    @pl.when(c == 0)
    def _():
        st_ref[...] = s0_ref[0]

    hd = range(hps)
    qs = [q_ref[:, h * R_DK:(h + 1) * R_DK] for h in hd]
    ks = [k_ref[:, h * R_DK:(h + 1) * R_DK] for h in hd]
    vs = [v_ref[:, h * R_DV:(h + 1) * R_DV] for h in hd]
    sts = [st_ref[h] for h in hd]
    nt = (((1,), (1,)), ((), ()))
    inner = [lax.dot_general(qs[h], ks[h], nt, preferred_element_type=F32) for h in hd]
    cross = [jnp.dot(qs[h], sts[h].astype(BF16), preferred_element_type=F32) for h in hd]
    kdec = []
    for h in hd:
        kd = ks[h].astype(F32) * jnp.concatenate([kd_ref[h]] * (R_DK // LANES), axis=1)
        if lc < LANES:
            kd = jnp.concatenate([kd, jnp.zeros((LANES - lc, R_DK), F32)], axis=0)
        kdec.append(kd.T.astype(BF16))
    inner = [(inner[h] * dec_ref[h]).astype(BF16) for h in hd]
    o = [jnp.dot(inner[h], vs[h], preferred_element_type=F32) for h in hd]
    upd = []
    for h in hd:
        vv = vs[h]
        if lc < LANES:
            vv = jnp.concatenate([vv, jnp.zeros((LANES - lc, R_DV), BF16)], axis=0)
        upd.append(jnp.dot(kdec[h], vv, preferred_element_type=F32))
    for h in hd:
        cd = jnp.concatenate([cd_ref[h]] * (R_DV // LANES), axis=1)
        oh = o[h] + cross[h] * cd
        ms = jnp.mean(oh * oh, axis=-1, keepdims=True)
        g = g_ref[:, h * R_DV:(h + 1) * R_DV].astype(F32)
        o_ref[:, h * R_DV:(h + 1) * R_DV] = (oh * lax.rsqrt(ms + EPS) * (g * _sigmoid(g))).astype(o_ref.dtype)
    st_new = []
    for h in hd:
        gl = jnp.concatenate([gl_ref[h]] * (R_DV // LANES), axis=1)
        st_new.append(sts[h] * gl + upd[h])
        st_ref[h] = st_new[h]

    @pl.when(c == pl.num_programs(2) - 1)
    def _():
        for h in hd:
            sout_ref[0, h] = st_new[h]


def _retention(q, k, vg, s0, lc, *, name):
    bsz, heads = s0.shape[:2]
    t = q.shape[0] // bsz
    nc = t // lc
    lg = jnp.log1p(-(2.0 ** (-5.0 - jnp.arange(heads, dtype=F32))))
    idx = jnp.arange(lc, dtype=F32)
    diff = idx[:, None] - idx[None, :]
    dec = jnp.where(diff >= 0, jnp.exp(lg[:, None, None] * jnp.maximum(diff, 0.0)), 0.0)
    cd = jnp.broadcast_to(jnp.exp(lg[:, None] * (idx[None, :] + 1.0))[:, :, None], (heads, lc, LANES))
    kd = jnp.broadcast_to(jnp.exp(lg[:, None] * (lc - 1.0 - idx[None, :]))[:, :, None], (heads, lc, LANES))
    gl = jnp.broadcast_to(jnp.exp(lg * lc)[:, None, None], (heads, 1, LANES))
    hps = RET_HEADS_PER_STEP
    assert heads % hps == 0
    rows = lambda b, h, c: (b * nc + c, h)
    gate = lambda b, h, c: (b * nc + c, heads // hps + h)
    tab = lambda b, h, c: (h, 0, 0)
    state = lambda b, h, c: (b, h, 0, 0)
    return pl.pallas_call(
        functools.partial(_retention_kernel, lc=lc, hps=hps),
        out_shape=(jax.ShapeDtypeStruct((vg.shape[0], heads * R_DV), BF16),
                   jax.ShapeDtypeStruct(s0.shape, F32)),
        grid=(bsz, heads // hps, nc),
        in_specs=[pl.BlockSpec((lc, hps * R_DK), rows),
                  pl.BlockSpec((lc, hps * R_DK), rows),
                  pl.BlockSpec((lc, hps * R_DV), rows),
                  pl.BlockSpec((lc, hps * R_DV), gate),
                  pl.BlockSpec((1, hps, R_DK, R_DV), state),
                  pl.BlockSpec((hps, lc, lc), tab),
                  pl.BlockSpec((hps, lc, LANES), tab),
                  pl.BlockSpec((hps, lc, LANES), tab),
                  pl.BlockSpec((hps, 1, LANES), tab)],
        out_specs=(pl.BlockSpec((lc, hps * R_DV), rows),
                   pl.BlockSpec((1, hps, R_DK, R_DV), state)),
        scratch_shapes=[pltpu.VMEM((hps, R_DK, R_DV), F32)],
        compiler_params=_params(("parallel", "parallel", "arbitrary")),
        cost_estimate=_cost(2 * q.shape[0] * heads * (lc * (R_DK + R_DV) + 2 * R_DK * R_DV),
                            _nbytes(q, k, vg, s0, s0) + q.shape[0] * heads * R_DV * 2,
                            q.shape[0] * heads * R_DV),
        name=name,
    )(q, k, vg, vg, s0, dec, cd, kd, gl)


def _cmlp_kernel(u_ref, g_ref, v_ref, vg_ref, w_ref, b_ref, *out_refs, emit_v):
    if emit_v:
        a_ref, vn_ref = out_refs
    else:
        (a_ref,) = out_refs
    va = _gelu(v_ref[...])
    ms = jnp.mean(va * va, axis=-1, keepdims=True)
    vn = va * lax.rsqrt(ms + EPS) * vg_ref[...]
    if emit_v:
        vn_ref[...] = vn
    vb = vn.astype(BF16)
    gd = vn.shape[1] // M_GROUPS
    for grp in range(M_GROUPS):
        sl = slice(grp * gd, (grp + 1) * gd)
        bias = jnp.concatenate([b_ref[grp]] * (gd // LANES), axis=1)
        mix = jnp.dot(w_ref[grp], vb[:, sl], preferred_element_type=F32) + bias
        g = g_ref[:, sl].astype(F32)
        a_ref[:, sl] = (_gelu(u_ref[:, sl].astype(F32)) * mix * (g * _sigmoid(g))).astype(a_ref.dtype)


def _cmlp(ug, v, v_gain, wmix, bmix, *, emit_v, name):
    m, w = v.shape
    t = wmix.shape[1]
    out_shape = [jax.ShapeDtypeStruct((m, w), BF16)]
    out_specs = [pl.BlockSpec((t, w), lambda i: (i, 0))]
    if emit_v:
        out_shape.append(jax.ShapeDtypeStruct((m, w), F32))
        out_specs.append(pl.BlockSpec((t, w), lambda i: (i, 0)))
    res = pl.pallas_call(
        functools.partial(_cmlp_kernel, emit_v=emit_v),
        out_shape=tuple(out_shape),
        grid=(m // t,),
        in_specs=[pl.BlockSpec((t, w), lambda i: (i, 0)),
                  pl.BlockSpec((t, w), lambda i: (i, 1)),
                  pl.BlockSpec((t, w), lambda i: (i, 0)),
                  pl.BlockSpec((1, w), lambda i: (0, 0)),
                  pl.BlockSpec((M_GROUPS, t, t), lambda i: (0, 0, 0)),
                  pl.BlockSpec((M_GROUPS, t, LANES), lambda i: (0, 0, 0))],
        out_specs=tuple(out_specs),
        compiler_params=_params(("parallel",)),
        cost_estimate=_cost(2 * m * t * w, _nbytes(ug, v) + sum(m * w * jnp.dtype(o.dtype).itemsize for o in out_shape),
                            3 * m * w),
        name=name,
    )(ug, ug, v, v_gain.reshape(1, w), wmix, bmix)
    return res if emit_v else res[0]


def _rope_tables(pos, d, signed):
    inv = ROPE_THETA ** (-jnp.arange(0, d, 2, dtype=F32) / d)
    ang = pos.astype(F32)[:, None] * inv[None, :]
    cos, sin = jnp.cos(ang), jnp.sin(ang)
    reps = LANES // (d // 2)
    if signed:
        return (jnp.tile(cos, (1, reps)),
                jnp.tile(jnp.concatenate([-sin, sin], axis=1), (1, reps // 2)))
    return jnp.tile(cos, (1, reps)), jnp.tile(sin, (1, reps))


def _diff_attn_layer(h, pos, nw, w_in, w_out, q_gain, k_gain, lamp, sub_gain, lam_init,
                     layers, layer, kbuf, vbuf, cache, tag):
    w = w_out.shape[0]
    cos, sin = _rope_tables(pos, A_HD, signed=True)
    wq, wk, wv, wg = (w_in[:, i * w:(i + 1) * w].astype(BF16) for i in range(4))
    (q,), (k, kb), (v, vb), (g,) = _proj(h, nw, [
        _seg(wq, (BF16,), "qk", (A_HD ** -0.5) * LOG2E, q_gain),
        _seg(wk, (F32, BF16), "qk", gain=k_gain, stack=(layers, layer, kbuf)),
        _seg(wv, (F32, BF16), stack=(layers, layer, vbuf)),
        _seg(wg, (BF16,))], cos=cos, sin=sin, name=f"{tag}_proj")
    if cache is None:
        o = _attn_prompt(q, kb, vb, g, lamp, sub_gain, lam_init, name=f"{tag}_attn")
    else:
        kc, vc = cache
        o = _attn_sample(q, kb, vb, g, kc, vc, layer, lamp, sub_gain, lam_init, name=f"{tag}_attn")
    return _outproj(o, w_out.astype(BF16), h, name=f"{tag}_out"), k, v


def _retention_layer(h, pos, nw, w_in, w_out, s0, lc, tag):
    heads = s0.shape[1]
    qk_w, v_w = heads * R_DK, heads * R_DV
    cos, sin = _rope_tables(pos, R_DK, signed=False)
    wq = w_in[:, :qk_w].astype(BF16)
    wk = w_in[:, qk_w:2 * qk_w].astype(BF16)
    assert w_in.shape[1] == 2 * qk_w + 2 * v_w
    wvg = w_in[:, 2 * qk_w:].astype(BF16)
    (q,), (k,), (vg,) = _proj(h, nw, [
        _seg(wq, (BF16,), "rope256"),
        _seg(wk, (BF16,), "rope256", R_DK ** -0.5),
        _seg(wvg, (BF16,))], cos=cos, sin=sin, name=f"{tag}_proj")
    o, s_new = _retention(q, k, vg, s0, lc, name=f"{tag}_ret")
    return _outproj(o, w_out.astype(BF16), h, name=f"{tag}_out"), s_new


def _cmlp_layer(h, nw, w_in, w_out, v_gain, wmix, bmix, emit_v, tag):
    w = w_out.shape[0]
    wu, wv, wg = (w_in[:, i * w:(i + 1) * w].astype(BF16) for i in range(3))
    (ug,), (v,) = _proj(h, nw, [_seg(jnp.concatenate([wu, wg], axis=1), (BF16,)), _seg(wv, (F32,))],
                        name=f"{tag}_proj")
    res = _cmlp(ug, v, v_gain, wmix, bmix, emit_v=emit_v, name=f"{tag}_mix")
    a, vn = res if emit_v else (res, None)
    return _outproj(a, w_out.astype(BF16), h, name=f"{tag}_out"), vn


def _mix_tables(w_s, b_s, chunk_len):
    groups = w_s.shape[0]
    wl = jnp.tril(w_s[:, :chunk_len, :chunk_len])
    reps = M_CHUNK // chunk_len
    eye = jnp.eye(reps, dtype=F32)
    wt = jnp.einsum("ab,gij->gaibj", eye, wl).reshape(groups, M_CHUNK, M_CHUNK)
    bt = jnp.tile(b_s[:, :chunk_len], (1, reps))
    return wt.astype(BF16), jnp.broadcast_to(bt[:, :, None], (groups, M_CHUNK, LANES))


def kernel(x_prompt, x_sample, cache_k_attn, cache_v_attn, state_ret, norm_w, a_w_in, a_w_out, a_q_gain, a_k_gain, a_lam_q1, a_lam_k1, a_lam_q2, a_lam_k2, a_sub_gain, r_w_in, r_w_out, c_w_in, c_w_out, c_v_gain, c_w_s, c_b_s):
    batch, s_len, d = x_prompt.shape
    dec_b, dec_len, _ = x_sample.shape
    past = cache_k_attn.shape[2]
    depth = norm_w.shape[0]
    assert batch == 1 and M_CHUNK % dec_len == 0 and s_len % M_CHUNK == 0

    hp = x_prompt.reshape(s_len, d)
    hs = x_sample.reshape(dec_b * dec_len, d)
    pos_p = jnp.arange(s_len, dtype=jnp.int32)
    pos_s = jnp.tile(past + jnp.arange(dec_len, dtype=jnp.int32), dec_b)

    n_a = a_w_in.shape[0]
    aw = a_w_out.shape[1]
    cache = (cache_k_attn.reshape(n_a, dec_b, past, aw), cache_v_attn.reshape(n_a, dec_b, past, aw))
    kp = vp = kn = vn = None
    sp_l, ss_l, vm_l = [], [], []
    for i in range(depth):
        kind, j = i % N_MIXERS, i // N_MIXERS
        if kind == 0:
            lam_init = 0.8 - 0.6 * math.exp(-0.3 * i)
            lamp = jnp.stack([a_lam_q1[j], a_lam_k1[j], a_lam_q2[j], a_lam_k2[j]])
            args = (norm_w[i], a_w_in[j], a_w_out[j], a_q_gain[j], a_k_gain[j], lamp, a_sub_gain[j], lam_init)
            hp, kp, vp = _diff_attn_layer(hp, pos_p, *args, n_a, j, kp, vp, None, f"l{i}p")
            hs, kn, vn = _diff_attn_layer(hs, pos_s, *args, n_a, j, kn, vn, cache, f"l{i}s")
        elif kind == 1:
            heads = state_ret.shape[2]
            s0 = jnp.zeros((batch, heads, R_DK, R_DV), F32)
            hp, st_p = _retention_layer(hp, pos_p, norm_w[i], r_w_in[j], r_w_out[j], s0,
                                        _tile(s_len, 256), f"l{i}p")
            hs, st_s = _retention_layer(hs, pos_s, norm_w[i], r_w_in[j], r_w_out[j],
                                        state_ret[j].astype(F32), dec_len, f"l{i}s")
            sp_l.append(st_p)
            ss_l.append(st_s)
        else:
            wp, bp = _mix_tables(c_w_s[j], c_b_s[j], M_CHUNK)
            ws, bs = _mix_tables(c_w_s[j], c_b_s[j], dec_len)
            hp, _ = _cmlp_layer(hp, norm_w[i], c_w_in[j], c_w_out[j], c_v_gain[j], wp, bp, False, f"l{i}p")
            hs, v_s = _cmlp_layer(hs, norm_w[i], c_w_in[j], c_w_out[j], c_v_gain[j], ws, bs, True, f"l{i}s")
            vm_l.append(v_s.reshape(dec_b, dec_len, -1))

    return (hp.reshape(batch, s_len, d), hs.reshape(dec_b, dec_len, d),
            kp.reshape(n_a, batch, s_len, aw // A_HD, A_HD), vp.reshape(n_a, batch, s_len, aw // A_VD, A_VD),
            kn.reshape(n_a, dec_b, dec_len, aw // A_HD, A_HD), vn.reshape(n_a, dec_b, dec_len, aw // A_VD, A_VD),
            jnp.stack(sp_l), jnp.stack(ss_l), jnp.stack(vm_l))
```

```python
import functools
import math

import jax
import jax.numpy as jnp
from jax import lax
from jax.experimental import pallas as pl
from jax.experimental.pallas import tpu as pltpu

F32 = jnp.float32
BF16 = jnp.bfloat16

EPS = 1e-6
CHUNK = 64
ROPE_THETA = 10000.0
N_MIXERS = 3
A_HD = 64
A_VD = 2 * A_HD
R_DK = 256
R_DV = 2 * R_DK
M_GROUPS = 8
M_CHUNK = 128
LOG2E = 1.4426950408889634

LANES = 128
MXU_COLS = 256
V7X_VMEM_BYTES = 64 * 1024 * 1024
VMEM_LIMIT = V7X_VMEM_BYTES * 7 // 8
NEG_BIG = -1e30


def _tile(n, pref):
    if n <= pref:
        return n
    t = pref
    while t >= 8:
        if n % t == 0:
            return t
        t -= 8
    return n


def _params(sem):
    return pltpu.CompilerParams(dimension_semantics=sem, vmem_limit_bytes=VMEM_LIMIT)


def _nbytes(*arrays):
    return sum(a.size * jnp.dtype(a.dtype).itemsize for a in arrays)


def _cost(flops, nbytes, transcendentals=0):
    return pl.CostEstimate(flops=int(flops), transcendentals=int(transcendentals), bytes_accessed=int(nbytes))


def _gelu(x):
    return 0.5 * x * (1.0 + jnp.tanh(0.7978845608028654 * (x + 0.044715 * (x * x * x))))


def _sigmoid(x):
    return 1.0 / (1.0 + jnp.exp(-x))


def _proj_kernel(*refs, plan):
    xn_ref = refs[-1]
    it = iter(refs)
    x_ref, nw_ref = next(it), next(it)
    epis = [p[0] for p in plan]
    cos_ref, sin_ref = (next(it), next(it)) if any(e != "none" for e in epis) else (None, None)
    gsum_ref = next(it) if "qk" in epis else None
    w_refs, gain_refs = [], []
    for e in epis:
        w_refs.append(next(it))
        gain_refs.append(next(it) if e == "qk" else None)
    outs = list(refs[len(refs) - 1 - sum(p[2] for p in plan):-1])
    o_refs = []
    for p in plan:
        o_refs.append(outs[:p[2]])
        outs = outs[p[2]:]

    def store(seg, cols, val):
        scale = plan[seg][1]
        if scale != 1.0:
            val = val * scale
        for o_ref in o_refs[seg]:
            o_ref[:, cols] = val.astype(o_ref.dtype)

    @pl.when(pl.program_id(1) == 0)
    def _():
        x = x_ref[...]
        ms = jnp.mean(x * x, axis=-1, keepdims=True)
        xn_ref[...] = (x * lax.rsqrt(ms + EPS) * nw_ref[...]).astype(BF16)

    qk = [s for s, e in enumerate(epis) if e == "qk"]
    rest = [s for s, e in enumerate(epis) if e == "rope256"] + [s for s, e in enumerate(epis) if e == "none"]
    slabs = {s: [slice(c * MXU_COLS, (c + 1) * MXU_COLS) for c in range(w_refs[s].shape[1] // MXU_COLS)]
             for s in qk}
    zq = {(s, c): jnp.dot(xn_ref[...], w_refs[s][:, sl], preferred_element_type=F32)
          for s in qk for c, sl in enumerate(slabs[s])}
    sq = {key: jnp.dot((z * z).astype(BF16), gsum_ref[...], preferred_element_type=F32) for key, z in zq.items()}
    zr = {s: jnp.dot(xn_ref[...], w_refs[s][...], preferred_element_type=F32) for s in rest}

    if qk:
        cos = jnp.concatenate([cos_ref[...]] * 2, axis=1)
        sin = jnp.concatenate([sin_ref[...]] * 2, axis=1)
        lane = lax.broadcasted_iota(jnp.int32, (xn_ref.shape[0], MXU_COLS), 1)
        first_half = (lane % A_HD) < (A_HD // 2)
        for (s, c), z in zq.items():
            gain = jnp.concatenate([gain_refs[s][...]] * 2, axis=1)
            zn = z * lax.rsqrt(sq[(s, c)] * (1.0 / A_HD) + EPS) * gain
            partner = jnp.where(first_half,
                                pltpu.roll(zn, MXU_COLS - A_HD // 2, 1),
                                pltpu.roll(zn, A_HD // 2, 1))
            store(s, slabs[s][c], zn * cos + partner * sin)
    for s in rest:
        z = zr[s]
        if epis[s] == "rope256":
            cos = cos_ref[...]
            sin = sin_ref[...]
            for c in range(z.shape[1] // R_DK):
                lo, hi = slice(c * R_DK, c * R_DK + LANES), slice(c * R_DK + LANES, (c + 1) * R_DK)
                x1, x2 = z[:, lo], z[:, hi]
                store(s, lo, x1 * cos - x2 * sin)
                store(s, hi, x2 * cos + x1 * sin)
        else:
            store(s, slice(None), z)


def _seg(w, out_dtypes, epi="none", out_scale=1.0, gain=None, stack=None):
    return dict(w=w, out_dtypes=out_dtypes, epi=epi, out_scale=out_scale, gain=gain, stack=stack)


PROJ_VMEM_BUDGET = V7X_VMEM_BYTES * 3 // 4


def _proj(x, nw, segs, *, cos=None, sin=None, name):
    m, d = x.shape
    tm = _tile(m, 1024)
    epis = [s["epi"] for s in segs]
    assert not ("qk" in epis and "rope256" in epis)
    unit = {"qk": MXU_COLS, "rope256": R_DK, "none": LANES}

    def vmem_bytes(nj):
        total = 2 * tm * d * 4 + tm * d * 2
        for s in segs:
            tn = s["w"].shape[1] // nj
            total += 2 * d * tn * 2 + tm * tn * 4
            total += sum(2 * tm * tn * jnp.dtype(dt).itemsize for dt in s["out_dtypes"])
        return total

    nj = next(c for c in (1, 2, 4, 8, 16, 32)
              if all(s["w"].shape[1] % (c * unit[s["epi"]]) == 0 for s in segs) and vmem_bytes(c) <= PROJ_VMEM_BUDGET)
    in_specs = [pl.BlockSpec((tm, d), lambda i, j: (i, 0)),
                pl.BlockSpec((1, d), lambda i, j: (0, 0))]
    args = [x, nw.reshape(1, d)]
    if any(e != "none" for e in epis):
        in_specs += [pl.BlockSpec((tm, LANES), lambda i, j: (i, 0))] * 2
        args += [cos, sin]
    if "qk" in epis:
        gidx = jnp.arange(MXU_COLS) // A_HD
        in_specs.append(pl.BlockSpec((MXU_COLS, MXU_COLS), lambda i, j: (0, 0)))
        args.append((gidx[:, None] == gidx[None, :]).astype(BF16))
    out_shape, out_specs, bufs = [], [], []
    for s in segs:
        n = s["w"].shape[1]
        tn = n // nj
        in_specs.append(pl.BlockSpec((d, tn), lambda i, j: (0, j)))
        args.append(s["w"])
        if s["epi"] == "qk":
            in_specs.append(pl.BlockSpec((1, LANES), lambda i, j: (0, 0)))
            args.append(jnp.tile(s["gain"].reshape(1, A_HD), (1, LANES // A_HD)))
        for o, dt in enumerate(s["out_dtypes"]):
            if o == 0 and s["stack"] is not None:
                layers, layer, buf = s["stack"]
                out_specs.append(pl.BlockSpec((None, tm, tn), lambda i, j, layer=layer: (layer, i, j)))
                out_shape.append(jax.ShapeDtypeStruct((layers, m, n), dt))
                if buf is not None:
                    bufs.append((len(out_shape) - 1, buf))
            else:
                out_specs.append(pl.BlockSpec((tm, tn), lambda i, j: (i, j)))
                out_shape.append(jax.ShapeDtypeStruct((m, n), dt))
    aliases = {}
    for out_idx, buf in bufs:
        aliases[len(args)] = out_idx
        in_specs.append(pl.BlockSpec(memory_space=pl.ANY))
        args.append(buf)
    res = list(pl.pallas_call(
        functools.partial(_proj_kernel, plan=tuple((s["epi"], s["out_scale"], len(s["out_dtypes"])) for s in segs)),
        out_shape=tuple(out_shape),
        grid=(m // tm, nj),
        in_specs=in_specs,
        out_specs=tuple(out_specs),
        scratch_shapes=[pltpu.VMEM((tm, d), BF16)],
        input_output_aliases=aliases,
        compiler_params=_params(("parallel", "arbitrary")),
        cost_estimate=_cost(2 * m * d * sum(s["w"].shape[1] for s in segs),
                            _nbytes(x) + (m // tm) * _nbytes(*(s["w"] for s in segs))
                            + sum(m * s["w"].shape[1] * jnp.dtype(dt).itemsize for s in segs for dt in s["out_dtypes"])),
        name=name,
    )(*args))
    out = []
    for s in segs:
        out.append(res[:len(s["out_dtypes"])])
        res = res[len(s["out_dtypes"]):]
    return out


def _outproj_kernel(a_ref, w_ref, h_ref, o_ref):
    o_ref[...] = h_ref[...] + jnp.dot(a_ref[...], w_ref[...], preferred_element_type=F32)


def _outproj(a, w, h, *, name):
    m, k = a.shape
    n = w.shape[1]
    tm = _tile(m, 1024)
    tn = _tile(n, 1024)
    return pl.pallas_call(
        _outproj_kernel,
        out_shape=jax.ShapeDtypeStruct((m, n), F32),
        grid=(m // tm, n // tn),
        in_specs=[pl.BlockSpec((tm, k), lambda i, j: (i, 0)),
                  pl.BlockSpec((k, tn), lambda i, j: (0, j)),
                  pl.BlockSpec((tm, tn), lambda i, j: (i, j))],
        out_specs=pl.BlockSpec((tm, tn), lambda i, j: (i, j)),
        compiler_params=_params(("parallel", "parallel")),
        cost_estimate=_cost(2 * m * k * n, _nbytes(a) + (m // tm) * _nbytes(w) + 2 * _nbytes(h)),
        name=name,
    )(a, w, h)


def _lambda(lamp_ref, lam_init):
    lp = lamp_ref[...]
    s1 = jnp.sum(lp[0:1, :] * lp[1:2, :], axis=-1, keepdims=True)
    s2 = jnp.sum(lp[2:3, :] * lp[3:4, :], axis=-1, keepdims=True)
    return jnp.exp(s1) - jnp.exp(s2) + lam_init


BF16_SUBLANES = 16
V_AUG = A_VD + BF16_SUBLANES


LOOKAHEAD = 5


def _attn_prompt_kernel(q_ref, qn_ref, k_ref, v_ref, g_ref, lamp_ref, sg_ref, o_ref,
                        qpad_ref, vt_ref, m_ref, acc_ref, s_ref, mx_ref, *, tq, tk, lam_init):
    i = pl.program_id(1)
    slot = i % 2

    sw = s_ref.shape[2]
    n_strips = 2 * tq // sw
    la = min(LOOKAHEAD, n_strips - 1)
    order = sorted(range(n_strips), key=lambda c: -((c * sw) % tq))

    def load_queries(src_ref, dst):
        qt = src_ref[...].astype(F32).T
        row = lax.broadcasted_iota(jnp.int32, qt.shape, 0)
        qpad_ref[dst, :, :tq] = jnp.where(row < A_HD, qt, 0.0).astype(BF16)
        qpad_ref[dst, :, tq:] = jnp.where(row >= A_HD, qt, 0.0).astype(BF16)

    def scores(j, c, rows=tk, src=slot):
        kj = k_ref[pl.ds(pl.multiple_of(j * tk, tk), rows), :]
        s = jnp.dot(kj, qpad_ref[src, :, c * sw:(c + 1) * sw], preferred_element_type=F32)
        s_ref[c, :rows] = s
        mx_ref[c] = jnp.max(s, axis=0, keepdims=True)

    @pl.when(i == 0)
    def _():
        aug = lax.broadcasted_iota(jnp.int32, (V_AUG - A_VD, tk), 0)
        ones_row = jnp.where(aug == 0, 1.0, 0.0).astype(BF16)

        def fill(t, carry):
            vj = v_ref[pl.ds(pl.multiple_of(t * tk, tk), tk), :].astype(F32)
            vt_ref[t, :A_VD, :] = vj.T.astype(BF16)
            vt_ref[t, A_VD:, :] = ones_row
            return carry

        lax.fori_loop(0, vt_ref.shape[0], fill, 0)
        load_queries(q_ref, 0)
        for c in order[:la]:
            scores(0, c, src=0)

    m_ref[...] = jnp.full(m_ref.shape, NEG_BIG, F32)
    acc_ref[...] = jnp.zeros(acc_ref.shape, F32)

    def update(j, c, mask, rows=tk):
        cs = slice(c * sw, (c + 1) * sw)
        s = s_ref[c, :rows]
        if mask is None:
            mx = mx_ref[c]
        else:
            s = jnp.where(mask[:rows], s, NEG_BIG)
            mx = jnp.max(s, axis=0, keepdims=True)
        m_old = m_ref[:, cs]
        m_new = jnp.maximum(m_old, mx)
        alpha = jnp.exp2(m_old - m_new)
        p = jnp.exp2(s - m_new).astype(BF16)
        pv = jnp.dot(vt_ref[j][:, :rows], p, preferred_element_type=F32)
        acc_ref[:, cs] = alpha * acc_ref[:, cs] + pv
        m_ref[:, cs] = m_new

    n_sub = tq // tk
    n_full = i * n_sub

    def full_tiles(j0, count):
        for j in range(j0, j0 + count) if isinstance(j0, int) else [j0 + t for t in range(count)]:
            for n, c in enumerate(order):
                update(j, c, None)
                if n + la < n_strips:
                    scores(j, order[n + la])
                else:
                    scores(j + 1, order[n + la - n_strips])

    odd = i % 2

    @pl.when(odd == 1)
    def _():
        full_tiles(0, n_sub)

    def body(jj, carry):
        full_tiles(odd * n_sub + jj * (2 * n_sub), 2 * n_sub)
        return carry

    lax.fori_loop(0, i // 2, body, 0)

    r = lax.broadcasted_iota(jnp.int32, (tk, sw), 0)
    cc = lax.broadcasted_iota(jnp.int32, (tk, sw), 1)
    units = []
    for d in range(n_sub):
        for c in order:
            q_lo, k_lo = (c * sw) % tq, d * tk
            if q_lo + sw <= k_lo:
                continue
            full = q_lo >= k_lo + tk
            rows = min(tk, q_lo + sw - k_lo)
            units.append((d, c, None if full else (k_lo + r) // CHUNK <= (q_lo + cc) // CHUNK, rows))
    assert [u[:2] + u[3:] for u in units[:la]] == [(0, c, tk) for c in order[:la]]
    load_queries(qn_ref, 1 - slot)
    for n, (d, c, mask, rows) in enumerate(units):
        update(n_full + d, c, mask, rows)
        if n + la < len(units):
            nd, nc, _, nrows = units[n + la]
            assert all(u[1] != nc for u in units[n + 1:n + la])
            scores(n_full + nd, nc, nrows)
    for c in order[:la]:
        scores(0, c, src=1 - slot)

    lam = _lambda(lamp_ref, lam_init)
    acc = acc_ref[...]
    inv_l = 1.0 / acc[A_VD:A_VD + 1, :]
    ot = acc[:A_VD, :tq] * inv_l[:, :tq] - lam * (acc[:A_VD, tq:] * inv_l[:, tq:])
    ms = jnp.mean(ot * ot, axis=0, keepdims=True)
    ot = ot * lax.rsqrt(ms + EPS) * (sg_ref[...] * (1.0 - lam_init))
    g = g_ref[...].astype(F32)
    o_ref[...] = (ot.T * (g * _sigmoid(g))).astype(o_ref.dtype)


def _attn_prompt(q, k, v, g, lamp, sub_gain, lam_init, *, name):
    s, w = q.shape
    heads = w // A_VD
    tq = _tile(s, 1024)
    tk = _tile(tq, 512)
    sw = min(MXU_COLS, tq)
    nq = s // tq
    return pl.pallas_call(
        functools.partial(_attn_prompt_kernel, tq=tq, tk=tk, lam_init=lam_init),
        out_shape=jax.ShapeDtypeStruct((s, w), BF16),
        grid=(heads, nq),
        in_specs=[pl.BlockSpec((tq, A_VD), lambda h, i: (i, h)),
                  pl.BlockSpec((tq, A_VD), lambda h, i: (jnp.minimum(i + 1, nq - 1), h)),
                  pl.BlockSpec((s, A_VD), lambda h, i: (0, h)),
                  pl.BlockSpec((s, A_VD), lambda h, i: (0, h)),
                  pl.BlockSpec((tq, A_VD), lambda h, i: (i, h)),
                  pl.BlockSpec((4, A_HD), lambda h, i: (0, 0)),
                  pl.BlockSpec((A_VD, 1), lambda h, i: (0, 0))],
        out_specs=pl.BlockSpec((tq, A_VD), lambda h, i: (i, h)),
        scratch_shapes=[pltpu.VMEM((2, A_VD, 2 * tq), BF16),
                        pltpu.VMEM((s // tk, V_AUG, tk), BF16),
                        pltpu.VMEM((1, 2 * tq), F32),
                        pltpu.VMEM((V_AUG, 2 * tq), F32),
                        pltpu.VMEM((2 * tq // sw, tk, sw), F32),
                        pltpu.VMEM((2 * tq // sw, 1, sw), F32)],
        compiler_params=_params(("parallel", "arbitrary")),
        cost_estimate=_cost(heads * (s * s // 2) * 2 * 2 * (A_VD + V_AUG), _nbytes(q, k, v, g, q),
                            heads * (s * s // 2) * 2),
        name=name,
    )(q, q, k, v, g, lamp, sub_gain.reshape(A_VD, 1))


SAMPLE_HEADS_PER_STEP = 4


def _attn_sample_kernel(q_ref, kn_ref, vn_ref, g_ref, kc_ref, vc_ref, lamp_ref, sg_ref, o_ref, *, lam_init, hps):
    lam = _lambda(lamp_ref, lam_init)
    nt = (((1,), (1,)), ((), ()))
    lane = lax.broadcasted_iota(jnp.int32, (q_ref.shape[0], A_VD), 1)
    heads = [slice(h * A_VD, (h + 1) * A_VD) for h in range(hps)]
    scores = []
    for hs in heads:
        q = q_ref[:, hs]
        kc = kc_ref[0, :, hs].astype(BF16)
        kn = kn_ref[:, hs].astype(BF16)
        for half in range(2):
            qh = jnp.where((lane >= A_HD) == bool(half), q, jnp.zeros_like(q))
            scores.append((lax.dot_general(qh, kc, nt, preferred_element_type=F32),
                           lax.dot_general(qh, kn, nt, preferred_element_type=F32)))
    probs = []
    for sc, sn in scores:
        m = jnp.maximum(jnp.max(sc, axis=-1, keepdims=True), jnp.max(sn, axis=-1, keepdims=True))
        pc = jnp.exp2(sc - m)
        pn = jnp.exp2(sn - m)
        inv = 1.0 / (jnp.sum(pc, axis=-1, keepdims=True) + jnp.sum(pn, axis=-1, keepdims=True))
        probs.append((pc * inv, pn * inv))
    outs = []
    for h, hs in enumerate(heads):
        ac = (probs[2 * h][0] - lam * probs[2 * h + 1][0]).astype(BF16)
        an = (probs[2 * h][1] - lam * probs[2 * h + 1][1]).astype(BF16)
        outs.append(jnp.dot(ac, vc_ref[0, :, hs].astype(BF16), preferred_element_type=F32)
                    + jnp.dot(an, vn_ref[:, hs].astype(BF16), preferred_element_type=F32))
    for hs, o in zip(heads, outs):
        ms = jnp.mean(o * o, axis=-1, keepdims=True)
        g = g_ref[:, hs].astype(F32)
        o = o * lax.rsqrt(ms + EPS) * (sg_ref[...] * (1.0 - lam_init))
        o_ref[:, hs] = (o * (g * _sigmoid(g))).astype(o_ref.dtype)


def _attn_sample(q, k, v, g, kc, vc, layer, lamp, sub_gain, lam_init, *, name):
    _, bsz, past, w = kc.shape
    heads = w // A_VD
    ln = q.shape[0] // bsz
    assert past % CHUNK == 0 and ln <= CHUNK
    hps = SAMPLE_HEADS_PER_STEP
    assert heads % hps == 0
    row = lambda b, h: (b, h)
    cache = lambda b, h: (layer, b, 0, h)
    return pl.pallas_call(
        functools.partial(_attn_sample_kernel, lam_init=lam_init, hps=hps),
        out_shape=jax.ShapeDtypeStruct(q.shape, BF16),
        grid=(bsz, heads // hps),
        in_specs=[pl.BlockSpec((ln, hps * A_VD), row),
                  pl.BlockSpec((ln, hps * A_VD), row),
                  pl.BlockSpec((ln, hps * A_VD), row),
                  pl.BlockSpec((ln, hps * A_VD), row),
                  pl.BlockSpec((None, 1, past, hps * A_VD), cache),
                  pl.BlockSpec((None, 1, past, hps * A_VD), cache),
                  pl.BlockSpec((4, A_HD), lambda b, h: (0, 0)),
                  pl.BlockSpec((1, A_VD), lambda b, h: (0, 0))],
        out_specs=pl.BlockSpec((ln, hps * A_VD), row),
        compiler_params=_params(("parallel", "parallel")),
        cost_estimate=_cost(bsz * heads * ln * (past + ln) * 2 * 2 * 2 * A_VD,
                            _nbytes(q, k, v, g, q) + 2 * bsz * past * w * jnp.dtype(kc.dtype).itemsize,
                            bsz * heads * ln * (past + ln) * 2),
        name=name,
    )(q, k, v, g, kc, vc, lamp, sub_gain.reshape(1, A_VD))


RET_HEADS_PER_STEP = 8


def _retention_kernel(q_ref, k_ref, v_ref, g_ref, s0_ref, dec_ref, cd_ref, kd_ref, gl_ref,
                      o_ref, sout_ref, st_ref, *, lc, hps):
    c = pl.program_id(2)

    @pl.when(c == 0)
    def _():
        st_ref[...] = s0_ref[0]

    hd = range(hps)
    qs = [q_ref[:, h * R_DK:(h + 1) * R_DK] for h in hd]
    ks = [k_ref[:, h * R_DK:(h + 1) * R_DK] for h in hd]
    vs = [v_ref[:, h * R_DV:(h + 1) * R_DV] for h in hd]
    sts = [st_ref[h] for h in hd]
    nt = (((1,), (1,)), ((), ()))
    inner = [lax.dot_general(qs[h], ks[h], nt, preferred_element_type=F32) for h in hd]
    cross = [jnp.dot(qs[h], sts[h].astype(BF16), preferred_element_type=F32) for h in hd]
    kdec = []
    for h in hd:
        kd = ks[h].astype(F32) * jnp.concatenate([kd_ref[h]] * (R_DK // LANES), axis=1)
        if lc < LANES:
            kd = jnp.concatenate([kd, jnp.zeros((LANES - lc, R_DK), F32)], axis=0)
        kdec.append(kd.T.astype(BF16))
    inner = [(inner[h] * dec_ref[h]).astype(BF16) for h in hd]
    o = [jnp.dot(inner[h], vs[h], preferred_element_type=F32) for h in hd]
    upd = []
    for h in hd:
        vv = vs[h]
        if lc < LANES:
            vv = jnp.concatenate([vv, jnp.zeros((LANES - lc, R_DV), BF16)], axis=0)
        upd.append(jnp.dot(kdec[h], vv, preferred_element_type=F32))
    for h in hd:
        cd = jnp.concatenate([cd_ref[h]] * (R_DV // LANES), axis=1)
        oh = o[h] + cross[h] * cd
        ms = jnp.mean(oh * oh, axis=-1, keepdims=True)
        g = g_ref[:, h * R_DV:(h + 1) * R_DV].astype(F32)
        o_ref[:, h * R_DV:(h + 1) * R_DV] = (oh * lax.rsqrt(ms + EPS) * (g * _sigmoid(g))).astype(o_ref.dtype)
    st_new = []
    for h in hd:
        gl = jnp.concatenate([gl_ref[h]] * (R_DV // LANES), axis=1)
        st_new.append(sts[h] * gl + upd[h])
        st_ref[h] = st_new[h]

    @pl.when(c == pl.num_programs(2) - 1)
    def _():
        for h in hd:
            sout_ref[0, h] = st_new[h]


def _retention(q, k, vg, s0, lc, *, name):
    bsz, heads = s0.shape[:2]
    t = q.shape[0] // bsz
    nc = t // lc
    lg = jnp.log1p(-(2.0 ** (-5.0 - jnp.arange(heads, dtype=F32))))
    idx = jnp.arange(lc, dtype=F32)
    diff = idx[:, None] - idx[None, :]
    dec = jnp.where(diff >= 0, jnp.exp(lg[:, None, None] * jnp.maximum(diff, 0.0)), 0.0)
    cd = jnp.broadcast_to(jnp.exp(lg[:, None] * (idx[None, :] + 1.0))[:, :, None], (heads, lc, LANES))
    kd = jnp.broadcast_to(jnp.exp(lg[:, None] * (lc - 1.0 - idx[None, :]))[:, :, None], (heads, lc, LANES))
    gl = jnp.broadcast_to(jnp.exp(lg * lc)[:, None, None], (heads, 1, LANES))
    hps = RET_HEADS_PER_STEP
    assert heads % hps == 0
    rows = lambda b, h, c: (b * nc + c, h)
    gate = lambda b, h, c: (b * nc + c, heads // hps + h)
    tab = lambda b, h, c: (h, 0, 0)
    state = lambda b, h, c: (b, h, 0, 0)
    return pl.pallas_call(
        functools.partial(_retention_kernel, lc=lc, hps=hps),
        out_shape=(jax.ShapeDtypeStruct((vg.shape[0], heads * R_DV), BF16),
                   jax.ShapeDtypeStruct(s0.shape, F32)),
        grid=(bsz, heads // hps, nc),
        in_specs=[pl.BlockSpec((lc, hps * R_DK), rows),
                  pl.BlockSpec((lc, hps * R_DK), rows),
                  pl.BlockSpec((lc, hps * R_DV), rows),
                  pl.BlockSpec((lc, hps * R_DV), gate),
                  pl.BlockSpec((1, hps, R_DK, R_DV), state),
                  pl.BlockSpec((hps, lc, lc), tab),
                  pl.BlockSpec((hps, lc, LANES), tab),
                  pl.BlockSpec((hps, lc, LANES), tab),
                  pl.BlockSpec((hps, 1, LANES), tab)],
        out_specs=(pl.BlockSpec((lc, hps * R_DV), rows),
                   pl.BlockSpec((1, hps, R_DK, R_DV), state)),
        scratch_shapes=[pltpu.VMEM((hps, R_DK, R_DV), F32)],
        compiler_params=_params(("parallel", "parallel", "arbitrary")),
        cost_estimate=_cost(2 * q.shape[0] * heads * (lc * (R_DK + R_DV) + 2 * R_DK * R_DV),
                            _nbytes(q, k, vg, s0, s0) + q.shape[0] * heads * R_DV * 2,
                            q.shape[0] * heads * R_DV),
        name=name,
    )(q, k, vg, vg, s0, dec, cd, kd, gl)


def _cmlp_kernel(u_ref, g_ref, v_ref, vg_ref, w_ref, b_ref, *out_refs, emit_v):
    if emit_v:
        a_ref, vn_ref = out_refs
    else:
        (a_ref,) = out_refs
    va = _gelu(v_ref[...])
    ms = jnp.mean(va * va, axis=-1, keepdims=True)
    vn = va * lax.rsqrt(ms + EPS) * vg_ref[...]
    if emit_v:
        vn_ref[...] = vn
    vb = vn.astype(BF16)
    gd = vn.shape[1] // M_GROUPS
    for grp in range(M_GROUPS):
        sl = slice(grp * gd, (grp + 1) * gd)
        bias = jnp.concatenate([b_ref[grp]] * (gd // LANES), axis=1)
        mix = jnp.dot(w_ref[grp], vb[:, sl], preferred_element_type=F32) + bias
        g = g_ref[:, sl].astype(F32)
        a_ref[:, sl] = (_gelu(u_ref[:, sl].astype(F32)) * mix * (g * _sigmoid(g))).astype(a_ref.dtype)


def _cmlp(ug, v, v_gain, wmix, bmix, *, emit_v, name):
    m, w = v.shape
    t = wmix.shape[1]
    out_shape = [jax.ShapeDtypeStruct((m, w), BF16)]
    out_specs = [pl.BlockSpec((t, w), lambda i: (i, 0))]
    if emit_v:
        out_shape.append(jax.ShapeDtypeStruct((m, w), F32))
        out_specs.append(pl.BlockSpec((t, w), lambda i: (i, 0)))
    res = pl.pallas_call(
        functools.partial(_cmlp_kernel, emit_v=emit_v),
        out_shape=tuple(out_shape),
        grid=(m // t,),
        in_specs=[pl.BlockSpec((t, w), lambda i: (i, 0)),
                  pl.BlockSpec((t, w), lambda i: (i, 1)),
                  pl.BlockSpec((t, w), lambda i: (i, 0)),
                  pl.BlockSpec((1, w), lambda i: (0, 0)),
                  pl.BlockSpec((M_GROUPS, t, t), lambda i: (0, 0, 0)),
                  pl.BlockSpec((M_GROUPS, t, LANES), lambda i: (0, 0, 0))],
        out_specs=tuple(out_specs),
        compiler_params=_params(("parallel",)),
        cost_estimate=_cost(2 * m * t * w, _nbytes(ug, v) + sum(m * w * jnp.dtype(o.dtype).itemsize for o in out_shape),
                            3 * m * w),
        name=name,
    )(ug, ug, v, v_gain.reshape(1, w), wmix, bmix)
    return res if emit_v else res[0]


def _rope_tables(pos, d, signed):
    inv = ROPE_THETA ** (-jnp.arange(0, d, 2, dtype=F32) / d)
    ang = pos.astype(F32)[:, None] * inv[None, :]
    cos, sin = jnp.cos(ang), jnp.sin(ang)
    reps = LANES // (d // 2)
    if signed:
        return (jnp.tile(cos, (1, reps)),
                jnp.tile(jnp.concatenate([-sin, sin], axis=1), (1, reps // 2)))
    return jnp.tile(cos, (1, reps)), jnp.tile(sin, (1, reps))


def _diff_attn_layer(h, pos, nw, w_in, w_out, q_gain, k_gain, lamp, sub_gain, lam_init,
                     layers, layer, kbuf, vbuf, cache, tag):
    w = w_out.shape[0]
    cos, sin = _rope_tables(pos, A_HD, signed=True)
    wq, wk, wv, wg = (w_in[:, i * w:(i + 1) * w].astype(BF16) for i in range(4))
    (q,), (k, kb), (v, vb), (g,) = _proj(h, nw, [
        _seg(wq, (BF16,), "qk", (A_HD ** -0.5) * LOG2E, q_gain),
        _seg(wk, (F32, BF16), "qk", gain=k_gain, stack=(layers, layer, kbuf)),
        _seg(wv, (F32, BF16), stack=(layers, layer, vbuf)),
        _seg(wg, (BF16,))], cos=cos, sin=sin, name=f"{tag}_proj")
    if cache is None:
        o = _attn_prompt(q, kb, vb, g, lamp, sub_gain, lam_init, name=f"{tag}_attn")
    else:
        kc, vc = cache
        o = _attn_sample(q, kb, vb, g, kc, vc, layer, lamp, sub_gain, lam_init, name=f"{tag}_attn")
    return _outproj(o, w_out.astype(BF16), h, name=f"{tag}_out"), k, v


def _retention_layer(h, pos, nw, w_in, w_out, s0, lc, tag):
    heads = s0.shape[1]
    qk_w, v_w = heads * R_DK, heads * R_DV
    cos, sin = _rope_tables(pos, R_DK, signed=False)
    wq = w_in[:, :qk_w].astype(BF16)
    wk = w_in[:, qk_w:2 * qk_w].astype(BF16)
    assert w_in.shape[1] == 2 * qk_w + 2 * v_w
    wvg = w_in[:, 2 * qk_w:].astype(BF16)
    (q,), (k,), (vg,) = _proj(h, nw, [
        _seg(wq, (BF16,), "rope256"),
        _seg(wk, (BF16,), "rope256", R_DK ** -0.5),
        _seg(wvg, (BF16,))], cos=cos, sin=sin, name=f"{tag}_proj")
    o, s_new = _retention(q, k, vg, s0, lc, name=f"{tag}_ret")
    return _outproj(o, w_out.astype(BF16), h, name=f"{tag}_out"), s_new


def _cmlp_layer(h, nw, w_in, w_out, v_gain, wmix, bmix, emit_v, tag):
    w = w_out.shape[0]
    wu, wv, wg = (w_in[:, i * w:(i + 1) * w].astype(BF16) for i in range(3))
    (ug,), (v,) = _proj(h, nw, [_seg(jnp.concatenate([wu, wg], axis=1), (BF16,)), _seg(wv, (F32,))],
                        name=f"{tag}_proj")
    res = _cmlp(ug, v, v_gain, wmix, bmix, emit_v=emit_v, name=f"{tag}_mix")
    a, vn = res if emit_v else (res, None)
    return _outproj(a, w_out.astype(BF16), h, name=f"{tag}_out"), vn


def _mix_tables(w_s, b_s, chunk_len):
    groups = w_s.shape[0]
    wl = jnp.tril(w_s[:, :chunk_len, :chunk_len])
    reps = M_CHUNK // chunk_len
    eye = jnp.eye(reps, dtype=F32)
    wt = jnp.einsum("ab,gij->gaibj", eye, wl).reshape(groups, M_CHUNK, M_CHUNK)
    bt = jnp.tile(b_s[:, :chunk_len], (1, reps))
    return wt.astype(BF16), jnp.broadcast_to(bt[:, :, None], (groups, M_CHUNK, LANES))


def kernel(x_prompt, x_sample, cache_k_attn, cache_v_attn, state_ret, norm_w, a_w_in, a_w_out, a_q_gain, a_k_gain, a_lam_q1, a_lam_k1, a_lam_q2, a_lam_k2, a_sub_gain, r_w_in, r_w_out, c_w_in, c_w_out, c_v_gain, c_w_s, c_b_s):
    batch, s_len, d = x_prompt.shape
    dec_b, dec_len, _ = x_sample.shape
    past = cache_k_attn.shape[2]
    depth = norm_w.shape[0]
    assert batch == 1 and M_CHUNK % dec_len == 0 and s_len % M_CHUNK == 0

    hp = x_prompt.reshape(s_len, d)
    hs = x_sample.reshape(dec_b * dec_len, d)
    pos_p = jnp.arange(s_len, dtype=jnp.int32)
    pos_s = jnp.tile(past + jnp.arange(dec_len, dtype=jnp.int32), dec_b)

    n_a = a_w_in.shape[0]
    aw = a_w_out.shape[1]
    cache = (cache_k_attn.reshape(n_a, dec_b, past, aw), cache_v_attn.reshape(n_a, dec_b, past, aw))
    kp = vp = kn = vn = None
    sp_l, ss_l, vm_l = [], [], []
    for i in range(depth):
        kind, j = i % N_MIXERS, i // N_MIXERS
        if kind == 0:
            lam_init = 0.8 - 0.6 * math.exp(-0.3 * i)
            lamp = jnp.stack([a_lam_q1[j], a_lam_k1[j], a_lam_q2[j], a_lam_k2[j]])
            args = (norm_w[i], a_w_in[j], a_w_out[j], a_q_gain[j], a_k_gain[j], lamp, a_sub_gain[j], lam_init)
            hp, kp, vp = _diff_attn_layer(hp, pos_p, *args, n_a, j, kp, vp, None, f"l{i}p")
            hs, kn, vn = _diff_attn_layer(hs, pos_s, *args, n_a, j, kn, vn, cache, f"l{i}s")
        elif kind == 1:
            heads = state_ret.shape[2]
            s0 = jnp.zeros((batch, heads, R_DK, R_DV), F32)
            hp, st_p = _retention_layer(hp, pos_p, norm_w[i], r_w_in[j], r_w_out[j], s0,
                                        _tile(s_len, 256), f"l{i}p")
            hs, st_s = _retention_layer(hs, pos_s, norm_w[i], r_w_in[j], r_w_out[j],
                                        state_ret[j].astype(F32), dec_len, f"l{i}s")
            sp_l.append(st_p)
            ss_l.append(st_s)
        else:
            wp, bp = _mix_tables(c_w_s[j], c_b_s[j], M_CHUNK)
            ws, bs = _mix_tables(c_w_s[j], c_b_s[j], dec_len)
            hp, _ = _cmlp_layer(hp, norm_w[i], c_w_in[j], c_w_out[j], c_v_gain[j], wp, bp, False, f"l{i}p")
            hs, v_s = _cmlp_layer(hs, norm_w[i], c_w_in[j], c_w_out[j], c_v_gain[j], ws, bs, True, f"l{i}s")
            vm_l.append(v_s.reshape(dec_b, dec_len, -1))

    return (hp.reshape(batch, s_len, d), hs.reshape(dec_b, dec_len, d),
            kp.reshape(n_a, batch, s_len, aw // A_HD, A_HD), vp.reshape(n_a, batch, s_len, aw // A_VD, A_VD),
            kn.reshape(n_a, dec_b, dec_len, aw // A_HD, A_HD), vn.reshape(n_a, dec_b, dec_len, aw // A_VD, A_VD),
            jnp.stack(sp_l), jnp.stack(ss_l), jnp.stack(vm_l))
```

```python
import functools
import math

import jax
import jax.numpy as jnp
from jax import lax
from jax.experimental import pallas as pl
from jax.experimental.pallas import tpu as pltpu

F32 = jnp.float32
BF16 = jnp.bfloat16

EPS = 1e-6
CHUNK = 64
ROPE_THETA = 10000.0
N_MIXERS = 3
A_HD = 64
A_VD = 2 * A_HD
R_DK = 256
R_DV = 2 * R_DK
M_GROUPS = 8
M_CHUNK = 128
LOG2E = 1.4426950408889634

LANES = 128
MXU_COLS = 256
V7X_VMEM_BYTES = 64 * 1024 * 1024
VMEM_LIMIT = V7X_VMEM_BYTES * 7 // 8
NEG_BIG = -1e30


def _tile(n, pref):
    if n <= pref:
        return n
    t = pref
    while t >= 8:
        if n % t == 0:
            return t
        t -= 8
    return n


def _params(sem):
    return pltpu.CompilerParams(dimension_semantics=sem, vmem_limit_bytes=VMEM_LIMIT)


def _nbytes(*arrays):
    return sum(a.size * jnp.dtype(a.dtype).itemsize for a in arrays)


def _cost(flops, nbytes, transcendentals=0):
    return pl.CostEstimate(flops=int(flops), transcendentals=int(transcendentals), bytes_accessed=int(nbytes))


def _gelu(x):
    return 0.5 * x * (1.0 + jnp.tanh(0.7978845608028654 * (x + 0.044715 * (x * x * x))))


def _sigmoid(x):
    return 1.0 / (1.0 + jnp.exp(-x))


def _proj_kernel(*refs, plan):
    xn_ref = refs[-1]
    it = iter(refs)
    x_ref, nw_ref = next(it), next(it)
    epis = [p[0] for p in plan]
    cos_ref, sin_ref = (next(it), next(it)) if any(e != "none" for e in epis) else (None, None)
    gsum_ref = next(it) if "qk" in epis else None
    w_refs, gain_refs = [], []
    for e in epis:
        w_refs.append(next(it))
        gain_refs.append(next(it) if e == "qk" else None)
    outs = list(refs[len(refs) - 1 - sum(p[2] for p in plan):-1])
    o_refs = []
    for p in plan:
        o_refs.append(outs[:p[2]])
        outs = outs[p[2]:]

    def store(seg, cols, val):
        scale = plan[seg][1]
        if scale != 1.0:
            val = val * scale
        for o_ref in o_refs[seg]:
            o_ref[:, cols] = val.astype(o_ref.dtype)

    @pl.when(pl.program_id(1) == 0)
    def _():
        x = x_ref[...]
        ms = jnp.mean(x * x, axis=-1, keepdims=True)
        xn_ref[...] = (x * lax.rsqrt(ms + EPS) * nw_ref[...]).astype(BF16)

    qk = [s for s, e in enumerate(epis) if e == "qk"]
    rest = [s for s, e in enumerate(epis) if e == "rope256"] + [s for s, e in enumerate(epis) if e == "none"]
    slabs = {s: [slice(c * MXU_COLS, (c + 1) * MXU_COLS) for c in range(w_refs[s].shape[1] // MXU_COLS)]
             for s in qk}
    zq = {(s, c): jnp.dot(xn_ref[...], w_refs[s][:, sl], preferred_element_type=F32)
          for s in qk for c, sl in enumerate(slabs[s])}
    sq = {key: jnp.dot((z * z).astype(BF16), gsum_ref[...], preferred_element_type=F32) for key, z in zq.items()}
    zr = {s: jnp.dot(xn_ref[...], w_refs[s][...], preferred_element_type=F32) for s in rest}

    if qk:
        cos = jnp.concatenate([cos_ref[...]] * 2, axis=1)
        sin = jnp.concatenate([sin_ref[...]] * 2, axis=1)
        lane = lax.broadcasted_iota(jnp.int32, (xn_ref.shape[0], MXU_COLS), 1)
        first_half = (lane % A_HD) < (A_HD // 2)
        for (s, c), z in zq.items():
            gain = jnp.concatenate([gain_refs[s][...]] * 2, axis=1)
            zn = z * lax.rsqrt(sq[(s, c)] * (1.0 / A_HD) + EPS) * gain
            partner = jnp.where(first_half,
                                pltpu.roll(zn, MXU_COLS - A_HD // 2, 1),
                                pltpu.roll(zn, A_HD // 2, 1))
            store(s, slabs[s][c], zn * cos + partner * sin)
    for s in rest:
        z = zr[s]
        if epis[s] == "rope256":
            cos = cos_ref[...]
            sin = sin_ref[...]
            for c in range(z.shape[1] // R_DK):
                lo, hi = slice(c * R_DK, c * R_DK + LANES), slice(c * R_DK + LANES, (c + 1) * R_DK)
                x1, x2 = z[:, lo], z[:, hi]
                store(s, lo, x1 * cos - x2 * sin)
                store(s, hi, x2 * cos + x1 * sin)
        else:
            store(s, slice(None), z)


def _seg(w, out_dtypes, epi="none", out_scale=1.0, gain=None, stack=None):
    return dict(w=w, out_dtypes=out_dtypes, epi=epi, out_scale=out_scale, gain=gain, stack=stack)


PROJ_VMEM_BUDGET = V7X_VMEM_BYTES * 3 // 4


def _proj(x, nw, segs, *, cos=None, sin=None, name):
    m, d = x.shape
    tm = _tile(m, 1024)
    epis = [s["epi"] for s in segs]
    assert not ("qk" in epis and "rope256" in epis)
    unit = {"qk": MXU_COLS, "rope256": R_DK, "none": LANES}

    def vmem_bytes(nj):
        total = 2 * tm * d * 4 + tm * d * 2
        for s in segs:
            tn = s["w"].shape[1] // nj
            total += 2 * d * tn * 2 + tm * tn * 4
            total += sum(2 * tm * tn * jnp.dtype(dt).itemsize for dt in s["out_dtypes"])
        return total

    nj = next(c for c in (1, 2, 4, 8, 16, 32)
              if all(s["w"].shape[1] % (c * unit[s["epi"]]) == 0 for s in segs) and vmem_bytes(c) <= PROJ_VMEM_BUDGET)
    in_specs = [pl.BlockSpec((tm, d), lambda i, j: (i, 0)),
                pl.BlockSpec((1, d), lambda i, j: (0, 0))]
    args = [x, nw.reshape(1, d)]
    if any(e != "none" for e in epis):
        in_specs += [pl.BlockSpec((tm, LANES), lambda i, j: (i, 0))] * 2
        args += [cos, sin]
    if "qk" in epis:
        gidx = jnp.arange(MXU_COLS) // A_HD
        in_specs.append(pl.BlockSpec((MXU_COLS, MXU_COLS), lambda i, j: (0, 0)))
        args.append((gidx[:, None] == gidx[None, :]).astype(BF16))
    out_shape, out_specs, bufs = [], [], []
    for s in segs:
        n = s["w"].shape[1]
        tn = n // nj
        in_specs.append(pl.BlockSpec((d, tn), lambda i, j: (0, j)))
        args.append(s["w"])
        if s["epi"] == "qk":
            in_specs.append(pl.BlockSpec((1, LANES), lambda i, j: (0, 0)))
            args.append(jnp.tile(s["gain"].reshape(1, A_HD), (1, LANES // A_HD)))
        for o, dt in enumerate(s["out_dtypes"]):
            if o == 0 and s["stack"] is not None:
                layers, layer, buf = s["stack"]
                out_specs.append(pl.BlockSpec((None, tm, tn), lambda i, j, layer=layer: (layer, i, j)))
                out_shape.append(jax.ShapeDtypeStruct((layers, m, n), dt))
                if buf is not None:
                    bufs.append((len(out_shape) - 1, buf))
            else:
                out_specs.append(pl.BlockSpec((tm, tn), lambda i, j: (i, j)))
                out_shape.append(jax.ShapeDtypeStruct((m, n), dt))
    aliases = {}
    for out_idx, buf in bufs:
        aliases[len(args)] = out_idx
        in_specs.append(pl.BlockSpec(memory_space=pl.ANY))
        args.append(buf)
    res = list(pl.pallas_call(
        functools.partial(_proj_kernel, plan=tuple((s["epi"], s["out_scale"], len(s["out_dtypes"])) for s in segs)),
        out_shape=tuple(out_shape),
        grid=(m // tm, nj),
        in_specs=in_specs,
        out_specs=tuple(out_specs),
        scratch_shapes=[pltpu.VMEM((tm, d), BF16)],
        input_output_aliases=aliases,
        compiler_params=_params(("parallel", "arbitrary")),
        cost_estimate=_cost(2 * m * d * sum(s["w"].shape[1] for s in segs),
                            _nbytes(x) + (m // tm) * _nbytes(*(s["w"] for s in segs))
                            + sum(m * s["w"].shape[1] * jnp.dtype(dt).itemsize for s in segs for dt in s["out_dtypes"])),
        name=name,
    )(*args))
    out = []
    for s in segs:
        out.append(res[:len(s["out_dtypes"])])
        res = res[len(s["out_dtypes"]):]
    return out


def _outproj_kernel(a_ref, w_ref, h_ref, o_ref):
    o_ref[...] = h_ref[...] + jnp.dot(a_ref[...], w_ref[...], preferred_element_type=F32)


def _outproj(a, w, h, *, name):
    m, k = a.shape
    n = w.shape[1]
    tm = _tile(m, 1024)
    tn = _tile(n, 1024)
    return pl.pallas_call(
        _outproj_kernel,
        out_shape=jax.ShapeDtypeStruct((m, n), F32),
        grid=(m // tm, n // tn),
        in_specs=[pl.BlockSpec((tm, k), lambda i, j: (i, 0)),
                  pl.BlockSpec((k, tn), lambda i, j: (0, j)),
                  pl.BlockSpec((tm, tn), lambda i, j: (i, j))],
        out_specs=pl.BlockSpec((tm, tn), lambda i, j: (i, j)),
        compiler_params=_params(("parallel", "parallel")),
        cost_estimate=_cost(2 * m * k * n, _nbytes(a) + (m // tm) * _nbytes(w) + 2 * _nbytes(h)),
        name=name,
    )(a, w, h)


def _lambda(lamp_ref, lam_init):
    lp = lamp_ref[...]
    s1 = jnp.sum(lp[0:1, :] * lp[1:2, :], axis=-1, keepdims=True)
    s2 = jnp.sum(lp[2:3, :] * lp[3:4, :], axis=-1, keepdims=True)
    return jnp.exp(s1) - jnp.exp(s2) + lam_init


BF16_SUBLANES = 16
V_AUG = A_VD + BF16_SUBLANES


LOOKAHEAD = 5


def _attn_prompt_kernel(q_ref, qn_ref, k_ref, v_ref, g_ref, lamp_ref, sg_ref, o_ref,
                        qpad_ref, vt_ref, m_ref, acc_ref, s_ref, mx_ref, *, tq, tk, lam_init):
    i = pl.program_id(1)
    slot = i % 2

    sw = s_ref.shape[2]
    n_strips = 2 * tq // sw
    la = min(LOOKAHEAD, n_strips - 1)
    order = sorted(range(n_strips), key=lambda c: -((c * sw) % tq))

    def load_queries(src_ref, dst):
        qt = src_ref[...].astype(F32).T
        row = lax.broadcasted_iota(jnp.int32, qt.shape, 0)
        qpad_ref[dst, :, :tq] = jnp.where(row < A_HD, qt, 0.0).astype(BF16)
        qpad_ref[dst, :, tq:] = jnp.where(row >= A_HD, qt, 0.0).astype(BF16)

    def scores(j, c, rows=tk, src=slot):
        kj = k_ref[pl.ds(pl.multiple_of(j * tk, tk), rows), :]
        s = jnp.dot(kj, qpad_ref[src, :, c * sw:(c + 1) * sw], preferred_element_type=F32)
        s_ref[c, :rows] = s
        mx_ref[c] = jnp.max(s, axis=0, keepdims=True)

    @pl.when(i == 0)
    def _():
        aug = lax.broadcasted_iota(jnp.int32, (V_AUG - A_VD, tk), 0)
        ones_row = jnp.where(aug == 0, 1.0, 0.0).astype(BF16)

        def fill(t, carry):
            vj = v_ref[pl.ds(pl.multiple_of(t * tk, tk), tk), :].astype(F32)
            vt_ref[t, :A_VD, :] = vj.T.astype(BF16)
            vt_ref[t, A_VD:, :] = ones_row
            return carry

        lax.fori_loop(0, vt_ref.shape[0], fill, 0)
        load_queries(q_ref, 0)
        for c in order[:la]:
            scores(0, c, src=0)

    m_ref[...] = jnp.full(m_ref.shape, NEG_BIG, F32)
    acc_ref[...] = jnp.zeros(acc_ref.shape, F32)

    def update(j, c, mask, rows=tk):
        cs = slice(c * sw, (c + 1) * sw)
        s = s_ref[c, :rows]
        if mask is None:
            mx = mx_ref[c]
        else:
            s = jnp.where(mask[:rows], s, NEG_BIG)
            mx = jnp.max(s, axis=0, keepdims=True)
        m_old = m_ref[:, cs]
        m_new = jnp.maximum(m_old, mx)
        alpha = jnp.exp2(m_old - m_new)
        p = jnp.exp2(s - m_new).astype(BF16)
        pv = jnp.dot(vt_ref[j][:, :rows], p, preferred_element_type=F32)
        acc_ref[:, cs] = alpha * acc_ref[:, cs] + pv
        m_ref[:, cs] = m_new

    n_sub = tq // tk
    n_full = i * n_sub

    def full_tiles(j0, count):
        for j in range(j0, j0 + count) if isinstance(j0, int) else [j0 + t for t in range(count)]:
            for n, c in enumerate(order):
                update(j, c, None)
                if n + la < n_strips:
                    scores(j, order[n + la])
                else:
                    scores(j + 1, order[n + la - n_strips])

    odd = i % 2

    @pl.when(odd == 1)
    def _():
        full_tiles(0, n_sub)

    def body(jj, carry):
        full_tiles(odd * n_sub + jj * (2 * n_sub), 2 * n_sub)
        return carry

    lax.fori_loop(0, i // 2, body, 0)

    r = lax.broadcasted_iota(jnp.int32, (tk, sw), 0)
    cc = lax.broadcasted_iota(jnp.int32, (tk, sw), 1)
    units = []
    for d in range(n_sub):
        for c in order:
            q_lo, k_lo = (c * sw) % tq, d * tk
            if q_lo + sw <= k_lo:
                continue
            full = q_lo >= k_lo + tk
            rows = min(tk, q_lo + sw - k_lo)
            units.append((d, c, None if full else (k_lo + r) // CHUNK <= (q_lo + cc) // CHUNK, rows))
    assert [u[:2] + u[3:] for u in units[:la]] == [(0, c, tk) for c in order[:la]]
    load_queries(qn_ref, 1 - slot)
    for n, (d, c, mask, rows) in enumerate(units):
        update(n_full + d, c, mask, rows)
        if n + la < len(units):
            nd, nc, _, nrows = units[n + la]
            assert all(u[1] != nc for u in units[n + 1:n + la])
            scores(n_full + nd, nc, nrows)
    for c in order[:la]:
        scores(0, c, src=1 - slot)

    lam = _lambda(lamp_ref, lam_init)
    acc = acc_ref[...]
    inv_l = 1.0 / acc[A_VD:A_VD + 1, :]
    ot = acc[:A_VD, :tq] * inv_l[:, :tq] - lam * (acc[:A_VD, tq:] * inv_l[:, tq:])
    ms = jnp.mean(ot * ot, axis=0, keepdims=True)
    ot = ot * lax.rsqrt(ms + EPS) * (sg_ref[...] * (1.0 - lam_init))
    g = g_ref[...].astype(F32)
    o_ref[...] = (ot.T * (g * _sigmoid(g))).astype(o_ref.dtype)


def _attn_prompt(q, k, v, g, lamp, sub_gain, lam_init, *, name):
    s, w = q.shape
    heads = w // A_VD
    tq = _tile(s, 1024)
    tk = _tile(tq, 512)
    sw = min(MXU_COLS, tq)
    nq = s // tq
    return pl.pallas_call(
        functools.partial(_attn_prompt_kernel, tq=tq, tk=tk, lam_init=lam_init),
        out_shape=jax.ShapeDtypeStruct((s, w), BF16),
        grid=(heads, nq),
        in_specs=[pl.BlockSpec((tq, A_VD), lambda h, i: (i, h)),
                  pl.BlockSpec((tq, A_VD), lambda h, i: (jnp.minimum(i + 1, nq - 1), h)),
                  pl.BlockSpec((s, A_VD), lambda h, i: (0, h)),
                  pl.BlockSpec((s, A_VD), lambda h, i: (0, h)),
                  pl.BlockSpec((tq, A_VD), lambda h, i: (i, h)),
                  pl.BlockSpec((4, A_HD), lambda h, i: (0, 0)),
                  pl.BlockSpec((A_VD, 1), lambda h, i: (0, 0))],
        out_specs=pl.BlockSpec((tq, A_VD), lambda h, i: (i, h)),
        scratch_shapes=[pltpu.VMEM((2, A_VD, 2 * tq), BF16),
                        pltpu.VMEM((s // tk, V_AUG, tk), BF16),
                        pltpu.VMEM((1, 2 * tq), F32),
                        pltpu.VMEM((V_AUG, 2 * tq), F32),
                        pltpu.VMEM((2 * tq // sw, tk, sw), F32),
                        pltpu.VMEM((2 * tq // sw, 1, sw), F32)],
        compiler_params=_params(("parallel", "arbitrary")),
        cost_estimate=_cost(heads * (s * s // 2) * 2 * 2 * (A_VD + V_AUG), _nbytes(q, k, v, g, q),
                            heads * (s * s // 2) * 2),
        name=name,
    )(q, q, k, v, g, lamp, sub_gain.reshape(A_VD, 1))


SAMPLE_HEADS_PER_STEP = 4


def _attn_sample_kernel(q_ref, kn_ref, vn_ref, g_ref, kc_ref, vc_ref, lamp_ref, sg_ref, o_ref, *, lam_init, hps):
    lam = _lambda(lamp_ref, lam_init)
    nt = (((1,), (1,)), ((), ()))
    lane = lax.broadcasted_iota(jnp.int32, (q_ref.shape[0], A_VD), 1)
    heads = [slice(h * A_VD, (h + 1) * A_VD) for h in range(hps)]
    scores = []
    for hs in heads:
        q = q_ref[:, hs]
        kc = kc_ref[0, :, hs].astype(BF16)
        kn = kn_ref[:, hs].astype(BF16)
        for half in range(2):
            qh = jnp.where((lane >= A_HD) == bool(half), q, jnp.zeros_like(q))
            scores.append((lax.dot_general(qh, kc, nt, preferred_element_type=F32),
                           lax.dot_general(qh, kn, nt, preferred_element_type=F32)))
    probs = []
    for sc, sn in scores:
        m = jnp.maximum(jnp.max(sc, axis=-1, keepdims=True), jnp.max(sn, axis=-1, keepdims=True))
        pc = jnp.exp2(sc - m)
        pn = jnp.exp2(sn - m)
        inv = 1.0 / (jnp.sum(pc, axis=-1, keepdims=True) + jnp.sum(pn, axis=-1, keepdims=True))
        probs.append((pc * inv, pn * inv))
    outs = []
    for h, hs in enumerate(heads):
        ac = (probs[2 * h][0] - lam * probs[2 * h + 1][0]).astype(BF16)
        an = (probs[2 * h][1] - lam * probs[2 * h + 1][1]).astype(BF16)
        outs.append(jnp.dot(ac, vc_ref[0, :, hs].astype(BF16), preferred_element_type=F32)
                    + jnp.dot(an, vn_ref[:, hs].astype(BF16), preferred_element_type=F32))
    for hs, o in zip(heads, outs):
        ms = jnp.mean(o * o, axis=-1, keepdims=True)
        g = g_ref[:, hs].astype(F32)
        o = o * lax.rsqrt(ms + EPS) * (sg_ref[...] * (1.0 - lam_init))
        o_ref[:, hs] = (o * (g * _sigmoid(g))).astype(o_ref.dtype)


def _attn_sample(q, k, v, g, kc, vc, layer, lamp, sub_gain, lam_init, *, name):
    _, bsz, past, w = kc.shape
    heads = w // A_VD
    ln = q.shape[0] // bsz
    assert past % CHUNK == 0 and ln <= CHUNK
    hps = SAMPLE_HEADS_PER_STEP
    assert heads % hps == 0
    row = lambda b, h: (b, h)
    cache = lambda b, h: (layer, b, 0, h)
    return pl.pallas_call(
        functools.partial(_attn_sample_kernel, lam_init=lam_init, hps=hps),
        out_shape=jax.ShapeDtypeStruct(q.shape, BF16),
        grid=(bsz, heads // hps),
        in_specs=[pl.BlockSpec((ln, hps * A_VD), row),
                  pl.BlockSpec((ln, hps * A_VD), row),
                  pl.BlockSpec((ln, hps * A_VD), row),
                  pl.BlockSpec((ln, hps * A_VD), row),
                  pl.BlockSpec((None, 1, past, hps * A_VD), cache),
                  pl.BlockSpec((None, 1, past, hps * A_VD), cache),
                  pl.BlockSpec((4, A_HD), lambda b, h: (0, 0)),
                  pl.BlockSpec((1, A_VD), lambda b, h: (0, 0))],
        out_specs=pl.BlockSpec((ln, hps * A_VD), row),
        compiler_params=_params(("parallel", "parallel")),
        cost_estimate=_cost(bsz * heads * ln * (past + ln) * 2 * 2 * 2 * A_VD,
                            _nbytes(q, k, v, g, q) + _nbytes(kc, vc) // kc.shape[0],
                            bsz * heads * ln * (past + ln) * 2),
        name=name,
    )(q, k, v, g, kc, vc, lamp, sub_gain.reshape(1, A_VD))


RET_HEADS_PER_STEP = 8


def _retention_kernel(q_ref, k_ref, v_ref, g_ref, s0_ref, dec_ref, cd_ref, kd_ref, gl_ref,
                      o_ref, sout_ref, st_ref, *, lc, hps):
    c = pl.program_id(2)

    @pl.when(c == 0)
    def _():
        st_ref[...] = s0_ref[0]

    hd = range(hps)
    qs = [q_ref[:, h * R_DK:(h + 1) * R_DK] for h in hd]
    ks = [k_ref[:, h * R_DK:(h + 1) * R_DK] for h in hd]
    vs = [v_ref[:, h * R_DV:(h + 1) * R_DV] for h in hd]
    sts = [st_ref[h] for h in hd]
    nt = (((1,), (1,)), ((), ()))
    inner = [lax.dot_general(qs[h], ks[h], nt, preferred_element_type=F32) for h in hd]
    cross = [jnp.dot(qs[h], sts[h].astype(BF16), preferred_element_type=F32) for h in hd]
    kdec = []
    for h in hd:
        kd = ks[h].astype(F32) * jnp.concatenate([kd_ref[h]] * (R_DK // LANES), axis=1)
        if lc < LANES:
            kd = jnp.concatenate([kd, jnp.zeros((LANES - lc, R_DK), F32)], axis=0)
        kdec.append(kd.T.astype(BF16))
    inner = [(inner[h] * dec_ref[h]).astype(BF16) for h in hd]
    o = [jnp.dot(inner[h], vs[h], preferred_element_type=F32) for h in hd]
    upd = []
    for h in hd:
        vv = vs[h]
        if lc < LANES:
            vv = jnp.concatenate([vv, jnp.zeros((LANES - lc, R_DV), BF16)], axis=0)
        upd.append(jnp.dot(kdec[h], vv, preferred_element_type=F32))
    for h in hd:
        cd = jnp.concatenate([cd_ref[h]] * (R_DV // LANES), axis=1)
        oh = o[h] + cross[h] * cd
        ms = jnp.mean(oh * oh, axis=-1, keepdims=True)
        g = g_ref[:, h * R_DV:(h + 1) * R_DV].astype(F32)
        o_ref[:, h * R_DV:(h + 1) * R_DV] = (oh * lax.rsqrt(ms + EPS) * (g * _sigmoid(g))).astype(o_ref.dtype)
    st_new = []
    for h in hd:
        gl = jnp.concatenate([gl_ref[h]] * (R_DV // LANES), axis=1)
        st_new.append(sts[h] * gl + upd[h])
        st_ref[h] = st_new[h]

    @pl.when(c == pl.num_programs(2) - 1)
    def _():
        for h in hd:
            sout_ref[0, h] = st_new[h]


def _retention(q, k, vg, s0, lc, *, name):
    bsz, heads = s0.shape[:2]
    t = q.shape[0] // bsz
    nc = t // lc
    lg = jnp.log1p(-(2.0 ** (-5.0 - jnp.arange(heads, dtype=F32))))
    idx = jnp.arange(lc, dtype=F32)
    diff = idx[:, None] - idx[None, :]
    dec = jnp.where(diff >= 0, jnp.exp(lg[:, None, None] * jnp.maximum(diff, 0.0)), 0.0)
    cd = jnp.broadcast_to(jnp.exp(lg[:, None] * (idx[None, :] + 1.0))[:, :, None], (heads, lc, LANES))
    kd = jnp.broadcast_to(jnp.exp(lg[:, None] * (lc - 1.0 - idx[None, :]))[:, :, None], (heads, lc, LANES))
    gl = jnp.broadcast_to(jnp.exp(lg * lc)[:, None, None], (heads, 1, LANES))
    hps = RET_HEADS_PER_STEP
    assert heads % hps == 0
    rows = lambda b, h, c: (b * nc + c, h)
    gate = lambda b, h, c: (b * nc + c, heads // hps + h)
    tab = lambda b, h, c: (h, 0, 0)
    state = lambda b, h, c: (b, h, 0, 0)
    return pl.pallas_call(
        functools.partial(_retention_kernel, lc=lc, hps=hps),
        out_shape=(jax.ShapeDtypeStruct((vg.shape[0], heads * R_DV), BF16),
                   jax.ShapeDtypeStruct(s0.shape, F32)),
        grid=(bsz, heads // hps, nc),
        in_specs=[pl.BlockSpec((lc, hps * R_DK), rows),
                  pl.BlockSpec((lc, hps * R_DK), rows),
                  pl.BlockSpec((lc, hps * R_DV), rows),
                  pl.BlockSpec((lc, hps * R_DV), gate),
                  pl.BlockSpec((1, hps, R_DK, R_DV), state),
                  pl.BlockSpec((hps, lc, lc), tab),
                  pl.BlockSpec((hps, lc, LANES), tab),
                  pl.BlockSpec((hps, lc, LANES), tab),
                  pl.BlockSpec((hps, 1, LANES), tab)],
        out_specs=(pl.BlockSpec((lc, hps * R_DV), rows),
                   pl.BlockSpec((1, hps, R_DK, R_DV), state)),
        scratch_shapes=[pltpu.VMEM((hps, R_DK, R_DV), F32)],
        compiler_params=_params(("parallel", "parallel", "arbitrary")),
        cost_estimate=_cost(2 * q.shape[0] * heads * (lc * (R_DK + R_DV) + 2 * R_DK * R_DV),
                            _nbytes(q, k, vg, s0, s0) + q.shape[0] * heads * R_DV * 2,
                            q.shape[0] * heads * R_DV),
        name=name,
    )(q, k, vg, vg, s0, dec, cd, kd, gl)


def _cmlp_kernel(u_ref, g_ref, v_ref, vg_ref, w_ref, b_ref, *out_refs, emit_v):
    if emit_v:
        a_ref, vn_ref = out_refs
    else:
        (a_ref,) = out_refs
    va = _gelu(v_ref[...])
    ms = jnp.mean(va * va, axis=-1, keepdims=True)
    vn = va * lax.rsqrt(ms + EPS) * vg_ref[...]
    if emit_v:
        vn_ref[...] = vn
    vb = vn.astype(BF16)
    gd = vn.shape[1] // M_GROUPS
    for grp in range(M_GROUPS):
        sl = slice(grp * gd, (grp + 1) * gd)
        bias = jnp.concatenate([b_ref[grp]] * (gd // LANES), axis=1)
        mix = jnp.dot(w_ref[grp], vb[:, sl], preferred_element_type=F32) + bias
        g = g_ref[:, sl].astype(F32)
        a_ref[:, sl] = (_gelu(u_ref[:, sl].astype(F32)) * mix * (g * _sigmoid(g))).astype(a_ref.dtype)


def _cmlp(ug, v, v_gain, wmix, bmix, *, emit_v, name):
    m, w = v.shape
    t = wmix.shape[1]
    out_shape = [jax.ShapeDtypeStruct((m, w), BF16)]
    out_specs = [pl.BlockSpec((t, w), lambda i: (i, 0))]
    if emit_v:
        out_shape.append(jax.ShapeDtypeStruct((m, w), F32))
        out_specs.append(pl.BlockSpec((t, w), lambda i: (i, 0)))
    res = pl.pallas_call(
        functools.partial(_cmlp_kernel, emit_v=emit_v),
        out_shape=tuple(out_shape),
        grid=(m // t,),
        in_specs=[pl.BlockSpec((t, w), lambda i: (i, 0)),
                  pl.BlockSpec((t, w), lambda i: (i, 1)),
                  pl.BlockSpec((t, w), lambda i: (i, 0)),
                  pl.BlockSpec((1, w), lambda i: (0, 0)),
                  pl.BlockSpec((M_GROUPS, t, t), lambda i: (0, 0, 0)),
                  pl.BlockSpec((M_GROUPS, t, LANES), lambda i: (0, 0, 0))],
        out_specs=tuple(out_specs),
        compiler_params=_params(("parallel",)),
        cost_estimate=_cost(2 * m * t * w, _nbytes(ug, v) + sum(m * w * jnp.dtype(o.dtype).itemsize for o in out_shape),
                            3 * m * w),
        name=name,
    )(ug, ug, v, v_gain.reshape(1, w), wmix, bmix)
    return res if emit_v else res[0]


def _rope_tables(pos, d, signed):
    inv = ROPE_THETA ** (-jnp.arange(0, d, 2, dtype=F32) / d)
    ang = pos.astype(F32)[:, None] * inv[None, :]
    cos, sin = jnp.cos(ang), jnp.sin(ang)
    reps = LANES // (d // 2)
    if signed:
        return (jnp.tile(cos, (1, reps)),
                jnp.tile(jnp.concatenate([-sin, sin], axis=1), (1, reps // 2)))
    return jnp.tile(cos, (1, reps)), jnp.tile(sin, (1, reps))


def _diff_attn_layer(h, pos, nw, w_in, w_out, q_gain, k_gain, lamp, sub_gain, lam_init,
                     layers, layer, kbuf, vbuf, cache, tag):
    w = w_out.shape[0]
    cos, sin = _rope_tables(pos, A_HD, signed=True)
    wq, wk, wv, wg = (w_in[:, i * w:(i + 1) * w].astype(BF16) for i in range(4))
    (q,), (k, kb), (v, vb), (g,) = _proj(h, nw, [
        _seg(wq, (BF16,), "qk", (A_HD ** -0.5) * LOG2E, q_gain),
        _seg(wk, (F32, BF16), "qk", gain=k_gain, stack=(layers, layer, kbuf)),
        _seg(wv, (F32, BF16), stack=(layers, layer, vbuf)),
        _seg(wg, (BF16,))], cos=cos, sin=sin, name=f"{tag}_proj")
    if cache is None:
        o = _attn_prompt(q, kb, vb, g, lamp, sub_gain, lam_init, name=f"{tag}_attn")
    else:
        kc, vc = cache
        o = _attn_sample(q, kb, vb, g, kc, vc, layer, lamp, sub_gain, lam_init, name=f"{tag}_attn")
    return _outproj(o, w_out.astype(BF16), h, name=f"{tag}_out"), k, v


def _retention_layer(h, pos, nw, w_in, w_out, s0, lc, tag):
    heads = s0.shape[1]
    qk_w, v_w = heads * R_DK, heads * R_DV
    cos, sin = _rope_tables(pos, R_DK, signed=False)
    wq = w_in[:, :qk_w].astype(BF16)
    wk = w_in[:, qk_w:2 * qk_w].astype(BF16)
    assert w_in.shape[1] == 2 * qk_w + 2 * v_w
    wvg = w_in[:, 2 * qk_w:].astype(BF16)
    (q,), (k,), (vg,) = _proj(h, nw, [
        _seg(wq, (BF16,), "rope256"),
        _seg(wk, (BF16,), "rope256", R_DK ** -0.5),
        _seg(wvg, (BF16,))], cos=cos, sin=sin, name=f"{tag}_proj")
    o, s_new = _retention(q, k, vg, s0, lc, name=f"{tag}_ret")
    return _outproj(o, w_out.astype(BF16), h, name=f"{tag}_out"), s_new


def _cmlp_layer(h, nw, w_in, w_out, v_gain, wmix, bmix, emit_v, tag):
    w = w_out.shape[0]
    wu, wv, wg = (w_in[:, i * w:(i + 1) * w].astype(BF16) for i in range(3))
    (ug,), (v,) = _proj(h, nw, [_seg(jnp.concatenate([wu, wg], axis=1), (BF16,)), _seg(wv, (F32,))],
                        name=f"{tag}_proj")
    res = _cmlp(ug, v, v_gain, wmix, bmix, emit_v=emit_v, name=f"{tag}_mix")
    a, vn = res if emit_v else (res, None)
    return _outproj(a, w_out.astype(BF16), h, name=f"{tag}_out"), vn


def _mix_tables(w_s, b_s, chunk_len):
    groups = w_s.shape[0]
    wl = jnp.tril(w_s[:, :chunk_len, :chunk_len])
    reps = M_CHUNK // chunk_len
    eye = jnp.eye(reps, dtype=F32)
    wt = jnp.einsum("ab,gij->gaibj", eye, wl).reshape(groups, M_CHUNK, M_CHUNK)
    bt = jnp.tile(b_s[:, :chunk_len], (1, reps))
    return wt.astype(BF16), jnp.broadcast_to(bt[:, :, None], (groups, M_CHUNK, LANES))


def kernel(x_prompt, x_sample, cache_k_attn, cache_v_attn, state_ret, norm_w, a_w_in, a_w_out, a_q_gain, a_k_gain, a_lam_q1, a_lam_k1, a_lam_q2, a_lam_k2, a_sub_gain, r_w_in, r_w_out, c_w_in, c_w_out, c_v_gain, c_w_s, c_b_s):
    batch, s_len, d = x_prompt.shape
    dec_b, dec_len, _ = x_sample.shape
    past = cache_k_attn.shape[2]
    depth = norm_w.shape[0]
    assert batch == 1 and M_CHUNK % dec_len == 0 and s_len % M_CHUNK == 0

    hp = x_prompt.reshape(s_len, d)
    hs = x_sample.reshape(dec_b * dec_len, d)
    pos_p = jnp.arange(s_len, dtype=jnp.int32)
    pos_s = jnp.tile(past + jnp.arange(dec_len, dtype=jnp.int32), dec_b)

    n_a = a_w_in.shape[0]
    aw = a_w_out.shape[1]
    cache = (cache_k_attn.astype(BF16).reshape(n_a, dec_b, past, aw), cache_v_attn.reshape(n_a, dec_b, past, aw))
    kp = vp = kn = vn = None
    sp_l, ss_l, vm_l = [], [], []
    for i in range(depth):
        kind, j = i % N_MIXERS, i // N_MIXERS
        if kind == 0:
            lam_init = 0.8 - 0.6 * math.exp(-0.3 * i)
            lamp = jnp.stack([a_lam_q1[j], a_lam_k1[j], a_lam_q2[j], a_lam_k2[j]])
            args = (norm_w[i], a_w_in[j], a_w_out[j], a_q_gain[j], a_k_gain[j], lamp, a_sub_gain[j], lam_init)
            hp, kp, vp = _diff_attn_layer(hp, pos_p, *args, n_a, j, kp, vp, None, f"l{i}p")
            hs, kn, vn = _diff_attn_layer(hs, pos_s, *args, n_a, j, kn, vn, cache, f"l{i}s")
        elif kind == 1:
            heads = state_ret.shape[2]
            s0 = jnp.zeros((batch, heads, R_DK, R_DV), F32)
            hp, st_p = _retention_layer(hp, pos_p, norm_w[i], r_w_in[j], r_w_out[j], s0,
                                        _tile(s_len, 256), f"l{i}p")
            hs, st_s = _retention_layer(hs, pos_s, norm_w[i], r_w_in[j], r_w_out[j],
                                        state_ret[j].astype(F32), dec_len, f"l{i}s")
            sp_l.append(st_p)
            ss_l.append(st_s)
        else:
            wp, bp = _mix_tables(c_w_s[j], c_b_s[j], M_CHUNK)
            ws, bs = _mix_tables(c_w_s[j], c_b_s[j], dec_len)
            hp, _ = _cmlp_layer(hp, norm_w[i], c_w_in[j], c_w_out[j], c_v_gain[j], wp, bp, False, f"l{i}p")
            hs, v_s = _cmlp_layer(hs, norm_w[i], c_w_in[j], c_w_out[j], c_v_gain[j], ws, bs, True, f"l{i}s")
            vm_l.append(v_s.reshape(dec_b, dec_len, -1))

    return (hp.reshape(batch, s_len, d), hs.reshape(dec_b, dec_len, d),
            kp.reshape(n_a, batch, s_len, aw // A_HD, A_HD), vp.reshape(n_a, batch, s_len, aw // A_VD, A_VD),
            kn.reshape(n_a, dec_b, dec_len, aw // A_HD, A_HD), vn.reshape(n_a, dec_b, dec_len, aw // A_VD, A_VD),
            jnp.stack(sp_l), jnp.stack(ss_l), jnp.stack(vm_l))
```

```python
import functools
import math

import jax
import jax.numpy as jnp
from jax import lax
from jax.experimental import pallas as pl
from jax.experimental.pallas import tpu as pltpu

F32 = jnp.float32
BF16 = jnp.bfloat16

EPS = 1e-6
CHUNK = 64
ROPE_THETA = 10000.0
N_MIXERS = 3
A_HD = 64
A_VD = 2 * A_HD
R_DK = 256
R_DV = 2 * R_DK
M_GROUPS = 8
M_CHUNK = 128
LOG2E = 1.4426950408889634

LANES = 128
MXU_COLS = 256
V7X_VMEM_BYTES = 64 * 1024 * 1024
VMEM_LIMIT = V7X_VMEM_BYTES * 7 // 8
NEG_BIG = -1e30


def _tile(n, pref):
    if n <= pref:
        return n
    t = pref
    while t >= 8:
        if n % t == 0:
            return t
        t -= 8
    return n


def _params(sem):
    return pltpu.CompilerParams(dimension_semantics=sem, vmem_limit_bytes=VMEM_LIMIT)


def _nbytes(*arrays):
    return sum(a.size * jnp.dtype(a.dtype).itemsize for a in arrays)


def _cost(flops, nbytes, transcendentals=0):
    return pl.CostEstimate(flops=int(flops), transcendentals=int(transcendentals), bytes_accessed=int(nbytes))


def _gelu(x):
    return 0.5 * x * (1.0 + jnp.tanh(0.7978845608028654 * (x + 0.044715 * (x * x * x))))


def _sigmoid(x):
    return 1.0 / (1.0 + jnp.exp(-x))


def _proj_kernel(*refs, plan):
    xn_ref = refs[-1]
    it = iter(refs)
    x_ref, nw_ref = next(it), next(it)
    epis = [p[0] for p in plan]
    cos_ref, sin_ref = (next(it), next(it)) if any(e != "none" for e in epis) else (None, None)
    gsum_ref = next(it) if "qk" in epis else None
    w_refs, gain_refs = [], []
    for e in epis:
        w_refs.append(next(it))
        gain_refs.append(next(it) if e == "qk" else None)
    outs = list(refs[len(refs) - 1 - sum(p[2] for p in plan):-1])
    o_refs = []
    for p in plan:
        o_refs.append(outs[:p[2]])
        outs = outs[p[2]:]

    def store(seg, cols, val):
        scale = plan[seg][1]
        if scale != 1.0:
            val = val * scale
        for o_ref in o_refs[seg]:
            o_ref[:, cols] = val.astype(o_ref.dtype)

    @pl.when(pl.program_id(1) == 0)
    def _():
        x = x_ref[...]
        ms = jnp.mean(x * x, axis=-1, keepdims=True)
        xn_ref[...] = (x * lax.rsqrt(ms + EPS) * nw_ref[...]).astype(BF16)

    qk = [s for s, e in enumerate(epis) if e == "qk"]
    rest = [s for s, e in enumerate(epis) if e == "rope256"] + [s for s, e in enumerate(epis) if e == "none"]
    slabs = {s: [slice(c * MXU_COLS, (c + 1) * MXU_COLS) for c in range(w_refs[s].shape[1] // MXU_COLS)]
             for s in qk}
    zq = {(s, c): jnp.dot(xn_ref[...], w_refs[s][:, sl], preferred_element_type=F32)
          for s in qk for c, sl in enumerate(slabs[s])}
    sq = {key: jnp.dot((z * z).astype(BF16), gsum_ref[...], preferred_element_type=F32) for key, z in zq.items()}
    zr = {s: jnp.dot(xn_ref[...], w_refs[s][...], preferred_element_type=F32) for s in rest}

    if qk:
        cos = jnp.concatenate([cos_ref[...]] * 2, axis=1)
        sin = jnp.concatenate([sin_ref[...]] * 2, axis=1)
        lane = lax.broadcasted_iota(jnp.int32, (xn_ref.shape[0], MXU_COLS), 1)
        first_half = (lane % A_HD) < (A_HD // 2)
        for (s, c), z in zq.items():
            gain = jnp.concatenate([gain_refs[s][...]] * 2, axis=1)
            zn = z * lax.rsqrt(sq[(s, c)] * (1.0 / A_HD) + EPS) * gain
            partner = jnp.where(first_half,
                                pltpu.roll(zn, MXU_COLS - A_HD // 2, 1),
                                pltpu.roll(zn, A_HD // 2, 1))
            store(s, slabs[s][c], zn * cos + partner * sin)
    for s in rest:
        z = zr[s]
        if epis[s] == "rope256":
            cos = cos_ref[...]
            sin = sin_ref[...]
            for c in range(z.shape[1] // R_DK):
                lo, hi = slice(c * R_DK, c * R_DK + LANES), slice(c * R_DK + LANES, (c + 1) * R_DK)
                x1, x2 = z[:, lo], z[:, hi]
                store(s, lo, x1 * cos - x2 * sin)
                store(s, hi, x2 * cos + x1 * sin)
        else:
            store(s, slice(None), z)


def _seg(w, out_dtypes, epi="none", out_scale=1.0, gain=None, stack=None):
    return dict(w=w, out_dtypes=out_dtypes, epi=epi, out_scale=out_scale, gain=gain, stack=stack)


PROJ_VMEM_BUDGET = V7X_VMEM_BYTES * 3 // 4


def _proj(x, nw, segs, *, cos=None, sin=None, name):
    m, d = x.shape
    tm = _tile(m, 1024)
    epis = [s["epi"] for s in segs]
    assert not ("qk" in epis and "rope256" in epis)
    unit = {"qk": MXU_COLS, "rope256": R_DK, "none": LANES}

    def vmem_bytes(nj):
        total = 2 * tm * d * 4 + tm * d * 2
        for s in segs:
            tn = s["w"].shape[1] // nj
            total += 2 * d * tn * 2 + tm * tn * 4
            total += sum(2 * tm * tn * jnp.dtype(dt).itemsize for dt in s["out_dtypes"])
        return total

    nj = next(c for c in (1, 2, 4, 8, 16, 32)
              if all(s["w"].shape[1] % (c * unit[s["epi"]]) == 0 for s in segs) and vmem_bytes(c) <= PROJ_VMEM_BUDGET)
    in_specs = [pl.BlockSpec((tm, d), lambda i, j: (i, 0)),
                pl.BlockSpec((1, d), lambda i, j: (0, 0))]
    args = [x, nw.reshape(1, d)]
    if any(e != "none" for e in epis):
        in_specs += [pl.BlockSpec((tm, LANES), lambda i, j: (i, 0))] * 2
        args += [cos, sin]
    if "qk" in epis:
        gidx = jnp.arange(MXU_COLS) // A_HD
        in_specs.append(pl.BlockSpec((MXU_COLS, MXU_COLS), lambda i, j: (0, 0)))
        args.append((gidx[:, None] == gidx[None, :]).astype(BF16))
    out_shape, out_specs, bufs = [], [], []
    for s in segs:
        n = s["w"].shape[1]
        tn = n // nj
        in_specs.append(pl.BlockSpec((d, tn), lambda i, j: (0, j)))
        args.append(s["w"])
        if s["epi"] == "qk":
            in_specs.append(pl.BlockSpec((1, LANES), lambda i, j: (0, 0)))
            args.append(jnp.tile(s["gain"].reshape(1, A_HD), (1, LANES // A_HD)))
        for o, dt in enumerate(s["out_dtypes"]):
            if o == 0 and s["stack"] is not None:
                layers, layer, buf = s["stack"]
                out_specs.append(pl.BlockSpec((None, tm, tn), lambda i, j, layer=layer: (layer, i, j)))
                out_shape.append(jax.ShapeDtypeStruct((layers, m, n), dt))
                if buf is not None:
                    bufs.append((len(out_shape) - 1, buf))
            else:
                out_specs.append(pl.BlockSpec((tm, tn), lambda i, j: (i, j)))
                out_shape.append(jax.ShapeDtypeStruct((m, n), dt))
    aliases = {}
    for out_idx, buf in bufs:
        aliases[len(args)] = out_idx
        in_specs.append(pl.BlockSpec(memory_space=pl.ANY))
        args.append(buf)
    res = list(pl.pallas_call(
        functools.partial(_proj_kernel, plan=tuple((s["epi"], s["out_scale"], len(s["out_dtypes"])) for s in segs)),
        out_shape=tuple(out_shape),
        grid=(m // tm, nj),
        in_specs=in_specs,
        out_specs=tuple(out_specs),
        scratch_shapes=[pltpu.VMEM((tm, d), BF16)],
        input_output_aliases=aliases,
        compiler_params=_params(("parallel", "arbitrary")),
        cost_estimate=_cost(2 * m * d * sum(s["w"].shape[1] for s in segs),
                            _nbytes(x) + (m // tm) * _nbytes(*(s["w"] for s in segs))
                            + sum(m * s["w"].shape[1] * jnp.dtype(dt).itemsize for s in segs for dt in s["out_dtypes"])),
        name=name,
    )(*args))
    out = []
    for s in segs:
        out.append(res[:len(s["out_dtypes"])])
        res = res[len(s["out_dtypes"]):]
    return out


def _outproj_kernel(a_ref, w_ref, h_ref, o_ref):
    o_ref[...] = h_ref[...] + jnp.dot(a_ref[...], w_ref[...], preferred_element_type=F32)


def _outproj(a, w, h, *, name):
    m, k = a.shape
    n = w.shape[1]
    tm = _tile(m, 1024)
    tn = _tile(n, 1024)
    return pl.pallas_call(
        _outproj_kernel,
        out_shape=jax.ShapeDtypeStruct((m, n), F32),
        grid=(m // tm, n // tn),
        in_specs=[pl.BlockSpec((tm, k), lambda i, j: (i, 0)),
                  pl.BlockSpec((k, tn), lambda i, j: (0, j)),
                  pl.BlockSpec((tm, tn), lambda i, j: (i, j))],
        out_specs=pl.BlockSpec((tm, tn), lambda i, j: (i, j)),
        compiler_params=_params(("parallel", "parallel")),
        cost_estimate=_cost(2 * m * k * n, _nbytes(a) + (m // tm) * _nbytes(w) + 2 * _nbytes(h)),
        name=name,
    )(a, w, h)


def _lambda(lamp_ref, lam_init):
    lp = lamp_ref[...]
    s1 = jnp.sum(lp[0:1, :] * lp[1:2, :], axis=-1, keepdims=True)
    s2 = jnp.sum(lp[2:3, :] * lp[3:4, :], axis=-1, keepdims=True)
    return jnp.exp(s1) - jnp.exp(s2) + lam_init


BF16_SUBLANES = 16
V_AUG = A_VD + BF16_SUBLANES


LOOKAHEAD = 5


def _attn_prompt_kernel(q_ref, qn_ref, k_ref, v_ref, g_ref, lamp_ref, sg_ref, o_ref,
                        qpad_ref, vt_ref, m_ref, acc_ref, s_ref, mx_ref, *, tq, tk, lam_init):
    i = pl.program_id(1)
    slot = i % 2

    sw = s_ref.shape[2]
    n_strips = 2 * tq // sw
    la = min(LOOKAHEAD, n_strips - 1)
    order = sorted(range(n_strips), key=lambda c: -((c * sw) % tq))

    def load_queries(src_ref, dst):
        qt = src_ref[...].astype(F32).T
        row = lax.broadcasted_iota(jnp.int32, qt.shape, 0)
        qpad_ref[dst, :, :tq] = jnp.where(row < A_HD, qt, 0.0).astype(BF16)
        qpad_ref[dst, :, tq:] = jnp.where(row >= A_HD, qt, 0.0).astype(BF16)

    def scores(j, c, rows=tk, src=slot):
        kj = k_ref[pl.ds(pl.multiple_of(j * tk, tk), rows), :]
        s = jnp.dot(kj, qpad_ref[src, :, c * sw:(c + 1) * sw], preferred_element_type=F32)
        s_ref[c, :rows] = s
        mx_ref[c] = jnp.max(s, axis=0, keepdims=True)

    @pl.when(i == 0)
    def _():
        aug = lax.broadcasted_iota(jnp.int32, (V_AUG - A_VD, tk), 0)
        ones_row = jnp.where(aug == 0, 1.0, 0.0).astype(BF16)

        def fill(t, carry):
            vj = v_ref[pl.ds(pl.multiple_of(t * tk, tk), tk), :].astype(F32)
            vt_ref[t, :A_VD, :] = vj.T.astype(BF16)
            vt_ref[t, A_VD:, :] = ones_row
            return carry

        lax.fori_loop(0, vt_ref.shape[0], fill, 0)
        load_queries(q_ref, 0)
        for c in order[:la]:
            scores(0, c, src=0)

    m_ref[...] = jnp.full(m_ref.shape, NEG_BIG, F32)
    acc_ref[...] = jnp.zeros(acc_ref.shape, F32)

    def update(j, c, mask, rows=tk):
        cs = slice(c * sw, (c + 1) * sw)
        s = s_ref[c, :rows]
        if mask is None:
            mx = mx_ref[c]
        else:
            s = jnp.where(mask[:rows], s, NEG_BIG)
            mx = jnp.max(s, axis=0, keepdims=True)
        m_old = m_ref[:, cs]
        m_new = jnp.maximum(m_old, mx)
        alpha = jnp.exp2(m_old - m_new)
        p = jnp.exp2(s - m_new).astype(BF16)
        pv = jnp.dot(vt_ref[j][:, :rows], p, preferred_element_type=F32)
        acc_ref[:, cs] = alpha * acc_ref[:, cs] + pv
        m_ref[:, cs] = m_new

    n_sub = tq // tk
    n_full = i * n_sub

    def full_tiles(j0, count):
        for j in range(j0, j0 + count) if isinstance(j0, int) else [j0 + t for t in range(count)]:
            for n, c in enumerate(order):
                update(j, c, None)
                if n + la < n_strips:
                    scores(j, order[n + la])
                else:
                    scores(j + 1, order[n + la - n_strips])

    odd = i % 2

    @pl.when(odd == 1)
    def _():
        full_tiles(0, n_sub)

    def body(jj, carry):
        full_tiles(odd * n_sub + jj * (2 * n_sub), 2 * n_sub)
        return carry

    lax.fori_loop(0, i // 2, body, 0)

    r = lax.broadcasted_iota(jnp.int32, (tk, sw), 0)
    cc = lax.broadcasted_iota(jnp.int32, (tk, sw), 1)
    units = []
    for d in range(n_sub):
        for c in order:
            q_lo, k_lo = (c * sw) % tq, d * tk
            if q_lo + sw <= k_lo:
                continue
            full = q_lo >= k_lo + tk
            rows = min(tk, q_lo + sw - k_lo)
            units.append((d, c, None if full else (k_lo + r) // CHUNK <= (q_lo + cc) // CHUNK, rows))
    assert [u[:2] + u[3:] for u in units[:la]] == [(0, c, tk) for c in order[:la]]
    load_queries(qn_ref, 1 - slot)
    for n, (d, c, mask, rows) in enumerate(units):
        update(n_full + d, c, mask, rows)
        if n + la < len(units):
            nd, nc, _, nrows = units[n + la]
            assert all(u[1] != nc for u in units[n + 1:n + la])
            scores(n_full + nd, nc, nrows)
    for c in order[:la]:
        scores(0, c, src=1 - slot)

    lam = _lambda(lamp_ref, lam_init)
    acc = acc_ref[...]
    inv_l = 1.0 / acc[A_VD:A_VD + 1, :]
    ot = acc[:A_VD, :tq] * inv_l[:, :tq] - lam * (acc[:A_VD, tq:] * inv_l[:, tq:])
    ms = jnp.mean(ot * ot, axis=0, keepdims=True)
    ot = ot * lax.rsqrt(ms + EPS) * (sg_ref[...] * (1.0 - lam_init))
    g = g_ref[...].astype(F32)
    o_ref[...] = (ot.T * (g * _sigmoid(g))).astype(o_ref.dtype)


def _attn_prompt(q, k, v, g, lamp, sub_gain, lam_init, *, name):
    s, w = q.shape
    heads = w // A_VD
    tq = _tile(s, 1024)
    tk = _tile(tq, 512)
    sw = min(MXU_COLS, tq)
    nq = s // tq
    return pl.pallas_call(
        functools.partial(_attn_prompt_kernel, tq=tq, tk=tk, lam_init=lam_init),
        out_shape=jax.ShapeDtypeStruct((s, w), BF16),
        grid=(heads, nq),
        in_specs=[pl.BlockSpec((tq, A_VD), lambda h, i: (i, h)),
                  pl.BlockSpec((tq, A_VD), lambda h, i: (jnp.minimum(i + 1, nq - 1), h)),
                  pl.BlockSpec((s, A_VD), lambda h, i: (0, h)),
                  pl.BlockSpec((s, A_VD), lambda h, i: (0, h)),
                  pl.BlockSpec((tq, A_VD), lambda h, i: (i, h)),
                  pl.BlockSpec((4, A_HD), lambda h, i: (0, 0)),
                  pl.BlockSpec((A_VD, 1), lambda h, i: (0, 0))],
        out_specs=pl.BlockSpec((tq, A_VD), lambda h, i: (i, h)),
        scratch_shapes=[pltpu.VMEM((2, A_VD, 2 * tq), BF16),
                        pltpu.VMEM((s // tk, V_AUG, tk), BF16),
                        pltpu.VMEM((1, 2 * tq), F32),
                        pltpu.VMEM((V_AUG, 2 * tq), F32),
                        pltpu.VMEM((2 * tq // sw, tk, sw), F32),
                        pltpu.VMEM((2 * tq // sw, 1, sw), F32)],
        compiler_params=_params(("parallel", "arbitrary")),
        cost_estimate=_cost(heads * (s * s // 2) * 2 * 2 * (A_VD + V_AUG), _nbytes(q, k, v, g, q),
                            heads * (s * s // 2) * 2),
        name=name,
    )(q, q, k, v, g, lamp, sub_gain.reshape(A_VD, 1))


SAMPLE_HEADS_PER_STEP = 4


def _attn_sample_kernel(q_ref, kn_ref, vn_ref, g_ref, kc_ref, vc_ref, lamp_ref, sg_ref, o_ref, *, lam_init, hps):
    lam = _lambda(lamp_ref, lam_init)
    nt = (((1,), (1,)), ((), ()))
    lane = lax.broadcasted_iota(jnp.int32, (q_ref.shape[0], A_VD), 1)
    heads = [slice(h * A_VD, (h + 1) * A_VD) for h in range(hps)]
    scores = []
    for hs in heads:
        q = q_ref[:, hs]
        kc = kc_ref[0, :, hs].astype(BF16)
        kn = kn_ref[:, hs].astype(BF16)
        for half in range(2):
            qh = jnp.where((lane >= A_HD) == bool(half), q, jnp.zeros_like(q))
            scores.append((lax.dot_general(qh, kc, nt, preferred_element_type=F32),
                           lax.dot_general(qh, kn, nt, preferred_element_type=F32)))
    probs = []
    for sc, sn in scores:
        m = jnp.maximum(jnp.max(sc, axis=-1, keepdims=True), jnp.max(sn, axis=-1, keepdims=True))
        pc = jnp.exp2(sc - m)
        pn = jnp.exp2(sn - m)
        inv = 1.0 / (jnp.sum(pc, axis=-1, keepdims=True) + jnp.sum(pn, axis=-1, keepdims=True))
        probs.append((pc * inv, pn * inv))
    outs = []
    for h, hs in enumerate(heads):
        ac = (probs[2 * h][0] - lam * probs[2 * h + 1][0]).astype(BF16)
        an = (probs[2 * h][1] - lam * probs[2 * h + 1][1]).astype(BF16)
        outs.append(jnp.dot(ac, vc_ref[0, :, hs].astype(BF16), preferred_element_type=F32)
                    + jnp.dot(an, vn_ref[:, hs].astype(BF16), preferred_element_type=F32))
    for hs, o in zip(heads, outs):
        ms = jnp.mean(o * o, axis=-1, keepdims=True)
        g = g_ref[:, hs].astype(F32)
        o = o * lax.rsqrt(ms + EPS) * (sg_ref[...] * (1.0 - lam_init))
        o_ref[:, hs] = (o * (g * _sigmoid(g))).astype(o_ref.dtype)


def _attn_sample(q, k, v, g, kc, vc, layer, lamp, sub_gain, lam_init, *, name):
    _, bsz, past, w = kc.shape
    heads = w // A_VD
    ln = q.shape[0] // bsz
    assert past % CHUNK == 0 and ln <= CHUNK
    hps = SAMPLE_HEADS_PER_STEP
    assert heads % hps == 0
    row = lambda b, h: (b, h)
    cache = lambda b, h: (layer, b, 0, h)
    return pl.pallas_call(
        functools.partial(_attn_sample_kernel, lam_init=lam_init, hps=hps),
        out_shape=jax.ShapeDtypeStruct(q.shape, BF16),
        grid=(bsz, heads // hps),
        in_specs=[pl.BlockSpec((ln, hps * A_VD), row),
                  pl.BlockSpec((ln, hps * A_VD), row),
                  pl.BlockSpec((ln, hps * A_VD), row),
                  pl.BlockSpec((ln, hps * A_VD), row),
                  pl.BlockSpec((None, 1, past, hps * A_VD), cache),
                  pl.BlockSpec((None, 1, past, hps * A_VD), cache),
                  pl.BlockSpec((4, A_HD), lambda b, h: (0, 0)),
                  pl.BlockSpec((1, A_VD), lambda b, h: (0, 0))],
        out_specs=pl.BlockSpec((ln, hps * A_VD), row),
        compiler_params=_params(("parallel", "parallel")),
        cost_estimate=_cost(bsz * heads * ln * (past + ln) * 2 * 2 * 2 * A_VD,
                            _nbytes(q, k, v, g, q) + _nbytes(kc, vc) // kc.shape[0],
                            bsz * heads * ln * (past + ln) * 2),
        name=name,
    )(q, k, v, g, kc, vc, lamp, sub_gain.reshape(1, A_VD))


RET_HEADS_PER_STEP = 8


def _retention_kernel(q_ref, k_ref, v_ref, g_ref, s0_ref, dec_ref, cd_ref, kd_ref, gl_ref,
                      o_ref, sout_ref, st_ref, *, lc, hps):
    c = pl.program_id(2)

    @pl.when(c == 0)
    def _():
        st_ref[...] = s0_ref[0]

    hd = range(hps)
    qs = [q_ref[:, h * R_DK:(h + 1) * R_DK] for h in hd]
    ks = [k_ref[:, h * R_DK:(h + 1) * R_DK] for h in hd]
    vs = [v_ref[:, h * R_DV:(h + 1) * R_DV] for h in hd]
    sts = [st_ref[h] for h in hd]
    nt = (((1,), (1,)), ((), ()))
    inner = [lax.dot_general(qs[h], ks[h], nt, preferred_element_type=F32) for h in hd]
    cross = [jnp.dot(qs[h], sts[h].astype(BF16), preferred_element_type=F32) for h in hd]
    kdec = []
    for h in hd:
        kd = ks[h].astype(F32) * jnp.concatenate([kd_ref[h]] * (R_DK // LANES), axis=1)
        if lc < LANES:
            kd = jnp.concatenate([kd, jnp.zeros((LANES - lc, R_DK), F32)], axis=0)
        kdec.append(kd.T.astype(BF16))
    inner = [(inner[h] * dec_ref[h]).astype(BF16) for h in hd]
    o = [jnp.dot(inner[h], vs[h], preferred_element_type=F32) for h in hd]
    upd = []
    for h in hd:
        vv = vs[h]
        if lc < LANES:
            vv = jnp.concatenate([vv, jnp.zeros((LANES - lc, R_DV), BF16)], axis=0)
        upd.append(jnp.dot(kdec[h], vv, preferred_element_type=F32))
    for h in hd:
        cd = jnp.concatenate([cd_ref[h]] * (R_DV // LANES), axis=1)
        oh = o[h] + cross[h] * cd
        ms = jnp.mean(oh * oh, axis=-1, keepdims=True)
        g = g_ref[:, h * R_DV:(h + 1) * R_DV].astype(F32)
        o_ref[:, h * R_DV:(h + 1) * R_DV] = (oh * lax.rsqrt(ms + EPS) * (g * _sigmoid(g))).astype(o_ref.dtype)
    st_new = []
    for h in hd:
        gl = jnp.concatenate([gl_ref[h]] * (R_DV // LANES), axis=1)
        st_new.append(sts[h] * gl + upd[h])
        st_ref[h] = st_new[h]

    @pl.when(c == pl.num_programs(2) - 1)
    def _():
        for h in hd:
            sout_ref[0, h] = st_new[h]


def _retention(q, k, vg, s0, lc, *, name):
    bsz, heads = s0.shape[:2]
    t = q.shape[0] // bsz
    nc = t // lc
    lg = jnp.log1p(-(2.0 ** (-5.0 - jnp.arange(heads, dtype=F32))))
    idx = jnp.arange(lc, dtype=F32)
    diff = idx[:, None] - idx[None, :]
    dec = jnp.where(diff >= 0, jnp.exp(lg[:, None, None] * jnp.maximum(diff, 0.0)), 0.0)
    cd = jnp.broadcast_to(jnp.exp(lg[:, None] * (idx[None, :] + 1.0))[:, :, None], (heads, lc, LANES))
    kd = jnp.broadcast_to(jnp.exp(lg[:, None] * (lc - 1.0 - idx[None, :]))[:, :, None], (heads, lc, LANES))
    gl = jnp.broadcast_to(jnp.exp(lg * lc)[:, None, None], (heads, 1, LANES))
    hps = RET_HEADS_PER_STEP
    assert heads % hps == 0
    rows = lambda b, h, c: (b * nc + c, h)
    gate = lambda b, h, c: (b * nc + c, heads // hps + h)
    tab = lambda b, h, c: (h, 0, 0)
    state = lambda b, h, c: (b, h, 0, 0)
    return pl.pallas_call(
        functools.partial(_retention_kernel, lc=lc, hps=hps),
        out_shape=(jax.ShapeDtypeStruct((vg.shape[0], heads * R_DV), BF16),
                   jax.ShapeDtypeStruct(s0.shape, F32)),
        grid=(bsz, heads // hps, nc),
        in_specs=[pl.BlockSpec((lc, hps * R_DK), rows),
                  pl.BlockSpec((lc, hps * R_DK), rows),
                  pl.BlockSpec((lc, hps * R_DV), rows),
                  pl.BlockSpec((lc, hps * R_DV), gate),
                  pl.BlockSpec((1, hps, R_DK, R_DV), state),
                  pl.BlockSpec((hps, lc, lc), tab),
                  pl.BlockSpec((hps, lc, LANES), tab),
                  pl.BlockSpec((hps, lc, LANES), tab),
                  pl.BlockSpec((hps, 1, LANES), tab)],
        out_specs=(pl.BlockSpec((lc, hps * R_DV), rows),
                   pl.BlockSpec((1, hps, R_DK, R_DV), state)),
        scratch_shapes=[pltpu.VMEM((hps, R_DK, R_DV), F32)],
        compiler_params=_params(("parallel", "parallel", "arbitrary")),
        cost_estimate=_cost(2 * q.shape[0] * heads * (lc * (R_DK + R_DV) + 2 * R_DK * R_DV),
                            _nbytes(q, k, vg, s0, s0) + q.shape[0] * heads * R_DV * 2,
                            q.shape[0] * heads * R_DV),
        name=name,
    )(q, k, vg, vg, s0, dec, cd, kd, gl)


def _cmlp_kernel(u_ref, g_ref, v_ref, vg_ref, w_ref, b_ref, *out_refs, emit_v):
    if emit_v:
        a_ref, vn_ref = out_refs
    else:
        (a_ref,) = out_refs
    va = _gelu(v_ref[...])
    ms = jnp.mean(va * va, axis=-1, keepdims=True)
    vn = va * lax.rsqrt(ms + EPS) * vg_ref[...]
    if emit_v:
        vn_ref[...] = vn
    vb = vn.astype(BF16)
    gd = vn.shape[1] // M_GROUPS
    for grp in range(M_GROUPS):
        sl = slice(grp * gd, (grp + 1) * gd)
        bias = jnp.concatenate([b_ref[grp]] * (gd // LANES), axis=1)
        mix = jnp.dot(w_ref[grp], vb[:, sl], preferred_element_type=F32) + bias
        g = g_ref[:, sl].astype(F32)
        a_ref[:, sl] = (_gelu(u_ref[:, sl].astype(F32)) * mix * (g * _sigmoid(g))).astype(a_ref.dtype)


def _cmlp(ug, v, v_gain, wmix, bmix, *, emit_v, name):
    m, w = v.shape
    t = wmix.shape[1]
    out_shape = [jax.ShapeDtypeStruct((m, w), BF16)]
    out_specs = [pl.BlockSpec((t, w), lambda i: (i, 0))]
    if emit_v:
        out_shape.append(jax.ShapeDtypeStruct((m, w), F32))
        out_specs.append(pl.BlockSpec((t, w), lambda i: (i, 0)))
    res = pl.pallas_call(
        functools.partial(_cmlp_kernel, emit_v=emit_v),
        out_shape=tuple(out_shape),
        grid=(m // t,),
        in_specs=[pl.BlockSpec((t, w), lambda i: (i, 0)),
                  pl.BlockSpec((t, w), lambda i: (i, 1)),
                  pl.BlockSpec((t, w), lambda i: (i, 0)),
                  pl.BlockSpec((1, w), lambda i: (0, 0)),
                  pl.BlockSpec((M_GROUPS, t, t), lambda i: (0, 0, 0)),
                  pl.BlockSpec((M_GROUPS, t, LANES), lambda i: (0, 0, 0))],
        out_specs=tuple(out_specs),
        compiler_params=_params(("parallel",)),
        cost_estimate=_cost(2 * m * t * w, _nbytes(ug, v) + sum(m * w * jnp.dtype(o.dtype).itemsize for o in out_shape),
                            3 * m * w),
        name=name,
    )(ug, ug, v, v_gain.reshape(1, w), wmix, bmix)
    return res if emit_v else res[0]


def _rope_tables(pos, d, signed):
    inv = ROPE_THETA ** (-jnp.arange(0, d, 2, dtype=F32) / d)
    ang = pos.astype(F32)[:, None] * inv[None, :]
    cos, sin = jnp.cos(ang), jnp.sin(ang)
    reps = LANES // (d // 2)
    if signed:
        return (jnp.tile(cos, (1, reps)),
                jnp.tile(jnp.concatenate([-sin, sin], axis=1), (1, reps // 2)))
    return jnp.tile(cos, (1, reps)), jnp.tile(sin, (1, reps))


def _diff_attn_layer(h, pos, nw, w_in, w_out, q_gain, k_gain, lamp, sub_gain, lam_init,
                     layers, layer, kbuf, vbuf, cache, tag):
    w = w_out.shape[0]
    cos, sin = _rope_tables(pos, A_HD, signed=True)
    wq, wk, wv, wg = (w_in[:, i * w:(i + 1) * w].astype(BF16) for i in range(4))
    (q,), (k, kb), (v, vb), (g,) = _proj(h, nw, [
        _seg(wq, (BF16,), "qk", (A_HD ** -0.5) * LOG2E, q_gain),
        _seg(wk, (F32, BF16), "qk", gain=k_gain, stack=(layers, layer, kbuf)),
        _seg(wv, (F32, BF16), stack=(layers, layer, vbuf)),
        _seg(wg, (BF16,))], cos=cos, sin=sin, name=f"{tag}_proj")
    if cache is None:
        o = _attn_prompt(q, kb, vb, g, lamp, sub_gain, lam_init, name=f"{tag}_attn")
    else:
        kc, vc = cache
        o = _attn_sample(q, kb, vb, g, kc, vc, layer, lamp, sub_gain, lam_init, name=f"{tag}_attn")
    return _outproj(o, w_out.astype(BF16), h, name=f"{tag}_out"), k, v


def _retention_layer(h, pos, nw, w_in, w_out, s0, lc, tag):
    heads = s0.shape[1]
    qk_w, v_w = heads * R_DK, heads * R_DV
    cos, sin = _rope_tables(pos, R_DK, signed=False)
    wq = w_in[:, :qk_w].astype(BF16)
    wk = w_in[:, qk_w:2 * qk_w].astype(BF16)
    assert w_in.shape[1] == 2 * qk_w + 2 * v_w
    wvg = w_in[:, 2 * qk_w:].astype(BF16)
    (q,), (k,), (vg,) = _proj(h, nw, [
        _seg(wq, (BF16,), "rope256"),
        _seg(wk, (BF16,), "rope256", R_DK ** -0.5),
        _seg(wvg, (BF16,))], cos=cos, sin=sin, name=f"{tag}_proj")
    o, s_new = _retention(q, k, vg, s0, lc, name=f"{tag}_ret")
    return _outproj(o, w_out.astype(BF16), h, name=f"{tag}_out"), s_new


def _cmlp_layer(h, nw, w_in, w_out, v_gain, wmix, bmix, emit_v, tag):
    w = w_out.shape[0]
    wu, wv, wg = (w_in[:, i * w:(i + 1) * w].astype(BF16) for i in range(3))
    (ug,), (v,) = _proj(h, nw, [_seg(jnp.concatenate([wu, wg], axis=1), (BF16,)), _seg(wv, (F32,))],
                        name=f"{tag}_proj")
    res = _cmlp(ug, v, v_gain, wmix, bmix, emit_v=emit_v, name=f"{tag}_mix")
    a, vn = res if emit_v else (res, None)
    return _outproj(a, w_out.astype(BF16), h, name=f"{tag}_out"), vn


def _mix_tables(w_s, b_s, chunk_len):
    groups = w_s.shape[0]
    wl = jnp.tril(w_s[:, :chunk_len, :chunk_len])
    reps = M_CHUNK // chunk_len
    eye = jnp.eye(reps, dtype=F32)
    wt = jnp.einsum("ab,gij->gaibj", eye, wl).reshape(groups, M_CHUNK, M_CHUNK)
    bt = jnp.tile(b_s[:, :chunk_len], (1, reps))
    return wt.astype(BF16), jnp.broadcast_to(bt[:, :, None], (groups, M_CHUNK, LANES))


def kernel(x_prompt, x_sample, cache_k_attn, cache_v_attn, state_ret, norm_w, a_w_in, a_w_out, a_q_gain, a_k_gain, a_lam_q1, a_lam_k1, a_lam_q2, a_lam_k2, a_sub_gain, r_w_in, r_w_out, c_w_in, c_w_out, c_v_gain, c_w_s, c_b_s):
    batch, s_len, d = x_prompt.shape
    dec_b, dec_len, _ = x_sample.shape
    past = cache_k_attn.shape[2]
    depth = norm_w.shape[0]
    assert batch == 1 and M_CHUNK % dec_len == 0 and s_len % M_CHUNK == 0

    hp = x_prompt.reshape(s_len, d)
    hs = x_sample.reshape(dec_b * dec_len, d)
    pos_p = jnp.arange(s_len, dtype=jnp.int32)
    pos_s = jnp.tile(past + jnp.arange(dec_len, dtype=jnp.int32), dec_b)

    n_a = a_w_in.shape[0]
    aw = a_w_out.shape[1]
    cache = (cache_k_attn.astype(BF16).reshape(n_a, dec_b, past, aw),
             cache_v_attn.astype(BF16).reshape(n_a, dec_b, past, aw))
    kp = vp = kn = vn = None
    sp_l, ss_l, vm_l = [], [], []
    for i in range(depth):
        kind, j = i % N_MIXERS, i // N_MIXERS
        if kind == 0:
            lam_init = 0.8 - 0.6 * math.exp(-0.3 * i)
            lamp = jnp.stack([a_lam_q1[j], a_lam_k1[j], a_lam_q2[j], a_lam_k2[j]])
            args = (norm_w[i], a_w_in[j], a_w_out[j], a_q_gain[j], a_k_gain[j], lamp, a_sub_gain[j], lam_init)
            hp, kp, vp = _diff_attn_layer(hp, pos_p, *args, n_a, j, kp, vp, None, f"l{i}p")
            hs, kn, vn = _diff_attn_layer(hs, pos_s, *args, n_a, j, kn, vn, cache, f"l{i}s")
        elif kind == 1:
            heads = state_ret.shape[2]
            s0 = jnp.zeros((batch, heads, R_DK, R_DV), F32)
            hp, st_p = _retention_layer(hp, pos_p, norm_w[i], r_w_in[j], r_w_out[j], s0,
                                        _tile(s_len, 256), f"l{i}p")
            hs, st_s = _retention_layer(hs, pos_s, norm_w[i], r_w_in[j], r_w_out[j],
                                        state_ret[j].astype(F32), dec_len, f"l{i}s")
            sp_l.append(st_p)
            ss_l.append(st_s)
        else:
            wp, bp = _mix_tables(c_w_s[j], c_b_s[j], M_CHUNK)
            ws, bs = _mix_tables(c_w_s[j], c_b_s[j], dec_len)
            hp, _ = _cmlp_layer(hp, norm_w[i], c_w_in[j], c_w_out[j], c_v_gain[j], wp, bp, False, f"l{i}p")
            hs, v_s = _cmlp_layer(hs, norm_w[i], c_w_in[j], c_w_out[j], c_v_gain[j], ws, bs, True, f"l{i}s")
            vm_l.append(v_s.reshape(dec_b, dec_len, -1))

    return (hp.reshape(batch, s_len, d), hs.reshape(dec_b, dec_len, d),
            kp.reshape(n_a, batch, s_len, aw // A_HD, A_HD), vp.reshape(n_a, batch, s_len, aw // A_VD, A_VD),
            kn.reshape(n_a, dec_b, dec_len, aw // A_HD, A_HD), vn.reshape(n_a, dec_b, dec_len, aw // A_VD, A_VD),
            jnp.stack(sp_l), jnp.stack(ss_l), jnp.stack(vm_l))
```

```python
import functools
import math

import jax
import jax.numpy as jnp
from jax import lax
from jax.experimental import pallas as pl
from jax.experimental.pallas import tpu as pltpu

F32 = jnp.float32
BF16 = jnp.bfloat16

EPS = 1e-6
CHUNK = 64
ROPE_THETA = 10000.0
N_MIXERS = 3
A_HD = 64
A_VD = 2 * A_HD
R_DK = 256
R_DV = 2 * R_DK
M_GROUPS = 8
M_CHUNK = 128
LOG2E = 1.4426950408889634

LANES = 128
MXU_COLS = 256
V7X_VMEM_BYTES = 64 * 1024 * 1024
VMEM_LIMIT = V7X_VMEM_BYTES * 7 // 8
NEG_BIG = -1e30


def _tile(n, pref):
    if n <= pref:
        return n
    t = pref
    while t >= 8:
        if n % t == 0:
            return t
        t -= 8
    return n


def _params(sem):
    return pltpu.CompilerParams(dimension_semantics=sem, vmem_limit_bytes=VMEM_LIMIT)


def _nbytes(*arrays):
    return sum(a.size * jnp.dtype(a.dtype).itemsize for a in arrays)


def _cost(flops, nbytes, transcendentals=0):
    return pl.CostEstimate(flops=int(flops), transcendentals=int(transcendentals), bytes_accessed=int(nbytes))


def _gelu(x):
    return 0.5 * x * (1.0 + jnp.tanh(0.7978845608028654 * (x + 0.044715 * (x * x * x))))


def _sigmoid(x):
    return 1.0 / (1.0 + jnp.exp(-x))


def _proj_kernel(*refs, plan):
    xn_ref = refs[-1]
    it = iter(refs)
    x_ref, nw_ref = next(it), next(it)
    epis = [p[0] for p in plan]
    cos_ref, sin_ref = (next(it), next(it)) if any(e != "none" for e in epis) else (None, None)
    gsum_ref = next(it) if "qk" in epis else None
    w_refs, gain_refs = [], []
    for e in epis:
        w_refs.append(next(it))
        gain_refs.append(next(it) if e == "qk" else None)
    outs = list(refs[len(refs) - 1 - sum(p[2] for p in plan):-1])
    o_refs = []
    for p in plan:
        o_refs.append(outs[:p[2]])
        outs = outs[p[2]:]

    def store(seg, cols, val):
        scale = plan[seg][1]
        if scale != 1.0:
            val = val * scale
        for o_ref in o_refs[seg]:
            o_ref[:, cols] = val.astype(o_ref.dtype)

    @pl.when(pl.program_id(1) == 0)
    def _():
        x = x_ref[...]
        ms = jnp.mean(x * x, axis=-1, keepdims=True)
        xn_ref[...] = (x * lax.rsqrt(ms + EPS) * nw_ref[...]).astype(BF16)

    qk = [s for s, e in enumerate(epis) if e == "qk"]
    rest = [s for s, e in enumerate(epis) if e == "rope256"] + [s for s, e in enumerate(epis) if e == "none"]
    slabs = {s: [slice(c * MXU_COLS, (c + 1) * MXU_COLS) for c in range(w_refs[s].shape[1] // MXU_COLS)]
             for s in qk}
    zq = {(s, c): jnp.dot(xn_ref[...], w_refs[s][:, sl], preferred_element_type=F32)
          for s in qk for c, sl in enumerate(slabs[s])}
    sq = {key: jnp.dot((z * z).astype(BF16), gsum_ref[...], preferred_element_type=F32) for key, z in zq.items()}
    zr = {s: jnp.dot(xn_ref[...], w_refs[s][...], preferred_element_type=F32) for s in rest}

    if qk:
        cos = jnp.concatenate([cos_ref[...]] * 2, axis=1)
        sin = jnp.concatenate([sin_ref[...]] * 2, axis=1)
        lane = lax.broadcasted_iota(jnp.int32, (xn_ref.shape[0], MXU_COLS), 1)
        first_half = (lane % A_HD) < (A_HD // 2)
        for (s, c), z in zq.items():
            gain = jnp.concatenate([gain_refs[s][...]] * 2, axis=1)
            zn = z * lax.rsqrt(sq[(s, c)] * (1.0 / A_HD) + EPS) * gain
            partner = jnp.where(first_half,
                                pltpu.roll(zn, MXU_COLS - A_HD // 2, 1),
                                pltpu.roll(zn, A_HD // 2, 1))
            store(s, slabs[s][c], zn * cos + partner * sin)
    for s in rest:
        z = zr[s]
        if epis[s] == "rope256":
            cos = cos_ref[...]
            sin = sin_ref[...]
            for c in range(z.shape[1] // R_DK):
                lo, hi = slice(c * R_DK, c * R_DK + LANES), slice(c * R_DK + LANES, (c + 1) * R_DK)
                x1, x2 = z[:, lo], z[:, hi]
                store(s, lo, x1 * cos - x2 * sin)
                store(s, hi, x2 * cos + x1 * sin)
        else:
            store(s, slice(None), z)


def _seg(w, out_dtypes, epi="none", out_scale=1.0, gain=None, stack=None):
    return dict(w=w, out_dtypes=out_dtypes, epi=epi, out_scale=out_scale, gain=gain, stack=stack)


PROJ_VMEM_BUDGET = V7X_VMEM_BYTES * 3 // 4


def _proj(x, nw, segs, *, cos=None, sin=None, name):
    m, d = x.shape
    tm = _tile(m, 1024)
    epis = [s["epi"] for s in segs]
    assert not ("qk" in epis and "rope256" in epis)
    unit = {"qk": MXU_COLS, "rope256": R_DK, "none": LANES}

    def vmem_bytes(nj):
        total = 2 * tm * d * 4 + tm * d * 2
        for s in segs:
            tn = s["w"].shape[1] // nj
            total += 2 * d * tn * 2 + tm * tn * 4
            total += sum(2 * tm * tn * jnp.dtype(dt).itemsize for dt in s["out_dtypes"])
        return total

    nj = next(c for c in (1, 2, 4, 8, 16, 32)
              if all(s["w"].shape[1] % (c * unit[s["epi"]]) == 0 for s in segs) and vmem_bytes(c) <= PROJ_VMEM_BUDGET)
    in_specs = [pl.BlockSpec((tm, d), lambda i, j: (i, 0)),
                pl.BlockSpec((1, d), lambda i, j: (0, 0))]
    args = [x, nw.reshape(1, d)]
    if any(e != "none" for e in epis):
        in_specs += [pl.BlockSpec((tm, LANES), lambda i, j: (i, 0))] * 2
        args += [cos, sin]
    if "qk" in epis:
        gidx = jnp.arange(MXU_COLS) // A_HD
        in_specs.append(pl.BlockSpec((MXU_COLS, MXU_COLS), lambda i, j: (0, 0)))
        args.append((gidx[:, None] == gidx[None, :]).astype(BF16))
    out_shape, out_specs, bufs = [], [], []
    for s in segs:
        n = s["w"].shape[1]
        tn = n // nj
        in_specs.append(pl.BlockSpec((d, tn), lambda i, j: (0, j)))
        args.append(s["w"])
        if s["epi"] == "qk":
            in_specs.append(pl.BlockSpec((1, LANES), lambda i, j: (0, 0)))
            args.append(jnp.tile(s["gain"].reshape(1, A_HD), (1, LANES // A_HD)))
        for o, dt in enumerate(s["out_dtypes"]):
            if o == 0 and s["stack"] is not None:
                layers, layer, buf = s["stack"]
                out_specs.append(pl.BlockSpec((None, tm, tn), lambda i, j, layer=layer: (layer, i, j)))
                out_shape.append(jax.ShapeDtypeStruct((layers, m, n), dt))
                if buf is not None:
                    bufs.append((len(out_shape) - 1, buf))
            else:
                out_specs.append(pl.BlockSpec((tm, tn), lambda i, j: (i, j)))
                out_shape.append(jax.ShapeDtypeStruct((m, n), dt))
    aliases = {}
    for out_idx, buf in bufs:
        aliases[len(args)] = out_idx
        in_specs.append(pl.BlockSpec(memory_space=pl.ANY))
        args.append(buf)
    res = list(pl.pallas_call(
        functools.partial(_proj_kernel, plan=tuple((s["epi"], s["out_scale"], len(s["out_dtypes"])) for s in segs)),
        out_shape=tuple(out_shape),
        grid=(m // tm, nj),
        in_specs=in_specs,
        out_specs=tuple(out_specs),
        scratch_shapes=[pltpu.VMEM((tm, d), BF16)],
        input_output_aliases=aliases,
        compiler_params=_params(("parallel", "arbitrary")),
        cost_estimate=_cost(2 * m * d * sum(s["w"].shape[1] for s in segs),
                            _nbytes(x) + (m // tm) * _nbytes(*(s["w"] for s in segs))
                            + sum(m * s["w"].shape[1] * jnp.dtype(dt).itemsize for s in segs for dt in s["out_dtypes"])),
        name=name,
    )(*args))
    out = []
    for s in segs:
        out.append(res[:len(s["out_dtypes"])])
        res = res[len(s["out_dtypes"]):]
    return out


def _outproj_kernel(a_ref, w_ref, h_ref, o_ref):
    o_ref[...] = h_ref[...] + jnp.dot(a_ref[...], w_ref[...], preferred_element_type=F32)


def _outproj(a, w, h, *, name):
    m, k = a.shape
    n = w.shape[1]
    tm = _tile(m, 1024)
    tn = _tile(n, 1024)
    return pl.pallas_call(
        _outproj_kernel,
        out_shape=jax.ShapeDtypeStruct((m, n), F32),
        grid=(m // tm, n // tn),
        in_specs=[pl.BlockSpec((tm, k), lambda i, j: (i, 0)),
                  pl.BlockSpec((k, tn), lambda i, j: (0, j)),
                  pl.BlockSpec((tm, tn), lambda i, j: (i, j))],
        out_specs=pl.BlockSpec((tm, tn), lambda i, j: (i, j)),
        compiler_params=_params(("parallel", "parallel")),
        cost_estimate=_cost(2 * m * k * n, _nbytes(a) + (m // tm) * _nbytes(w) + 2 * _nbytes(h)),
        name=name,
    )(a, w, h)


def _lambda(lamp_ref, lam_init):
    lp = lamp_ref[...]
    s1 = jnp.sum(lp[0:1, :] * lp[1:2, :], axis=-1, keepdims=True)
    s2 = jnp.sum(lp[2:3, :] * lp[3:4, :], axis=-1, keepdims=True)
    return jnp.exp(s1) - jnp.exp(s2) + lam_init


BF16_SUBLANES = 16
V_AUG = A_VD + BF16_SUBLANES


LOOKAHEAD = 5


def _attn_prompt_kernel(q_ref, qn_ref, k_ref, v_ref, g_ref, lamp_ref, sg_ref, o_ref,
                        qpad_ref, vt_ref, m_ref, acc_ref, s_ref, mx_ref, *, tq, tk, lam_init):
    i = pl.program_id(1)
    slot = i % 2

    sw = s_ref.shape[2]
    n_strips = 2 * tq // sw
    la = min(LOOKAHEAD, n_strips - 1)
    order = sorted(range(n_strips), key=lambda c: -((c * sw) % tq))

    def load_queries(src_ref, dst):
        qt = src_ref[...].astype(F32).T
        row = lax.broadcasted_iota(jnp.int32, qt.shape, 0)
        qpad_ref[dst, :, :tq] = jnp.where(row < A_HD, qt, 0.0).astype(BF16)
        qpad_ref[dst, :, tq:] = jnp.where(row >= A_HD, qt, 0.0).astype(BF16)

    def scores(j, c, rows=tk, src=slot):
        kj = k_ref[pl.ds(pl.multiple_of(j * tk, tk), rows), :]
        s = jnp.dot(kj, qpad_ref[src, :, c * sw:(c + 1) * sw], preferred_element_type=F32)
        s_ref[c, :rows] = s
        mx_ref[c] = jnp.max(s, axis=0, keepdims=True)

    @pl.when(i == 0)
    def _():
        aug = lax.broadcasted_iota(jnp.int32, (V_AUG - A_VD, tk), 0)
        ones_row = jnp.where(aug == 0, 1.0, 0.0).astype(BF16)

        def fill(t, carry):
            vj = v_ref[pl.ds(pl.multiple_of(t * tk, tk), tk), :].astype(F32)
            vt_ref[t, :A_VD, :] = vj.T.astype(BF16)
            vt_ref[t, A_VD:, :] = ones_row
            return carry

        lax.fori_loop(0, vt_ref.shape[0], fill, 0)
        load_queries(q_ref, 0)
        for c in order[:la]:
            scores(0, c, src=0)

    m_ref[...] = jnp.full(m_ref.shape, NEG_BIG, F32)
    acc_ref[...] = jnp.zeros(acc_ref.shape, F32)

    def update(j, c, mask, rows=tk):
        cs = slice(c * sw, (c + 1) * sw)
        s = s_ref[c, :rows]
        if mask is None:
            mx = mx_ref[c]
        else:
            s = jnp.where(mask[:rows], s, NEG_BIG)
            mx = jnp.max(s, axis=0, keepdims=True)
        m_old = m_ref[:, cs]
        m_new = jnp.maximum(m_old, mx)
        alpha = jnp.exp2(m_old - m_new)
        p = jnp.exp2(s - m_new).astype(BF16)
        pv = jnp.dot(vt_ref[j][:, :rows], p, preferred_element_type=F32)
        acc_ref[:, cs] = alpha * acc_ref[:, cs] + pv
        m_ref[:, cs] = m_new

    n_sub = tq // tk
    n_full = i * n_sub

    def full_tiles(j0, count):
        for j in range(j0, j0 + count) if isinstance(j0, int) else [j0 + t for t in range(count)]:
            for n, c in enumerate(order):
                update(j, c, None)
                if n + la < n_strips:
                    scores(j, order[n + la])
                else:
                    scores(j + 1, order[n + la - n_strips])

    odd = i % 2

    @pl.when(odd == 1)
    def _():
        full_tiles(0, n_sub)

    def body(jj, carry):
        full_tiles(odd * n_sub + jj * (2 * n_sub), 2 * n_sub)
        return carry

    lax.fori_loop(0, i // 2, body, 0)

    r = lax.broadcasted_iota(jnp.int32, (tk, sw), 0)
    cc = lax.broadcasted_iota(jnp.int32, (tk, sw), 1)
    units = []
    for d in range(n_sub):
        for c in order:
            q_lo, k_lo = (c * sw) % tq, d * tk
            if q_lo + sw <= k_lo:
                continue
            full = q_lo >= k_lo + tk
            rows = min(tk, q_lo + sw - k_lo)
            units.append((d, c, None if full else (k_lo + r) // CHUNK <= (q_lo + cc) // CHUNK, rows))
    assert [u[:2] for u in units[:la]] == [(0, c) for c in order[:la]]
    assert all(u[2] is not None or u[3] == tk for u in units)
    load_queries(qn_ref, 1 - slot)
    for n, (d, c, mask, rows) in enumerate(units):
        update(n_full + d, c, mask, rows)
        if n + la < len(units):
            nd, nc, _, nrows = units[n + la]
            assert all(u[1] != nc for u in units[n + 1:n + la])
            scores(n_full + nd, nc, nrows)
    for c in order[:la]:
        scores(0, c, src=1 - slot)

    lam = _lambda(lamp_ref, lam_init)
    acc = acc_ref[...]
    inv_l = 1.0 / acc[A_VD:A_VD + 1, :]
    ot = acc[:A_VD, :tq] * inv_l[:, :tq] - lam * (acc[:A_VD, tq:] * inv_l[:, tq:])
    ms = jnp.mean(ot * ot, axis=0, keepdims=True)
    ot = ot * lax.rsqrt(ms + EPS) * (sg_ref[...] * (1.0 - lam_init))
    g = g_ref[...].astype(F32)
    o_ref[...] = (ot.T * (g * _sigmoid(g))).astype(o_ref.dtype)


def _attn_prompt(q, k, v, g, lamp, sub_gain, lam_init, *, name):
    s, w = q.shape
    heads = w // A_VD
    tq = _tile(s, 1024)
    tk = tq
    sw = min(MXU_COLS, tq)
    nq = s // tq
    return pl.pallas_call(
        functools.partial(_attn_prompt_kernel, tq=tq, tk=tk, lam_init=lam_init),
        out_shape=jax.ShapeDtypeStruct((s, w), BF16),
        grid=(heads, nq),
        in_specs=[pl.BlockSpec((tq, A_VD), lambda h, i: (i, h)),
                  pl.BlockSpec((tq, A_VD), lambda h, i: (jnp.minimum(i + 1, nq - 1), h)),
                  pl.BlockSpec((s, A_VD), lambda h, i: (0, h)),
                  pl.BlockSpec((s, A_VD), lambda h, i: (0, h)),
                  pl.BlockSpec((tq, A_VD), lambda h, i: (i, h)),
                  pl.BlockSpec((4, A_HD), lambda h, i: (0, 0)),
                  pl.BlockSpec((A_VD, 1), lambda h, i: (0, 0))],
        out_specs=pl.BlockSpec((tq, A_VD), lambda h, i: (i, h)),
        scratch_shapes=[pltpu.VMEM((2, A_VD, 2 * tq), BF16),
                        pltpu.VMEM((s // tk, V_AUG, tk), BF16),
                        pltpu.VMEM((1, 2 * tq), F32),
                        pltpu.VMEM((V_AUG, 2 * tq), F32),
                        pltpu.VMEM((2 * tq // sw, tk, sw), F32),
                        pltpu.VMEM((2 * tq // sw, 1, sw), F32)],
        compiler_params=_params(("parallel", "arbitrary")),
        cost_estimate=_cost(heads * (s * s // 2) * 2 * 2 * (A_VD + V_AUG), _nbytes(q, k, v, g, q),
                            heads * (s * s // 2) * 2),
        name=name,
    )(q, q, k, v, g, lamp, sub_gain.reshape(A_VD, 1))


SAMPLE_HEADS_PER_STEP = 4


def _attn_sample_kernel(q_ref, kn_ref, vn_ref, g_ref, kc_ref, vc_ref, lamp_ref, sg_ref, o_ref, *, lam_init, hps):
    lam = _lambda(lamp_ref, lam_init)
    nt = (((1,), (1,)), ((), ()))
    lane = lax.broadcasted_iota(jnp.int32, (q_ref.shape[0], A_VD), 1)
    heads = [slice(h * A_VD, (h + 1) * A_VD) for h in range(hps)]
    scores = []
    for hs in heads:
        q = q_ref[:, hs]
        kc = kc_ref[0, :, hs].astype(BF16)
        kn = kn_ref[:, hs].astype(BF16)
        for half in range(2):
            qh = jnp.where((lane >= A_HD) == bool(half), q, jnp.zeros_like(q))
            scores.append((lax.dot_general(qh, kc, nt, preferred_element_type=F32),
                           lax.dot_general(qh, kn, nt, preferred_element_type=F32)))
    probs = []
    for sc, sn in scores:
        m = jnp.maximum(jnp.max(sc, axis=-1, keepdims=True), jnp.max(sn, axis=-1, keepdims=True))
        pc = jnp.exp2(sc - m)
        pn = jnp.exp2(sn - m)
        inv = 1.0 / (jnp.sum(pc, axis=-1, keepdims=True) + jnp.sum(pn, axis=-1, keepdims=True))
        probs.append((pc * inv, pn * inv))
    outs = []
    for h, hs in enumerate(heads):
        ac = (probs[2 * h][0] - lam * probs[2 * h + 1][0]).astype(BF16)
        an = (probs[2 * h][1] - lam * probs[2 * h + 1][1]).astype(BF16)
        outs.append(jnp.dot(ac, vc_ref[0, :, hs].astype(BF16), preferred_element_type=F32)
                    + jnp.dot(an, vn_ref[:, hs].astype(BF16), preferred_element_type=F32))
    for hs, o in zip(heads, outs):
        ms = jnp.mean(o * o, axis=-1, keepdims=True)
        g = g_ref[:, hs].astype(F32)
        o = o * lax.rsqrt(ms + EPS) * (sg_ref[...] * (1.0 - lam_init))
        o_ref[:, hs] = (o * (g * _sigmoid(g))).astype(o_ref.dtype)


def _attn_sample(q, k, v, g, kc, vc, layer, lamp, sub_gain, lam_init, *, name):
    _, bsz, past, w = kc.shape
    heads = w // A_VD
    ln = q.shape[0] // bsz
    assert past % CHUNK == 0 and ln <= CHUNK
    hps = SAMPLE_HEADS_PER_STEP
    assert heads % hps == 0
    row = lambda b, h: (b, h)
    cache = lambda b, h: (layer, b, 0, h)
    return pl.pallas_call(
        functools.partial(_attn_sample_kernel, lam_init=lam_init, hps=hps),
        out_shape=jax.ShapeDtypeStruct(q.shape, BF16),
        grid=(bsz, heads // hps),
        in_specs=[pl.BlockSpec((ln, hps * A_VD), row),
                  pl.BlockSpec((ln, hps * A_VD), row),
                  pl.BlockSpec((ln, hps * A_VD), row),
                  pl.BlockSpec((ln, hps * A_VD), row),
                  pl.BlockSpec((None, 1, past, hps * A_VD), cache),
                  pl.BlockSpec((None, 1, past, hps * A_VD), cache),
                  pl.BlockSpec((4, A_HD), lambda b, h: (0, 0)),
                  pl.BlockSpec((1, A_VD), lambda b, h: (0, 0))],
        out_specs=pl.BlockSpec((ln, hps * A_VD), row),
        compiler_params=_params(("parallel", "parallel")),
        cost_estimate=_cost(bsz * heads * ln * (past + ln) * 2 * 2 * 2 * A_VD,
                            _nbytes(q, k, v, g, q) + _nbytes(kc, vc) // kc.shape[0],
                            bsz * heads * ln * (past + ln) * 2),
        name=name,
    )(q, k, v, g, kc, vc, lamp, sub_gain.reshape(1, A_VD))


RET_HEADS_PER_STEP = 8


def _retention_kernel(q_ref, k_ref, v_ref, g_ref, s0_ref, dec_ref, cd_ref, kd_ref, gl_ref,
                      o_ref, sout_ref, st_ref, *, lc, hps):
    c = pl.program_id(2)

    @pl.when(c == 0)
    def _():
        st_ref[...] = s0_ref[0]

    hd = range(hps)
    qs = [q_ref[:, h * R_DK:(h + 1) * R_DK] for h in hd]
    ks = [k_ref[:, h * R_DK:(h + 1) * R_DK] for h in hd]
    vs = [v_ref[:, h * R_DV:(h + 1) * R_DV] for h in hd]
    sts = [st_ref[h] for h in hd]
    nt = (((1,), (1,)), ((), ()))
    inner = [lax.dot_general(qs[h], ks[h], nt, preferred_element_type=F32) for h in hd]
    cross = [jnp.dot(qs[h], sts[h].astype(BF16), preferred_element_type=F32) for h in hd]
    kdec = []
    for h in hd:
        kd = ks[h].astype(F32) * jnp.concatenate([kd_ref[h]] * (R_DK // LANES), axis=1)
        if lc < LANES:
            kd = jnp.concatenate([kd, jnp.zeros((LANES - lc, R_DK), F32)], axis=0)
        kdec.append(kd.T.astype(BF16))
    inner = [(inner[h] * dec_ref[h]).astype(BF16) for h in hd]
    o = [jnp.dot(inner[h], vs[h], preferred_element_type=F32) for h in hd]
    upd = []
    for h in hd:
        vv = vs[h]
        if lc < LANES:
            vv = jnp.concatenate([vv, jnp.zeros((LANES - lc, R_DV), BF16)], axis=0)
        upd.append(jnp.dot(kdec[h], vv, preferred_element_type=F32))
    for h in hd:
        cd = jnp.concatenate([cd_ref[h]] * (R_DV // LANES), axis=1)
        oh = o[h] + cross[h] * cd
        ms = jnp.mean(oh * oh, axis=-1, keepdims=True)
        g = g_ref[:, h * R_DV:(h + 1) * R_DV].astype(F32)
        o_ref[:, h * R_DV:(h + 1) * R_DV] = (oh * lax.rsqrt(ms + EPS) * (g * _sigmoid(g))).astype(o_ref.dtype)
    st_new = []
    for h in hd:
        gl = jnp.concatenate([gl_ref[h]] * (R_DV // LANES), axis=1)
        st_new.append(sts[h] * gl + upd[h])
        st_ref[h] = st_new[h]

    @pl.when(c == pl.num_programs(2) - 1)
    def _():
        for h in hd:
            sout_ref[0, h] = st_new[h]


def _retention(q, k, vg, s0, lc, *, name):
    bsz, heads = s0.shape[:2]
    t = q.shape[0] // bsz
    nc = t // lc
    lg = jnp.log1p(-(2.0 ** (-5.0 - jnp.arange(heads, dtype=F32))))
    idx = jnp.arange(lc, dtype=F32)
    diff = idx[:, None] - idx[None, :]
    dec = jnp.where(diff >= 0, jnp.exp(lg[:, None, None] * jnp.maximum(diff, 0.0)), 0.0)
    cd = jnp.broadcast_to(jnp.exp(lg[:, None] * (idx[None, :] + 1.0))[:, :, None], (heads, lc, LANES))
    kd = jnp.broadcast_to(jnp.exp(lg[:, None] * (lc - 1.0 - idx[None, :]))[:, :, None], (heads, lc, LANES))
    gl = jnp.broadcast_to(jnp.exp(lg * lc)[:, None, None], (heads, 1, LANES))
    hps = RET_HEADS_PER_STEP
    assert heads % hps == 0
    rows = lambda b, h, c: (b * nc + c, h)
    gate = lambda b, h, c: (b * nc + c, heads // hps + h)
    tab = lambda b, h, c: (h, 0, 0)
    state = lambda b, h, c: (b, h, 0, 0)
    return pl.pallas_call(
        functools.partial(_retention_kernel, lc=lc, hps=hps),
        out_shape=(jax.ShapeDtypeStruct((vg.shape[0], heads * R_DV), BF16),
                   jax.ShapeDtypeStruct(s0.shape, F32)),
        grid=(bsz, heads // hps, nc),
        in_specs=[pl.BlockSpec((lc, hps * R_DK), rows),
                  pl.BlockSpec((lc, hps * R_DK), rows),
                  pl.BlockSpec((lc, hps * R_DV), rows),
                  pl.BlockSpec((lc, hps * R_DV), gate),
                  pl.BlockSpec((1, hps, R_DK, R_DV), state),
                  pl.BlockSpec((hps, lc, lc), tab),
                  pl.BlockSpec((hps, lc, LANES), tab),
                  pl.BlockSpec((hps, lc, LANES), tab),
                  pl.BlockSpec((hps, 1, LANES), tab)],
        out_specs=(pl.BlockSpec((lc, hps * R_DV), rows),
                   pl.BlockSpec((1, hps, R_DK, R_DV), state)),
        scratch_shapes=[pltpu.VMEM((hps, R_DK, R_DV), F32)],
        compiler_params=_params(("parallel", "parallel", "arbitrary")),
        cost_estimate=_cost(2 * q.shape[0] * heads * (lc * (R_DK + R_DV) + 2 * R_DK * R_DV),
                            _nbytes(q, k, vg, s0, s0) + q.shape[0] * heads * R_DV * 2,
                            q.shape[0] * heads * R_DV),
        name=name,
    )(q, k, vg, vg, s0, dec, cd, kd, gl)


def _cmlp_kernel(u_ref, g_ref, v_ref, vg_ref, w_ref, b_ref, *out_refs, emit_v):
    if emit_v:
        a_ref, vn_ref = out_refs
    else:
        (a_ref,) = out_refs
    va = _gelu(v_ref[...])
    ms = jnp.mean(va * va, axis=-1, keepdims=True)
    vn = va * lax.rsqrt(ms + EPS) * vg_ref[...]
    if emit_v:
        vn_ref[...] = vn
    vb = vn.astype(BF16)
    gd = vn.shape[1] // M_GROUPS
    for grp in range(M_GROUPS):
        sl = slice(grp * gd, (grp + 1) * gd)
        bias = jnp.concatenate([b_ref[grp]] * (gd // LANES), axis=1)
        mix = jnp.dot(w_ref[grp], vb[:, sl], preferred_element_type=F32) + bias
        g = g_ref[:, sl].astype(F32)
        a_ref[:, sl] = (_gelu(u_ref[:, sl].astype(F32)) * mix * (g * _sigmoid(g))).astype(a_ref.dtype)


def _cmlp(ug, v, v_gain, wmix, bmix, *, emit_v, name):
    m, w = v.shape
    t = wmix.shape[1]
    out_shape = [jax.ShapeDtypeStruct((m, w), BF16)]
    out_specs = [pl.BlockSpec((t, w), lambda i: (i, 0))]
    if emit_v:
        out_shape.append(jax.ShapeDtypeStruct((m, w), F32))
        out_specs.append(pl.BlockSpec((t, w), lambda i: (i, 0)))
    res = pl.pallas_call(
        functools.partial(_cmlp_kernel, emit_v=emit_v),
        out_shape=tuple(out_shape),
        grid=(m // t,),
        in_specs=[pl.BlockSpec((t, w), lambda i: (i, 0)),
                  pl.BlockSpec((t, w), lambda i: (i, 1)),
                  pl.BlockSpec((t, w), lambda i: (i, 0)),
                  pl.BlockSpec((1, w), lambda i: (0, 0)),
                  pl.BlockSpec((M_GROUPS, t, t), lambda i: (0, 0, 0)),
                  pl.BlockSpec((M_GROUPS, t, LANES), lambda i: (0, 0, 0))],
        out_specs=tuple(out_specs),
        compiler_params=_params(("parallel",)),
        cost_estimate=_cost(2 * m * t * w, _nbytes(ug, v) + sum(m * w * jnp.dtype(o.dtype).itemsize for o in out_shape),
                            3 * m * w),
        name=name,
    )(ug, ug, v, v_gain.reshape(1, w), wmix, bmix)
    return res if emit_v else res[0]


def _rope_tables(pos, d, signed):
    inv = ROPE_THETA ** (-jnp.arange(0, d, 2, dtype=F32) / d)
    ang = pos.astype(F32)[:, None] * inv[None, :]
    cos, sin = jnp.cos(ang), jnp.sin(ang)
    reps = LANES // (d // 2)
    if signed:
        return (jnp.tile(cos, (1, reps)),
                jnp.tile(jnp.concatenate([-sin, sin], axis=1), (1, reps // 2)))
    return jnp.tile(cos, (1, reps)), jnp.tile(sin, (1, reps))


def _diff_attn_layer(h, pos, nw, w_in, w_out, q_gain, k_gain, lamp, sub_gain, lam_init,
                     layers, layer, kbuf, vbuf, cache, tag):
    w = w_out.shape[0]
    cos, sin = _rope_tables(pos, A_HD, signed=True)
    wq, wk, wv, wg = (w_in[:, i * w:(i + 1) * w].astype(BF16) for i in range(4))
    (q,), (k, kb), (v, vb), (g,) = _proj(h, nw, [
        _seg(wq, (BF16,), "qk", (A_HD ** -0.5) * LOG2E, q_gain),
        _seg(wk, (F32, BF16), "qk", gain=k_gain, stack=(layers, layer, kbuf)),
        _seg(wv, (F32, BF16), stack=(layers, layer, vbuf)),
        _seg(wg, (BF16,))], cos=cos, sin=sin, name=f"{tag}_proj")
    if cache is None:
        o = _attn_prompt(q, kb, vb, g, lamp, sub_gain, lam_init, name=f"{tag}_attn")
    else:
        kc, vc = cache
        o = _attn_sample(q, kb, vb, g, kc, vc, layer, lamp, sub_gain, lam_init, name=f"{tag}_attn")
    return _outproj(o, w_out.astype(BF16), h, name=f"{tag}_out"), k, v


def _retention_layer(h, pos, nw, w_in, w_out, s0, lc, tag):
    heads = s0.shape[1]
    qk_w, v_w = heads * R_DK, heads * R_DV
    cos, sin = _rope_tables(pos, R_DK, signed=False)
    wq = w_in[:, :qk_w].astype(BF16)
    wk = w_in[:, qk_w:2 * qk_w].astype(BF16)
    assert w_in.shape[1] == 2 * qk_w + 2 * v_w
    wvg = w_in[:, 2 * qk_w:].astype(BF16)
    (q,), (k,), (vg,) = _proj(h, nw, [
        _seg(wq, (BF16,), "rope256"),
        _seg(wk, (BF16,), "rope256", R_DK ** -0.5),
        _seg(wvg, (BF16,))], cos=cos, sin=sin, name=f"{tag}_proj")
    o, s_new = _retention(q, k, vg, s0, lc, name=f"{tag}_ret")
    return _outproj(o, w_out.astype(BF16), h, name=f"{tag}_out"), s_new


def _cmlp_layer(h, nw, w_in, w_out, v_gain, wmix, bmix, emit_v, tag):
    w = w_out.shape[0]
    wu, wv, wg = (w_in[:, i * w:(i + 1) * w].astype(BF16) for i in range(3))
    (ug,), (v,) = _proj(h, nw, [_seg(jnp.concatenate([wu, wg], axis=1), (BF16,)), _seg(wv, (F32,))],
                        name=f"{tag}_proj")
    res = _cmlp(ug, v, v_gain, wmix, bmix, emit_v=emit_v, name=f"{tag}_mix")
    a, vn = res if emit_v else (res, None)
    return _outproj(a, w_out.astype(BF16), h, name=f"{tag}_out"), vn


def _mix_tables(w_s, b_s, chunk_len):
    groups = w_s.shape[0]
    wl = jnp.tril(w_s[:, :chunk_len, :chunk_len])
    reps = M_CHUNK // chunk_len
    eye = jnp.eye(reps, dtype=F32)
    wt = jnp.einsum("ab,gij->gaibj", eye, wl).reshape(groups, M_CHUNK, M_CHUNK)
    bt = jnp.tile(b_s[:, :chunk_len], (1, reps))
    return wt.astype(BF16), jnp.broadcast_to(bt[:, :, None], (groups, M_CHUNK, LANES))


def kernel(x_prompt, x_sample, cache_k_attn, cache_v_attn, state_ret, norm_w, a_w_in, a_w_out, a_q_gain, a_k_gain, a_lam_q1, a_lam_k1, a_lam_q2, a_lam_k2, a_sub_gain, r_w_in, r_w_out, c_w_in, c_w_out, c_v_gain, c_w_s, c_b_s):
    batch, s_len, d = x_prompt.shape
    dec_b, dec_len, _ = x_sample.shape
    past = cache_k_attn.shape[2]
    depth = norm_w.shape[0]
    assert batch == 1 and M_CHUNK % dec_len == 0 and s_len % M_CHUNK == 0

    hp = x_prompt.reshape(s_len, d)
    hs = x_sample.reshape(dec_b * dec_len, d)
    pos_p = jnp.arange(s_len, dtype=jnp.int32)
    pos_s = jnp.tile(past + jnp.arange(dec_len, dtype=jnp.int32), dec_b)

    n_a = a_w_in.shape[0]
    aw = a_w_out.shape[1]
    cache = (cache_k_attn.astype(BF16).reshape(n_a, dec_b, past, aw), cache_v_attn.reshape(n_a, dec_b, past, aw))
    kp = vp = kn = vn = None
    sp_l, ss_l, vm_l = [], [], []
    for i in range(depth):
        kind, j = i % N_MIXERS, i // N_MIXERS
        if kind == 0:
            lam_init = 0.8 - 0.6 * math.exp(-0.3 * i)
            lamp = jnp.stack([a_lam_q1[j], a_lam_k1[j], a_lam_q2[j], a_lam_k2[j]])
            args = (norm_w[i], a_w_in[j], a_w_out[j], a_q_gain[j], a_k_gain[j], lamp, a_sub_gain[j], lam_init)
            hp, kp, vp = _diff_attn_layer(hp, pos_p, *args, n_a, j, kp, vp, None, f"l{i}p")
            hs, kn, vn = _diff_attn_layer(hs, pos_s, *args, n_a, j, kn, vn, cache, f"l{i}s")
        elif kind == 1:
            heads = state_ret.shape[2]
            s0 = jnp.zeros((batch, heads, R_DK, R_DV), F32)
            hp, st_p = _retention_layer(hp, pos_p, norm_w[i], r_w_in[j], r_w_out[j], s0,
                                        _tile(s_len, 256), f"l{i}p")
            hs, st_s = _retention_layer(hs, pos_s, norm_w[i], r_w_in[j], r_w_out[j],
                                        state_ret[j].astype(F32), dec_len, f"l{i}s")
            sp_l.append(st_p)
            ss_l.append(st_s)
        else:
            wp, bp = _mix_tables(c_w_s[j], c_b_s[j], M_CHUNK)
            ws, bs = _mix_tables(c_w_s[j], c_b_s[j], dec_len)
            hp, _ = _cmlp_layer(hp, norm_w[i], c_w_in[j], c_w_out[j], c_v_gain[j], wp, bp, False, f"l{i}p")
            hs, v_s = _cmlp_layer(hs, norm_w[i], c_w_in[j], c_w_out[j], c_v_gain[j], ws, bs, True, f"l{i}s")
            vm_l.append(v_s.reshape(dec_b, dec_len, -1))

    return (hp.reshape(batch, s_len, d), hs.reshape(dec_b, dec_len, d),
            kp.reshape(n_a, batch, s_len, aw // A_HD, A_HD), vp.reshape(n_a, batch, s_len, aw // A_VD, A_VD),
            kn.reshape(n_a, dec_b, dec_len, aw // A_HD, A_HD), vn.reshape(n_a, dec_b, dec_len, aw // A_VD, A_VD),
            jnp.stack(sp_l), jnp.stack(ss_l), jnp.stack(vm_l))
```
